```python
import jax, jax.numpy as jnp
from jax import lax
import numpy as np

D_MODEL = 4096
BATCH = 4
SEQ = 2048
DEPTH = 1
DEC_BATCH = 128
DEC_SEQ = 4
PAST_LEN = 2048
PAGE_SIZE = 128

HEAD_DIM = 128
N_MIX_HEADS = D_MODEL // HEAD_DIM
N_CONV_GROUPS = N_MIX_HEADS // 2
CONV_DIM = N_CONV_GROUPS * HEAD_DIM
N_Q_HEADS = N_MIX_HEADS - N_CONV_GROUPS
N_KV_HEADS = max(1, N_Q_HEADS // 4)
GQA = N_Q_HEADS // N_KV_HEADS
ATTN_DIM = N_Q_HEADS * HEAD_DIM
KV_DIM = N_KV_HEADS * HEAD_DIM
W_IN_COLS = 3 * CONV_DIM + ATTN_DIM + 6 * KV_DIM + 3 * N_Q_HEADS
CONV_WIDTH = 3
L_CMP = 32
CMP_STRIDE = 16
R_CMP = L_CMP // CMP_STRIDE
D_PHI = HEAD_DIM
L_SEL = 64
N_SEL = 8
WINDOW = 512
Q_BLK = 128
SEL_Q_BLK = 32
FORCE_BONUS = 1e3
SCALE = HEAD_DIM ** -0.5
N_GROUPS = 4
EXPERTS_PER_GROUP = 4
N_EXPERTS = N_GROUPS * EXPERTS_PER_GROUP
TOP_K = 2
D_FF_EXPERT = D_MODEL // 8
RMS_EPS = 1e-6
NEG_INF = -1e30
TINY = 1e-30

kernel_name = 'hymba_conv_nsa_hmoe_step'


def rmsnorm(x, g):
    xf = x.astype(jnp.float32)
    inv = lax.rsqrt(jnp.mean(xf * xf, axis=-1, keepdims=True) + RMS_EPS)
    return (xf * inv).astype(x.dtype) * g


def masked_softmax(s, mask):
    s = jnp.where(mask, s.astype(jnp.float32), NEG_INF)
    m = jnp.max(s, axis=-1, keepdims=True)
    e = jnp.where(mask, jnp.exp(s - m), 0.0)
    return e / jnp.maximum(jnp.sum(e, axis=-1, keepdims=True), TINY)


def qk_norm_keys(kv, g):
    return jnp.stack([rmsnorm(kv[:, :, 0], g), kv[:, :, 1]], axis=2)


def mix_project(x, lw):
    n, t = x.shape[:2]
    z = rmsnorm(x, lw['norm_mix_g']) @ lw['w_in']
    sizes = (CONV_DIM, CONV_DIM, CONV_DIM, ATTN_DIM, 2 * KV_DIM, 2 * KV_DIM, 2 * KV_DIM, 3 * N_Q_HEADS)
    cuts = [int(c) for c in np.cumsum(sizes)[:-1]]
    gb, gc, hc, q, kvc, kvs, kvw, g = jnp.split(z, cuts, axis=-1)
    q = rmsnorm(q.reshape(n, t, N_Q_HEADS, HEAD_DIM), lw['q_norm_g'])
    kv_shape = (n, t, 2, N_KV_HEADS, HEAD_DIM)
    kvc = kvc.reshape(kv_shape)
    kvs = qk_norm_keys(kvs.reshape(kv_shape), lw['k_norm_g'][1])
    kvw = qk_norm_keys(kvw.reshape(kv_shape), lw['k_norm_g'][2])
    gates = jax.nn.sigmoid(g.reshape(n, t, 3, N_Q_HEADS))
    return gb, gc, hc, q, kvc, kvs, kvw, gates


def short_conv(b, c, h, prefix, conv_w):
    u = c * h
    t = u.shape[1]
    up = jnp.concatenate([prefix.astype(u.dtype), u], axis=1)
    y = up[:, 0:t] * conv_w[0]
    for j in range(1, CONV_WIDTH):
        y = y + up[:, j:j + t] * conv_w[j]
    return b * y, up[:, t:]


def chunk_contrib(rows, w1):
    n, t = rows.shape[:2]
    ch = rows.reshape(n, t // CMP_STRIDE, CMP_STRIDE, 2, N_KV_HEADS, HEAD_DIM)
    w1r = w1.reshape(2, R_CMP, CMP_STRIDE, HEAD_DIM, D_PHI)
    return jnp.einsum('ncsvjd,vrsde->ncrvje', ch, w1r)


def compress_blocks(contrib, lw):
    nb = contrib.shape[1] - R_CMP + 1
    hpre = contrib[:, 0:nb, 0]
    for r in range(1, R_CMP):
        hpre = hpre + contrib[:, r:r + nb, r]
    pe_bias = jnp.einsum('vld,vlde->ve', lw['phi_pe'], lw['phi_w1'])
    hid = jax.nn.gelu(hpre + pe_bias[:, None, :])
    out = jnp.einsum('nbvje,ved->nbvjd', hid, lw['phi_w2'])
    return rmsnorm(out[:, :, 0], lw['k_norm_g'][0]), out[:, :, 1]


def cmp_attention(q, pos, k_c, v_c):
    n, t = q.shape[:2]
    nb = k_c.shape[1]
    qg = q.reshape(n, t, N_KV_HEADS, GQA, HEAD_DIM)
    s = jnp.einsum('nqkgd,nbkd->nkgqb', qg, k_c) * SCALE
    mask = (jnp.arange(nb) * CMP_STRIDE + L_CMP - 1)[None, :] <= pos[:, None]
    p = masked_softmax(s, mask)
    o = jnp.einsum('nkgqb,nbkd->nqkgd', p.astype(v_c.dtype), v_c)
    return o.reshape(n, t, N_Q_HEADS, HEAD_DIM), p


def block_cover_matrix(nb, nsb):
    i = np.arange(nb)[:, None]
    j = np.arange(nsb)[None, :]
    m = np.zeros((nb, nsb), np.float32)
    for a in range(L_SEL // CMP_STRIDE):
        for c in range(R_CMP):
            m += (i == (L_SEL // CMP_STRIDE) * j + a - c)
    return jnp.asarray(m)


def select_blocks(p_cmp, pos, nsb):
    imp = jnp.einsum('nkgqb,bj->nkqj', p_cmp, block_cover_matrix(p_cmp.shape[-1], nsb))
    blk = jnp.arange(nsb)[None, :]
    cur = (pos // L_SEL)[:, None]
    forced = (blk == 0) | (blk == cur) | (blk == cur - 1)
    score = jnp.where(blk <= cur, imp + jnp.where(forced, FORCE_BONUS, 0.0), NEG_INF)
    return lax.top_k(score, min(N_SEL, nsb))[1]


def sel_attention(q, pos, idx, k_g, v_g):
    n, t = q.shape[:2]
    ns = idx.shape[-1]
    qg = q.reshape(n, t, N_KV_HEADS, GQA, HEAD_DIM)
    s = jnp.einsum('nqkgd,nkqsld->nkgqsl', qg, k_g) * SCALE
    kpos = idx[..., None] * L_SEL + jnp.arange(L_SEL)
    mask = (kpos <= pos[:, None, None]).reshape(n, N_KV_HEADS, 1, t, ns * L_SEL)
    p = masked_softmax(s.reshape(n, N_KV_HEADS, GQA, t, ns * L_SEL), mask)
    o = jnp.einsum('nkgqsl,nkqsld->nqkgd', p.reshape(s.shape).astype(v_g.dtype), v_g)
    return o.reshape(n, t, N_Q_HEADS, HEAD_DIM)


def window_attention_prompt(q, kv_win):
    n, t = q.shape[:2]
    nqb, nw = t // Q_BLK, WINDOW // Q_BLK
    padded = jnp.pad(kv_win, ((0, 0), (WINDOW, 0), (0, 0), (0, 0), (0, 0)))
    blocks = padded.reshape(n, nqb + nw, Q_BLK, 2, N_KV_HEADS, HEAD_DIM)
    band = jnp.concatenate([blocks[:, j:j + nqb] for j in range(nw + 1)], axis=2)
    qg = q.reshape(n, nqb, Q_BLK, N_KV_HEADS, GQA, HEAD_DIM)
    s = jnp.einsum('biqkgd,bimkd->bkgiqm', qg, band[:, :, :, 0]) * SCALE
    qpos = jnp.arange(t).reshape(nqb, Q_BLK)[:, :, None]
    kpos = ((jnp.arange(nqb) - nw) * Q_BLK)[:, None, None] + jnp.arange((nw + 1) * Q_BLK)[None, None, :]
    mask = (kpos >= 0) & (kpos <= qpos) & (kpos > qpos - WINDOW)
    p = masked_softmax(s, mask)
    o = jnp.einsum('bkgiqm,bimkd->biqkgd', p.astype(band.dtype), band[:, :, :, 1])
    return o.reshape(n, t, N_Q_HEADS, HEAD_DIM)


def combine_branches(gates, o_cmp, o_sel, o_win):
    return (gates[:, :, 0, :, None] * o_cmp + gates[:, :, 1, :, None] * o_sel
            + gates[:, :, 2, :, None] * o_win)


def nsa_prompt(q, kvc, kvs, kvw, gates, lw):
    n, t = q.shape[:2]
    pos = jnp.arange(t)
    k_c, v_c = compress_blocks(chunk_contrib(kvc, lw['phi_w1']), lw)
    o_cmp, p_cmp = cmp_attention(q, pos, k_c, v_c)
    nsb = t // L_SEL
    idx = select_blocks(p_cmp, pos, nsb)
    ns = idx.shape[-1]
    kvb = kvs.reshape(n, nsb, L_SEL, 2, N_KV_HEADS, HEAD_DIM)
    b_idx = jnp.arange(n)[:, None, None, None]
    h_idx = jnp.arange(N_KV_HEADS)[None, :, None, None]

    def sel_block(args):
        qb, ib, pb = args
        g = kvb[b_idx, ib, :, :, h_idx]
        return sel_attention(qb, pb, ib, g[..., 0, :], g[..., 1, :])

    nqb = t // SEL_Q_BLK
    qbs = q.reshape(n, nqb, SEL_Q_BLK, N_Q_HEADS, HEAD_DIM).transpose(1, 0, 2, 3, 4)
    ibs = idx.reshape(n, N_KV_HEADS, nqb, SEL_Q_BLK, ns).transpose(2, 0, 1, 3, 4)
    pbs = pos.reshape(nqb, SEL_Q_BLK)
    o_sel = lax.map(sel_block, (qbs, ibs, pbs)).transpose(1, 0, 2, 3, 4).reshape(n, t, N_Q_HEADS, HEAD_DIM)
    o_win = window_attention_prompt(q, kvw)
    return combine_branches(gates, o_cmp, o_sel, o_win)


def nsa_sample(q, kvc, kvs, kvw, gates, cache_cmp, cache_sel, win_buf, page_table, lw):
    n, t = q.shape[:2]
    pos = PAST_LEN + jnp.arange(t)
    qg = q.reshape(n, t, N_KV_HEADS, GQA, HEAD_DIM)
    past_cmp = cache_cmp[page_table].reshape(n, -1, 2, N_KV_HEADS, HEAD_DIM)
    contrib = chunk_contrib(past_cmp, lw['phi_w1'])
    n_new_ch = t // CMP_STRIDE
    if n_new_ch > 0:
        contrib = jnp.concatenate([contrib, chunk_contrib(kvc[:, :n_new_ch * CMP_STRIDE], lw['phi_w1'])], axis=1)
    k_c, v_c = compress_blocks(contrib, lw)
    o_cmp, p_cmp = cmp_attention(q, pos, k_c, v_c)
    npb = PAST_LEN // L_SEL
    n_new_blk = -(-t // L_SEL)
    idx = select_blocks(p_cmp, pos, npb + n_new_blk)
    ns = idx.shape[-1]
    bpp = PAGE_SIZE // L_SEL
    pool_blk = cache_sel.reshape(cache_sel.shape[0], bpp, L_SEL, 2, N_KV_HEADS, HEAD_DIM)
    n_idx = jnp.arange(n)[:, None, None, None]
    h_idx = jnp.arange(N_KV_HEADS)[None, :, None, None]
    pidx = jnp.minimum(idx, npb - 1)
    phys = page_table[n_idx, pidx // bpp]
    g = pool_blk[phys, pidx % bpp, :, :, h_idx]
    s_past = (jnp.einsum('nqkgd,nkqsld->nkgqsl', qg, g[..., 0, :]) * SCALE).reshape(n, N_KV_HEADS, GQA, t, ns * L_SEL)
    kpos_p = idx[..., None] * L_SEL + jnp.arange(L_SEL)
    mask_past = ((idx < npb)[..., None] & (kpos_p <= pos[:, None, None])).reshape(n, N_KV_HEADS, t, ns * L_SEL)
    m_new = n_new_blk * L_SEL
    new_rows = jnp.pad(kvs, ((0, 0), (0, m_new - t), (0, 0), (0, 0), (0, 0)))
    new_blk_id = npb + jnp.arange(m_new) // L_SEL
    picked = jnp.any(idx[..., None] == new_blk_id, axis=-2)
    mask_new = picked & ((PAST_LEN + jnp.arange(m_new))[None, :] <= pos[:, None])
    s_new = jnp.einsum('nqkgd,nmkd->nkgqm', qg, new_rows[:, :, 0]) * SCALE
    p = masked_softmax(jnp.concatenate([s_past, s_new], axis=-1),
                       jnp.concatenate([mask_past, mask_new], axis=-1)[:, :, None])
    p_past = p[..., :ns * L_SEL].reshape(n, N_KV_HEADS, GQA, t, ns, L_SEL).astype(g.dtype)
    o_sel = (jnp.einsum('nkgqsl,nkqsld->nqkgd', p_past, g[..., 1, :])
             + jnp.einsum('nkgqm,nmkd->nqkgd', p[..., ns * L_SEL:].astype(new_rows.dtype), new_rows[:, :, 1]))
    o_sel = o_sel.reshape(n, t, N_Q_HEADS, HEAD_DIM)
    keys = jnp.concatenate([win_buf.astype(kvw.dtype), kvw], axis=1)
    w_buf = win_buf.shape[1]
    kpos = PAST_LEN - w_buf + jnp.arange(w_buf + t)
    mask = (kpos[None, :] <= pos[:, None]) & (kpos[None, :] > pos[:, None] - WINDOW)
    s = jnp.einsum('nqkgd,nmkd->nkgqm', qg, keys[:, :, 0]) * SCALE
    pw = masked_softmax(s, mask)
    o_win = jnp.einsum('nkgqm,nmkd->nqkgd', pw.astype(keys.dtype), keys[:, :, 1]).reshape(n, t, N_Q_HEADS, HEAD_DIM)
    return combine_branches(gates, o_cmp, o_sel, o_win), keys[:, -w_buf:]


def mix_output(x, conv_out, attn_out, lw):
    n, t = x.shape[:2]
    g = lw['out_norm_g']
    merged = jnp.concatenate([rmsnorm(conv_out, g[:CONV_DIM]),
                              rmsnorm(attn_out.reshape(n, t, ATTN_DIM), g[CONV_DIM:])], axis=-1)
    return x + merged @ lw['w_out']


def moe_ffn(h, lw):
    n, t, d = h.shape
    xt = rmsnorm(h, lw['norm_ffn_g']).reshape(n * t, d)
    p_grp = jax.nn.softmax((xt @ lw['w_group_router']).astype(jnp.float32) + lw['b_group_router'], axis=-1)
    g_val, g_idx = lax.top_k(p_grp, 1)
    e_logit = ((xt @ lw['w_expert_router']).astype(jnp.float32)
               + lw['b_expert_router']).reshape(n * t, N_GROUPS, EXPERTS_PER_GROUP)
    e_logit = jnp.take_along_axis(e_logit, g_idx[:, :, None], axis=1)[:, 0]
    e_val, e_idx = lax.top_k(jax.nn.softmax(e_logit, axis=-1), TOP_K)
    w = g_val * e_val / jnp.sum(e_val, axis=-1, keepdims=True)
    gate = jnp.einsum('tk,tke->te', w,
                      jax.nn.one_hot(g_idx * EXPERTS_PER_GROUP + e_idx, N_EXPERTS, dtype=jnp.float32))
    hid = jax.nn.silu(jnp.einsum('td,edf->tef', xt, lw['w_gate'])) * jnp.einsum('td,edf->tef', xt, lw['w_up'])
    out = jnp.einsum('tef,efd->td', hid * gate[:, :, None].astype(hid.dtype), lw['w_down'])
    return h + out.reshape(n, t, d)


def layer_prompt(x, lw):
    n, t = x.shape[:2]
    gb, gc, hc, q, kvc, kvs, kvw, gates = mix_project(x, lw)
    conv_out, conv_state = short_conv(gb, gc, hc, jnp.zeros((n, CONV_WIDTH - 1, CONV_DIM), x.dtype), lw['conv_w'])
    attn_out = nsa_prompt(q, kvc, kvs, kvw, gates, lw)
    y = moe_ffn(mix_output(x, conv_out, attn_out, lw), lw)
    return y, kvc, kvs, kvw[:, t - min(WINDOW, t):], conv_state


def layer_sample(x, cache_cmp, cache_sel, win_buf, conv_buf, page_table, lw):
    gb, gc, hc, q, kvc, kvs, kvw, gates = mix_project(x, lw)
    conv_out, conv_state = short_conv(gb, gc, hc, conv_buf, lw['conv_w'])
    attn_out, new_win = nsa_sample(q, kvc, kvs, kvw, gates, cache_cmp, cache_sel, win_buf, page_table, lw)
    y = moe_ffn(mix_output(x, conv_out, attn_out, lw), lw)
    return y, kvc, kvs, new_win, conv_state


def setup_inputs(seed: int = 0) -> dict:
    key = jax.random.key(seed)
    ks = jax.random.split(key, 26)
    f32 = jnp.float32

    def nrm(k, shape, scale=1.0):
        return jax.random.normal(k, shape, f32) * scale

    def gain(k, shape):
        return 1.0 + 0.01 * jax.random.normal(k, shape, f32)

    n_pages = PAST_LEN // PAGE_SIZE
    n_used = DEC_BATCH * n_pages
    n_pool = n_used + n_used // 4
    w_buf = min(WINDOW, PAST_LEN)
    page_table = jax.random.permutation(ks[6], n_pool)[:n_used].reshape(DEC_BATCH, n_pages).astype(jnp.int32)
    pool_shape = (DEPTH, n_pool, PAGE_SIZE, 2, N_KV_HEADS, HEAD_DIM)
    return {
        'x_prompt': nrm(ks[0], (BATCH, SEQ, D_MODEL)),
        'x_sample': nrm(ks[1], (DEC_BATCH, DEC_SEQ, D_MODEL)),
        'cache_cmp_kv': nrm(ks[2], pool_shape),
        'cache_sel_kv': nrm(ks[3], pool_shape),
        'state_win_kv': nrm(ks[4], (DEPTH, DEC_BATCH, w_buf, 2, N_KV_HEADS, HEAD_DIM)),
        'state_conv': nrm(ks[5], (DEPTH, DEC_BATCH, CONV_WIDTH - 1, CONV_DIM)),
        'page_table': page_table,
        'norm_mix_g': gain(ks[7], (DEPTH, D_MODEL)),
        'w_in': nrm(ks[8], (DEPTH, D_MODEL, W_IN_COLS), D_MODEL ** -0.5),
        'conv_w': nrm(ks[9], (DEPTH, CONV_WIDTH, CONV_DIM), CONV_WIDTH ** -0.5),
        'q_norm_g': gain(ks[10], (DEPTH, HEAD_DIM)),
        'k_norm_g': gain(ks[11], (DEPTH, 3, HEAD_DIM)),
        'phi_pe': nrm(ks[12], (DEPTH, 2, L_CMP, HEAD_DIM), 0.1),
        'phi_w1': nrm(ks[13], (DEPTH, 2, L_CMP, HEAD_DIM, D_PHI), (L_CMP * HEAD_DIM) ** -0.5),
        'phi_w2': nrm(ks[14], (DEPTH, 2, D_PHI, HEAD_DIM), D_PHI ** -0.5),
        'out_norm_g': gain(ks[15], (DEPTH, D_MODEL)),
        'w_out': nrm(ks[16], (DEPTH, D_MODEL, D_MODEL), D_MODEL ** -0.5),
        'norm_ffn_g': gain(ks[17], (DEPTH, D_MODEL)),
        'w_group_router': nrm(ks[18], (DEPTH, D_MODEL, N_GROUPS), D_MODEL ** -0.5),
        'b_group_router': nrm(ks[19], (DEPTH, N_GROUPS), 0.01),
        'w_expert_router': nrm(ks[20], (DEPTH, D_MODEL, N_EXPERTS), D_MODEL ** -0.5),
        'b_expert_router': nrm(ks[21], (DEPTH, N_EXPERTS), 0.01),
        'w_gate': nrm(ks[22], (DEPTH, N_EXPERTS, D_MODEL, D_FF_EXPERT), D_MODEL ** -0.5),
        'w_up': nrm(ks[23], (DEPTH, N_EXPERTS, D_MODEL, D_FF_EXPERT), D_MODEL ** -0.5),
        'w_down': nrm(ks[24], (DEPTH, N_EXPERTS, D_FF_EXPERT, D_MODEL), D_FF_EXPERT ** -0.5),
    }


def reference(x_prompt, x_sample, cache_cmp_kv, cache_sel_kv, state_win_kv, state_conv, page_table,
              norm_mix_g, w_in, conv_w, q_norm_g, k_norm_g, phi_pe, phi_w1, phi_w2, out_norm_g, w_out,
              norm_ffn_g, w_group_router, b_group_router, w_expert_router, b_expert_router,
              w_gate, w_up, w_down):
    y_prompt, y_sample = x_prompt, x_sample
    p_cmp, p_sel, p_win, p_conv = [], [], [], []
    s_cmp, s_sel, s_win, s_conv = [], [], [], []
    for l in range(DEPTH):
        lw = dict(norm_mix_g=norm_mix_g[l], w_in=w_in[l], conv_w=conv_w[l], q_norm_g=q_norm_g[l],
                  k_norm_g=k_norm_g[l], phi_pe=phi_pe[l], phi_w1=phi_w1[l], phi_w2=phi_w2[l],
                  out_norm_g=out_norm_g[l], w_out=w_out[l], norm_ffn_g=norm_ffn_g[l],
                  w_group_router=w_group_router[l], b_group_router=b_group_router[l],
                  w_expert_router=w_expert_router[l], b_expert_router=b_expert_router[l],
                  w_gate=w_gate[l], w_up=w_up[l], w_down=w_down[l])
        y_prompt, a, b, c, d = layer_prompt(y_prompt, lw)
        p_cmp.append(a); p_sel.append(b); p_win.append(c); p_conv.append(d)
        y_sample, a, b, c, d = layer_sample(y_sample, cache_cmp_kv[l], cache_sel_kv[l], state_win_kv[l],
                                            state_conv[l], page_table, lw)
        s_cmp.append(a); s_sel.append(b); s_win.append(c); s_conv.append(d)
    return (y_prompt, y_sample, jnp.stack(p_cmp), jnp.stack(p_sel), jnp.stack(p_win), jnp.stack(p_conv),
            jnp.stack(s_cmp), jnp.stack(s_sel), jnp.stack(s_win), jnp.stack(s_conv))
```

```python
import functools

import numpy as np
import jax
import jax.numpy as jnp
from jax import lax
from jax.experimental import pallas as pl
from jax.experimental.pallas import tpu as pltpu

F32 = jnp.float32
BF16 = jnp.bfloat16

D_MODEL = 4096
PAST_LEN = 2048
PAGE_SIZE = 128
HEAD_DIM = 128
CONV_DIM = 2048
N_Q_HEADS = 16
N_KV_HEADS = 4
GQA = 4
ATTN_DIM = 2048
KV_DIM = 512
CONV_WIDTH = 3
L_CMP = 32
CMP_STRIDE = 16
R_CMP = 2
L_SEL = 64
N_SEL = 8
WINDOW = 512
FORCE_BONUS = 1e3
SCALE = HEAD_DIM ** -0.5
N_GROUPS = 4
EXPERTS_PER_GROUP = 4
N_EXPERTS = 16
D_FF_EXPERT = 512
RMS_EPS = 1e-6
NEG_INF = -1e30
TINY = 1e-30
PICKED = -3e38

N_PAGES = PAST_LEN // PAGE_SIZE
N_CMP_BLK = 128
Z_MAIN = 3 * CONV_DIM + ATTN_DIM + 6 * KV_DIM
LANE = 128
VMEM_LIMIT = 56 * 1024 * 1024


def _cparams(*sem):
    return pltpu.CompilerParams(dimension_semantics=sem, vmem_limit_bytes=VMEM_LIMIT)


def _masked_softmax(s, mask):
    s = jnp.where(mask, s, NEG_INF)
    m = jnp.max(s, axis=-1, keepdims=True)
    e = jnp.where(mask, jnp.exp(s - m), 0.0)
    return e / jnp.maximum(jnp.sum(e, axis=-1, keepdims=True), TINY)


def _dot_nt(a, b):
    return lax.dot_general(a, b, (((1,), (1,)), ((), ())), preferred_element_type=F32)


def _dot(a, b):
    return jnp.dot(a, b, preferred_element_type=F32)


def _lane_pick(x, c):
    lane = lax.broadcasted_iota(jnp.int32, x.shape, 1)
    return jnp.sum(jnp.where(lane == c, x, 0.0), axis=-1, keepdims=True)


def _rmsnorm_body(x_ref, g_ref, o_ref):
    x = x_ref[...]
    inv = lax.rsqrt(jnp.mean(x * x, axis=-1, keepdims=True) + RMS_EPS)
    o_ref[...] = ((x * inv) * g_ref[...]).astype(o_ref.dtype)


def rmsnorm_cast(x, g, tm=256):
    m, d = x.shape
    return pl.pallas_call(
        _rmsnorm_body, grid=(m // tm,),
        in_specs=[pl.BlockSpec((tm, d), lambda i: (i, 0)), pl.BlockSpec((1, d), lambda i: (0, 0))],
        out_specs=pl.BlockSpec((tm, d), lambda i: (i, 0)),
        out_shape=jax.ShapeDtypeStruct((m, d), BF16),
        compiler_params=_cparams("arbitrary"), name="rmsnorm_cast",
    )(x, g.reshape(1, d))


def _matmul_body(*refs, n_pairs, has_res):
    n_in = 2 * n_pairs + (1 if has_res else 0)
    o_ref = refs[n_in]
    wbf = refs[n_in + 1:]

    @pl.when(pl.program_id(1) == 0)
    def _():
        for p in range(n_pairs):
            wbf[p][...] = refs[2 * p + 1][...].astype(BF16)

    acc = _dot(refs[0][...], wbf[0][...])
    for p in range(1, n_pairs):
        acc = acc + _dot(refs[2 * p][...], wbf[p][...])
    if has_res:
        acc = acc + refs[2 * n_pairs][...]
    o_ref[...] = acc


def matmul(pairs, res=None, tm=512, tn=512):
    m = pairs[0][0].shape[0]
    tm = min(tm, m)
    n = (pairs[0][1].shape[1] // tn) * tn
    in_specs, args, scratch = [], [], []
    for a, w, kb in pairs:
        k = a.shape[1]
        in_specs += [pl.BlockSpec((tm, k), lambda j, i: (i, 0)), pl.BlockSpec((k, tn), lambda j, i, kb=kb: (kb, j))]
        args += [a, w]
        scratch.append(pltpu.VMEM((k, tn), BF16))
    if res is not None:
        in_specs.append(pl.BlockSpec((tm, tn), lambda j, i: (i, j)))
        args.append(res)
    return pl.pallas_call(
        functools.partial(_matmul_body, n_pairs=len(pairs), has_res=res is not None),
        grid=(n // tn, m // tm), in_specs=in_specs,
        out_specs=pl.BlockSpec((tm, tn), lambda j, i: (i, j)),
        out_shape=jax.ShapeDtypeStruct((m, n), F32), scratch_shapes=scratch,
        compiler_params=_cparams("arbitrary", "arbitrary"), name="matmul",
    )(*args)


def _head_norm(x, g):
    inv = lax.rsqrt(jnp.mean(x * x, axis=-1, keepdims=True) + RMS_EPS)
    return (x * inv) * g


def _postproj_body(zq_ref, zc_ref, zs_ref, zw_ref, qg_ref, kg_ref, q_ref, kvc_ref, kvs_ref, kvw_ref):
    for h in range(N_Q_HEADS):
        sl = slice(h * HEAD_DIM, (h + 1) * HEAD_DIM)
        q_ref[:, sl] = _head_norm(zq_ref[:, sl], qg_ref[...]).astype(q_ref.dtype)
    kvc_ref[...] = zc_ref[...]
    for h in range(N_KV_HEADS):
        sl = slice(h * HEAD_DIM, (h + 1) * HEAD_DIM)
        kvs_ref[:, sl] = _head_norm(zs_ref[:, sl], kg_ref[1:2, :])
        kvw_ref[:, sl] = _head_norm(zw_ref[:, sl], kg_ref[2:3, :])
    kvs_ref[:, KV_DIM:] = zs_ref[:, KV_DIM:]
    kvw_ref[:, KV_DIM:] = zw_ref[:, KV_DIM:]


def postproj(z, q_norm_g, k_norm_g, tm=256):
    m = z.shape[0]
    kv = 2 * KV_DIM
    q0 = 3 * CONV_DIM // ATTN_DIM
    c0 = (3 * CONV_DIM + ATTN_DIM) // kv
    return pl.pallas_call(
        _postproj_body, grid=(m // tm,),
        in_specs=[pl.BlockSpec((tm, ATTN_DIM), lambda i: (i, q0)),
                  pl.BlockSpec((tm, kv), lambda i: (i, c0)),
                  pl.BlockSpec((tm, kv), lambda i: (i, c0 + 1)),
                  pl.BlockSpec((tm, kv), lambda i: (i, c0 + 2)),
                  pl.BlockSpec((1, HEAD_DIM), lambda i: (0, 0)),
                  pl.BlockSpec((3, HEAD_DIM), lambda i: (0, 0))],
        out_specs=[pl.BlockSpec((tm, ATTN_DIM), lambda i: (i, 0)),
                   pl.BlockSpec((tm, kv), lambda i: (i, 0)),
                   pl.BlockSpec((tm, kv), lambda i: (i, 0)),
                   pl.BlockSpec((tm, kv), lambda i: (i, 0))],
        out_shape=[jax.ShapeDtypeStruct((m, ATTN_DIM), F32)] + [jax.ShapeDtypeStruct((m, kv), F32)] * 3,
        compiler_params=_cparams("arbitrary"), name="postproj",
    )(z, z, z, z, q_norm_g.reshape(1, HEAD_DIM), k_norm_g)


def _conv_finish(b, y, g):
    c = b * y
    inv = lax.rsqrt(jnp.mean(c * c, axis=-1, keepdims=True) + RMS_EPS)
    return ((c * inv) * g).astype(BF16)


def _conv_prompt_body(gb_ref, gc_ref, hc_ref, w_ref, g_ref, o_ref, st_ref, carry_ref):
    tt = gb_ref.shape[0]

    @pl.when(pl.program_id(1) == 0)
    def _():
        carry_ref[...] = jnp.zeros_like(carry_ref)

    u = gc_ref[...] * hc_ref[...]
    prev = carry_ref[...]
    p1, p2 = prev[7:8, :], prev[6:7, :]
    row = lax.broadcasted_iota(jnp.int32, u.shape, 0)
    u1 = jnp.where(row == 0, p1, pltpu.roll(u, 1, axis=0))
    u2 = jnp.where(row == 0, p2, jnp.where(row == 1, p1, pltpu.roll(u, 2, axis=0)))
    y = u2 * w_ref[0:1, :] + u1 * w_ref[1:2, :] + u * w_ref[2:3, :]
    o_ref[...] = _conv_finish(gb_ref[...], y, g_ref[...])
    last = u[tt - 8:tt, :]
    carry_ref[...] = last
    st_ref[0] = last


def conv_prompt(z, n, t, conv_w, gain, tt=256):
    nt = t // tt
    row = lambda b, i: (b * nt + i, 0)
    return pl.pallas_call(
        _conv_prompt_body, grid=(n, nt),
        in_specs=[pl.BlockSpec((tt, CONV_DIM), lambda b, i: (b * nt + i, 0)),
                  pl.BlockSpec((tt, CONV_DIM), lambda b, i: (b * nt + i, 1)),
                  pl.BlockSpec((tt, CONV_DIM), lambda b, i: (b * nt + i, 2)),
                  pl.BlockSpec((CONV_WIDTH, CONV_DIM), lambda b, i: (0, 0)),
                  pl.BlockSpec((1, CONV_DIM), lambda b, i: (0, 0))],
        out_specs=[pl.BlockSpec((tt, CONV_DIM), row),
                   pl.BlockSpec((1, 8, CONV_DIM), lambda b, i: (b, 0, 0))],
        out_shape=[jax.ShapeDtypeStruct((n * t, CONV_DIM), BF16), jax.ShapeDtypeStruct((n, 8, CONV_DIM), F32)],
        scratch_shapes=[pltpu.VMEM((8, CONV_DIM), F32)],
        compiler_params=_cparams("arbitrary", "arbitrary"), name="conv_prompt",
    )(z, z, z, conv_w, gain.reshape(1, CONV_DIM))


def _conv_sample_body(z_ref, pre_ref, w_ref, g_ref, o_ref, st_ref, *, t):
    up = [pre_ref[k] for k in range(CONV_WIDTH - 1)] + [z_ref[1, k] * z_ref[2, k] for k in range(t)]
    for k in range(t):
        y = up[k] * w_ref[0:1, :] + up[k + 1] * w_ref[1:2, :] + up[k + 2] * w_ref[2:3, :]
        o_ref[k] = _conv_finish(z_ref[0, k], y, g_ref[...])
    for k in range(CONV_WIDTH - 1):
        st_ref[k] = up[t + k]


def conv_sample(z, n, t, state, conv_w, gain):
    zt = z[:, :3 * CONV_DIM].reshape(n, t, 3, CONV_DIM).transpose(2, 1, 0, 3)
    whole = lambda shape: pl.BlockSpec(shape, lambda i: (0,) * len(shape))
    out, st = pl.pallas_call(
        functools.partial(_conv_sample_body, t=t), grid=(1,),
        in_specs=[whole((3, t, n, CONV_DIM)), whole((CONV_WIDTH - 1, n, CONV_DIM)),
                  whole((CONV_WIDTH, CONV_DIM)), whole((1, CONV_DIM))],
        out_specs=[whole((t, n, CONV_DIM)), whole((CONV_WIDTH - 1, n, CONV_DIM))],
        out_shape=[jax.ShapeDtypeStruct((t, n, CONV_DIM), BF16),
                   jax.ShapeDtypeStruct((CONV_WIDTH - 1, n, CONV_DIM), F32)],
        compiler_params=_cparams("arbitrary"), name="conv_sample",
    )(zt, state.transpose(1, 0, 2), conv_w, gain.reshape(1, CONV_DIM))
    return out.transpose(1, 0, 2).reshape(n * t, CONV_DIM), st.transpose(1, 0, 2)


def _compress_body(pt_ref, *refs):
    del pt_ref
    pages = refs[:N_PAGES]
    w1_ref, pe_ref, w2_ref, kg_ref, kc_ref, vc_ref = refs[N_PAGES:]
    cpp = PAGE_SIZE // CMP_STRIDE
    n_slot = 2 * N_KV_HEADS
    for v in range(2):
        lhs = []
        for j in range(N_KV_HEADS):
            slot = v * N_KV_HEADS + j
            cols = []
            for s in range(CMP_STRIDE):
                pieces = [pages[p][pl.ds(s * n_slot + slot, cpp, stride=CMP_STRIDE * n_slot), :]
                          for p in range(N_PAGES)]
                cols.append(jnp.concatenate(pieces, axis=0).astype(BF16))
            lhs.append(jnp.concatenate(cols, axis=1))
        lhs = jnp.concatenate(lhs, axis=0)
        w1 = w1_ref[v].astype(BF16)
        r = _dot(lhs, w1)
        hpre = r[:, :HEAD_DIM] + pltpu.roll(r[:, HEAD_DIM:], r.shape[0] - 1, axis=0)
        bias = jnp.zeros((8, HEAD_DIM), F32)
        for rr in range(R_CMP):
            pe = jnp.broadcast_to(pe_ref[v, rr:rr + 1, :], (8, CMP_STRIDE * HEAD_DIM)).astype(BF16)
            bias = bias + _dot(pe, w1[:, rr * HEAD_DIM:(rr + 1) * HEAD_DIM])
        hid = jax.nn.gelu(hpre + bias[0:1, :])
        out = _dot(hid.astype(BF16), w2_ref[v].astype(BF16))
        if v == 0:
            out = _head_norm(out, kg_ref[0:1, :])
        dst = kc_ref if v == 0 else vc_ref
        for j in range(N_KV_HEADS):
            dst[0, j] = out[j * N_CMP_BLK:(j + 1) * N_CMP_BLK, :].astype(BF16)


def compress(pool, page_table, w1cat, pe_cat, phi_w2, k_norm_g):
    n = page_table.shape[0]
    page_rows = PAGE_SIZE * 2 * N_KV_HEADS
    page_spec = lambda p: pl.BlockSpec((page_rows, HEAD_DIM), lambda i, pt, p=p: (pt[i, p], 0))
    const = lambda shape: pl.BlockSpec(shape, lambda i, pt: (0,) * len(shape))
    out_spec = pl.BlockSpec((1, N_KV_HEADS, N_CMP_BLK, HEAD_DIM), lambda i, pt: (i, 0, 0, 0))
    out_sds = jax.ShapeDtypeStruct((n, N_KV_HEADS, N_CMP_BLK, HEAD_DIM), BF16)
    return pl.pallas_call(
        _compress_body,
        grid_spec=pltpu.PrefetchScalarGridSpec(
            num_scalar_prefetch=1, grid=(n,),
            in_specs=[page_spec(p) for p in range(N_PAGES)]
            + [const(w1cat.shape), const(pe_cat.shape), const(phi_w2.shape), const(k_norm_g.shape)],
            out_specs=[out_spec, out_spec]),
        out_shape=[out_sds, out_sds],
        compiler_params=_cparams("arbitrary"), name="compress",
    )(page_table, *([pool] * N_PAGES), w1cat, pe_cat, phi_w2, k_norm_g)


def _cmp_select_body(q_ref, kc_ref, vc_ref, cov_ref, o_ref, sel_ref, *, tq, pos_base):
    qt = pl.program_id(2)
    rows = GQA * tq
    q = q_ref[0].reshape(rows, HEAD_DIM).astype(BF16)
    s = _dot_nt(q, kc_ref[0, 0]) * SCALE
    row = lax.broadcasted_iota(jnp.int32, (rows, N_CMP_BLK), 0)
    blk = lax.broadcasted_iota(jnp.int32, (rows, N_CMP_BLK), 1)
    pos = pos_base + qt * tq + (row & (tq - 1))
    valid = (blk < N_CMP_BLK - 1) & (blk * CMP_STRIDE + (L_CMP - 1) <= pos)
    p = _masked_softmax(s, valid)
    o_ref[0] = _dot(p.astype(BF16), vc_ref[0, 0]).reshape(GQA, tq, HEAD_DIM)
    psum = p[0:tq]
    for g in range(1, GQA):
        psum = psum + p[g * tq:(g + 1) * tq]
    p_hi = psum.astype(BF16)
    p_lo = (psum - p_hi.astype(F32)).astype(BF16)
    imp = _dot(p_hi, cov_ref[...]) + _dot(p_lo, cov_ref[...])
    blk = lax.broadcasted_iota(jnp.int32, (tq, LANE), 1)
    cur = (pos_base + qt * tq + lax.broadcasted_iota(jnp.int32, (tq, LANE), 0)) >> 6
    forced = (blk == 0) | (blk == cur) | (blk == cur - 1)
    score = jnp.where(blk <= cur, imp + jnp.where(forced, FORCE_BONUS, 0.0), NEG_INF)
    sel = jnp.zeros((tq, LANE), F32)
    for _ in range(N_SEL):
        m = jnp.max(score, axis=-1, keepdims=True)
        first = jnp.min(jnp.where(score == m, blk, LANE), axis=-1, keepdims=True)
        hit = blk == first
        sel = jnp.where(hit, 1.0, sel)
        score = jnp.where(hit, PICKED, score)
    sel_ref[0, 0] = sel.astype(sel_ref.dtype)


def cmp_select(q, kc, vc, cover, tq, pos_base):
    n, _, t, _ = q.shape
    assert L_SEL == 64
    return pl.pallas_call(
        functools.partial(_cmp_select_body, tq=tq, pos_base=pos_base), grid=(n, N_KV_HEADS, t // tq),
        in_specs=[pl.BlockSpec((1, GQA, tq, HEAD_DIM), lambda b, j, i: (b, j, i, 0)),
                  pl.BlockSpec((1, 1, N_CMP_BLK, HEAD_DIM), lambda b, j, i: (b, j, 0, 0)),
                  pl.BlockSpec((1, 1, N_CMP_BLK, HEAD_DIM), lambda b, j, i: (b, j, 0, 0)),
                  pl.BlockSpec((N_CMP_BLK, LANE), lambda b, j, i: (0, 0))],
        out_specs=[pl.BlockSpec((1, GQA, tq, HEAD_DIM), lambda b, j, i: (b, j, i, 0)),
                   pl.BlockSpec((1, 1, tq, LANE), lambda b, j, i: (b, j, i, 0))],
        out_shape=[jax.ShapeDtypeStruct((n, N_Q_HEADS, t, HEAD_DIM), F32),
                   jax.ShapeDtypeStruct((n, N_KV_HEADS, t, LANE), F32)],
        compiler_params=_cparams("arbitrary", "arbitrary", "arbitrary"), name="cmp_select",
    )(q, kc, vc, cover)


def _attn_prompt_body(*refs, tq, t, selected):
    if selected:
        q_ref, k_ref, v_ref, sel_ref, e_ref, o_ref = refs
    else:
        q_ref, k_ref, v_ref, o_ref = refs
    qt = pl.program_id(2)
    rows = GQA * tq
    q = q_ref[0].reshape(rows, HEAD_DIM).astype(BF16)
    if selected:
        start, nk = 0, t
        k = k_ref[...].astype(BF16)
        v = v_ref[...].astype(BF16)
    else:
        nk = WINDOW + tq
        start = pl.multiple_of(jnp.maximum(qt * tq - WINDOW, 0), tq)
        k = k_ref[pl.ds(start, nk), :].astype(BF16)
        v = v_ref[pl.ds(start, nk), :].astype(BF16)
    s = _dot_nt(q, k) * SCALE
    qpos = qt * tq + (lax.broadcasted_iota(jnp.int32, (rows, nk), 0) & (tq - 1))
    kpos = start + lax.broadcasted_iota(jnp.int32, (rows, nk), 1)
    causal = kpos <= qpos
    if selected:
        picked = _dot(sel_ref[0, 0].astype(BF16), e_ref[...])
        picked = jnp.concatenate([picked] * GQA, axis=0)
        mask = causal & (picked > 0.5)
    else:
        mask = causal & (kpos > qpos - WINDOW)
    p = _masked_softmax(s, mask)
    o_ref[0] = _dot(p.astype(BF16), v).reshape(GQA, tq, HEAD_DIM)


def attn_prompt(q, kv, sel, expand, tq=128):
    n, _, t, _ = q.shape
    selected = sel is not None
    in_specs = [pl.BlockSpec((1, GQA, tq, HEAD_DIM), lambda b, j, i: (b, j, i, 0)),
                pl.BlockSpec((t, HEAD_DIM), lambda b, j, i: (b, j)),
                pl.BlockSpec((t, HEAD_DIM), lambda b, j, i: (b, N_KV_HEADS + j))]
    args = [q, kv, kv]
    if selected:
        in_specs += [pl.BlockSpec((1, 1, tq, LANE), lambda b, j, i: (b, j, i, 0)),
                     pl.BlockSpec(expand.shape, lambda b, j, i: (0, 0))]
        args += [sel, expand]
    return pl.pallas_call(
        functools.partial(_attn_prompt_body, tq=tq, t=t, selected=selected), grid=(n, N_KV_HEADS, t // tq),
        in_specs=in_specs,
        out_specs=pl.BlockSpec((1, GQA, tq, HEAD_DIM), lambda b, j, i: (b, j, i, 0)),
        out_shape=jax.ShapeDtypeStruct((n, N_Q_HEADS, t, HEAD_DIM), F32),
        compiler_params=_cparams("arbitrary", "arbitrary", "arbitrary"),
        name="attn_prompt_sel" if selected else "attn_prompt_win",
    )(*args)


T_PAD = 8


def _pad_keys(x):
    return jnp.concatenate([x, jnp.zeros((PAGE_SIZE - x.shape[0], x.shape[1]), x.dtype)], axis=0)


def _attn_sample_sel_body(pt_ref, *refs, t_real):
    del pt_ref
    pages = refs[:N_PAGES]
    q_ref, sel_ref, new_ref, e_ref, o_ref = refs[N_PAGES:]
    rows = GQA * T_PAD
    tok = lax.broadcasted_iota(jnp.int32, (rows, PAGE_SIZE), 0) & (T_PAD - 1)
    col = lax.broadcasted_iota(jnp.int32, (rows, PAGE_SIZE), 1)
    for j in range(N_KV_HEADS):
        kc0, vc0 = j * HEAD_DIM, KV_DIM + j * HEAD_DIM
        q = q_ref[0, j * GQA:(j + 1) * GQA].reshape(rows, HEAD_DIM).astype(BF16)
        parts = [_dot_nt(q, pages[p][0, :, kc0:kc0 + HEAD_DIM].astype(BF16)) for p in range(N_PAGES)]
        parts.append(_dot_nt(q, _pad_keys(new_ref[0, :, kc0:kc0 + HEAD_DIM]).astype(BF16)))
        s = jnp.concatenate(parts, axis=1) * SCALE
        sel = sel_ref[0, j]
        sel4 = jnp.concatenate([sel] * GQA, axis=0)
        picked = _dot(sel4.astype(BF16), e_ref[...])
        new_blk = PAST_LEN // L_SEL
        new_ok = (col < t_real) & (col <= tok)
        new_picked = jnp.where(new_ok, _lane_pick(sel4, new_blk), 0.0)
        mask = jnp.concatenate([picked, new_picked], axis=1) > 0.5
        p = _masked_softmax(s, mask).astype(BF16)
        o = _dot(p[:, PAST_LEN:], _pad_keys(new_ref[0, :, vc0:vc0 + HEAD_DIM]).astype(BF16))
        for pg in range(N_PAGES):
            o = o + _dot(p[:, pg * PAGE_SIZE:(pg + 1) * PAGE_SIZE], pages[pg][0, :, vc0:vc0 + HEAD_DIM].astype(BF16))
        o_ref[0, j * GQA:(j + 1) * GQA] = o.reshape(GQA, T_PAD, HEAD_DIM)


def attn_sample_sel(pool, page_table, q, sel, new_rows, expand, t_real):
    n = page_table.shape[0]
    cols = pool.shape[-1]
    page_spec = lambda p: pl.BlockSpec((1, PAGE_SIZE, cols), lambda i, pt, p=p: (pt[i, p], 0, 0))
    return pl.pallas_call(
        functools.partial(_attn_sample_sel_body, t_real=t_real),
        grid_spec=pltpu.PrefetchScalarGridSpec(
            num_scalar_prefetch=1, grid=(n,),
            in_specs=[page_spec(p) for p in range(N_PAGES)]
            + [pl.BlockSpec((1, N_Q_HEADS, T_PAD, HEAD_DIM), lambda i, pt: (i, 0, 0, 0)),
               pl.BlockSpec((1, N_KV_HEADS, T_PAD, LANE), lambda i, pt: (i, 0, 0, 0)),
               pl.BlockSpec((1, T_PAD, cols), lambda i, pt: (i, 0, 0)),
               pl.BlockSpec(expand.shape, lambda i, pt: (0, 0))],
            out_specs=pl.BlockSpec((1, N_Q_HEADS, T_PAD, HEAD_DIM), lambda i, pt: (i, 0, 0, 0))),
        out_shape=jax.ShapeDtypeStruct((n, N_Q_HEADS, T_PAD, HEAD_DIM), F32),
        compiler_params=_cparams("arbitrary"), name="attn_sample_sel",
    )(page_table, *([pool] * N_PAGES), q, sel, new_rows, expand)


def _attn_sample_win_body(win_ref, q_ref, new_ref, o_ref, wout_ref, *, t_real):
    rows = GQA * T_PAD
    w_buf = win_ref.shape[1]
    tok_o = lax.broadcasted_iota(jnp.int32, (rows, w_buf), 0) & (T_PAD - 1)
    col_o = lax.broadcasted_iota(jnp.int32, (rows, w_buf), 1)
    tok_n = lax.broadcasted_iota(jnp.int32, (rows, PAGE_SIZE), 0) & (T_PAD - 1)
    col_n = lax.broadcasted_iota(jnp.int32, (rows, PAGE_SIZE), 1)
    old_ok = jnp.where(col_o + (WINDOW - w_buf) > tok_o, 1.0, 0.0)
    new_ok = jnp.where((col_n < t_real) & (col_n <= tok_n), 1.0, 0.0)
    mask = jnp.concatenate([old_ok, new_ok], axis=1) > 0.5
    for j in range(N_KV_HEADS):
        kc0, vc0 = j * HEAD_DIM, KV_DIM + j * HEAD_DIM
        q = q_ref[0, j * GQA:(j + 1) * GQA].reshape(rows, HEAD_DIM).astype(BF16)
        k_new = _pad_keys(new_ref[0, :, kc0:kc0 + HEAD_DIM]).astype(BF16)
        v_new = _pad_keys(new_ref[0, :, vc0:vc0 + HEAD_DIM]).astype(BF16)
        s = jnp.concatenate([_dot_nt(q, win_ref[0, :, kc0:kc0 + HEAD_DIM].astype(BF16)), _dot_nt(q, k_new)], axis=1) * SCALE
        p = _masked_softmax(s, mask).astype(BF16)
        o = _dot(p[:, :w_buf], win_ref[0, :, vc0:vc0 + HEAD_DIM].astype(BF16)) + _dot(p[:, w_buf:], v_new)
        o_ref[0, j * GQA:(j + 1) * GQA] = o.reshape(GQA, T_PAD, HEAD_DIM)
    shifted = pltpu.roll(win_ref[0], w_buf - t_real, axis=0)
    wout_ref[0] = shifted
    sub = lax.broadcasted_iota(jnp.int32, (T_PAD, new_ref.shape[2]), 0)
    tail = jnp.where(sub >= T_PAD - t_real, pltpu.roll(new_ref[0], T_PAD - t_real, axis=0), shifted[w_buf - T_PAD:, :])
    wout_ref[0, w_buf - T_PAD:, :] = tail


def attn_sample_win(win, q, new_rows, t_real):
    n, w_buf, cols = win.shape
    return pl.pallas_call(
        functools.partial(_attn_sample_win_body, t_real=t_real), grid=(n,),
        in_specs=[pl.BlockSpec((1, w_buf, cols), lambda i: (i, 0, 0)),
                  pl.BlockSpec((1, N_Q_HEADS, T_PAD, HEAD_DIM), lambda i: (i, 0, 0, 0)),
                  pl.BlockSpec((1, T_PAD, cols), lambda i: (i, 0, 0))],
        out_specs=[pl.BlockSpec((1, N_Q_HEADS, T_PAD, HEAD_DIM), lambda i: (i, 0, 0, 0)),
                   pl.BlockSpec((1, w_buf, cols), lambda i: (i, 0, 0))],
        out_shape=[jax.ShapeDtypeStruct((n, N_Q_HEADS, T_PAD, HEAD_DIM), F32),
                   jax.ShapeDtypeStruct((n, w_buf, cols), F32)],
        compiler_params=_cparams("arbitrary"), name="attn_sample_win",
    )(win, q, new_rows)


def _combine_body(oc_ref, os_ref, ow_ref, gl_ref, g_ref, out_ref, *, rows):
    gates = jax.nn.sigmoid(gl_ref[...])
    outs = []
    sq = jnp.zeros((rows, 1), F32)
    for h in range(N_Q_HEADS):
        a = (_lane_pick(gates, h) * oc_ref[:, h].reshape(rows, HEAD_DIM)
             + _lane_pick(gates, N_Q_HEADS + h) * os_ref[:, h].reshape(rows, HEAD_DIM)
             + _lane_pick(gates, 2 * N_Q_HEADS + h) * ow_ref[:, h].reshape(rows, HEAD_DIM))
        outs.append(a)
        sq = sq + jnp.sum(a * a, axis=-1, keepdims=True)
    inv = lax.rsqrt(sq * (1.0 / ATTN_DIM) + RMS_EPS)
    for h in range(N_Q_HEADS):
        sl = slice(h * HEAD_DIM, (h + 1) * HEAD_DIM)
        out_ref[:, sl] = ((outs[h] * inv) * g_ref[:, sl]).astype(out_ref.dtype)


def combine(o_cmp, o_sel, o_win, gate_logits, gain, bn, tt):
    n, _, t, _ = o_cmp.shape
    nt = t // tt
    rows = bn * tt
    o_spec = pl.BlockSpec((bn, N_Q_HEADS, tt, HEAD_DIM), lambda b, i: (b, 0, i, 0))
    return pl.pallas_call(
        functools.partial(_combine_body, rows=rows), grid=(n // bn, nt),
        in_specs=[o_spec, o_spec, o_spec,
                  pl.BlockSpec((rows, LANE), lambda b, i: (b * nt + i, 0)),
                  pl.BlockSpec((1, ATTN_DIM), lambda b, i: (0, 0))],
        out_specs=pl.BlockSpec((rows, ATTN_DIM), lambda b, i: (b * nt + i, 0)),
        out_shape=jax.ShapeDtypeStruct((n * t, ATTN_DIM), BF16),
        compiler_params=_cparams("arbitrary", "arbitrary"), name="combine",
    )(o_cmp, o_sel, o_win, gate_logits, gain.reshape(1, ATTN_DIM))


def _router_body(x_ref, w_ref, b_ref, gate_ref):
    logits = _dot(x_ref[...], w_ref[...].astype(BF16)) + b_ref[...]
    lane = lax.broadcasted_iota(jnp.int32, logits.shape, 1)
    is_grp = (lane >= N_EXPERTS) & (lane < N_EXPERTS + N_GROUPS)
    gl = jnp.where(is_grp, logits, NEG_INF)
    ge = jnp.where(is_grp, jnp.exp(gl - jnp.max(gl, axis=-1, keepdims=True)), 0.0)
    p_grp = ge / jnp.sum(ge, axis=-1, keepdims=True)
    g_val = jnp.max(p_grp, axis=-1, keepdims=True)
    g_idx = jnp.min(jnp.where(is_grp & (p_grp == g_val), lane, 2 * LANE), axis=-1, keepdims=True) - N_EXPERTS
    lo = g_idx * EXPERTS_PER_GROUP
    in_grp = (lane >= lo) & (lane < lo + EXPERTS_PER_GROUP)
    el = jnp.where(in_grp, logits, NEG_INF)
    ee = jnp.where(in_grp, jnp.exp(el - jnp.max(el, axis=-1, keepdims=True)), 0.0)
    p_e = ee / jnp.sum(ee, axis=-1, keepdims=True)
    cand = jnp.where(in_grp, p_e, -1.0)
    e1 = jnp.max(cand, axis=-1, keepdims=True)
    i1 = jnp.min(jnp.where(cand == e1, lane, 2 * LANE), axis=-1, keepdims=True)
    cand = jnp.where(lane == i1, -1.0, cand)
    e2 = jnp.max(cand, axis=-1, keepdims=True)
    i2 = jnp.min(jnp.where(cand == e2, lane, 2 * LANE), axis=-1, keepdims=True)
    tot = e1 + e2
    gate_ref[...] = jnp.where(lane == i1, g_val * e1 / tot, 0.0) + jnp.where(lane == i2, g_val * e2 / tot, 0.0)


def router(xt, w_router, b_router, tm=256):
    m, d = xt.shape
    return pl.pallas_call(
        _router_body, grid=(m // tm,),
        in_specs=[pl.BlockSpec((tm, d), lambda i: (i, 0)),
                  pl.BlockSpec((d, LANE), lambda i: (0, 0)),
                  pl.BlockSpec((1, LANE), lambda i: (0, 0))],
        out_specs=pl.BlockSpec((tm, LANE), lambda i: (i, 0)),
        out_shape=jax.ShapeDtypeStruct((m, LANE), F32),
        compiler_params=_cparams("arbitrary"), name="router",
    )(xt, w_router, b_router)


def _moe_body(x_ref, gate_ref, h_ref, wg_ref, wu_ref, wd_ref, y_ref):
    e = pl.program_id(1)

    @pl.when(e == 0)
    def _():
        y_ref[...] = h_ref[...]

    x = x_ref[...]
    lane = lax.broadcasted_iota(jnp.int32, gate_ref.shape, 1)
    g = jnp.sum(jnp.where(lane == e, gate_ref[...], 0.0), axis=-1, keepdims=True)
    hid = jax.nn.silu(_dot(x, wg_ref[0])) * _dot(x, wu_ref[0])
    y_ref[...] += _dot((hid * g).astype(BF16), wd_ref[0])


def moe(xt, gate, h, w_gate, w_up, w_down, tm=256):
    m, d = xt.shape
    f = w_gate.shape[-1]
    return pl.pallas_call(
        _moe_body, grid=(m // tm, N_EXPERTS),
        in_specs=[pl.BlockSpec((tm, d), lambda i, e: (i, 0)),
                  pl.BlockSpec((tm, LANE), lambda i, e: (i, 0)),
                  pl.BlockSpec((tm, d), lambda i, e: (i, 0)),
                  pl.BlockSpec((1, d, f), lambda i, e: (e, 0, 0)),
                  pl.BlockSpec((1, d, f), lambda i, e: (e, 0, 0)),
                  pl.BlockSpec((1, f, d), lambda i, e: (e, 0, 0))],
        out_specs=pl.BlockSpec((tm, d), lambda i, e: (i, 0)),
        out_shape=jax.ShapeDtypeStruct((m, d), F32),
        compiler_params=_cparams("arbitrary", "arbitrary"), name="moe",
    )(xt, gate, h, w_gate, w_up, w_down)


def _cover_matrix(nsb):
    i = np.arange(N_CMP_BLK)[:, None]
    j = np.arange(LANE)[None, :]
    m = np.zeros((N_CMP_BLK, LANE), np.float32)
    for a in range(L_SEL // CMP_STRIDE):
        for c in range(R_CMP):
            m += (i == (L_SEL // CMP_STRIDE) * j + a - c)
    m[N_CMP_BLK - 1:, :] = 0.0
    m[:, nsb:] = 0.0
    return jnp.asarray(m, BF16)


def _expand_matrix(n_keys):
    b = np.arange(LANE)[:, None]
    k = np.arange(n_keys)[None, :]
    return jnp.asarray((k // L_SEL == b).astype(np.float32), BF16)


def _to_heads(q, n, t):
    return q.reshape(n, t, N_Q_HEADS, HEAD_DIM).transpose(0, 2, 1, 3)


def kernel(x_prompt, x_sample, cache_cmp_kv, cache_sel_kv, state_win_kv, state_conv, page_table, norm_mix_g, w_in,
           conv_w, q_norm_g, k_norm_g, phi_pe, phi_w1, phi_w2, out_norm_g, w_out, norm_ffn_g, w_group_router,
           b_group_router, w_expert_router, b_expert_router, w_gate, w_up, w_down):
    n_p, t_p, d = x_prompt.shape
    n_s, t_s, _ = x_sample.shape
    assert w_in.shape[0] == 1 and t_s < CMP_STRIDE and t_s <= T_PAD and t_p % WINDOW == 0
    kv_cols = 2 * KV_DIM
    w_in_main = w_in[0]
    w_in_gate = jnp.pad(w_in[0][:, Z_MAIN:], ((0, 0), (0, LANE - 3 * N_Q_HEADS)))
    w_o = w_out[0]
    w1cat = phi_w1[0].reshape(2, R_CMP, CMP_STRIDE, HEAD_DIM, HEAD_DIM).transpose(0, 2, 3, 1, 4)
    w1cat = w1cat.reshape(2, CMP_STRIDE * HEAD_DIM, R_CMP * HEAD_DIM)
    pe_cat = phi_pe[0].reshape(2, R_CMP, CMP_STRIDE * HEAD_DIM)
    w_router = jnp.pad(jnp.concatenate([w_expert_router[0], w_group_router[0]], axis=1),
                       ((0, 0), (0, LANE - N_EXPERTS - N_GROUPS)))
    b_router = jnp.pad(jnp.concatenate([b_expert_router[0], b_group_router[0]]),
                       (0, LANE - N_EXPERTS - N_GROUPS)).reshape(1, LANE)
    wg_bf, wu_bf, wd_bf = w_gate[0].astype(BF16), w_up[0].astype(BF16), w_down[0].astype(BF16)

    def project(x2d):
        xn = rmsnorm_cast(x2d, norm_mix_g[0])
        z = matmul([(xn, w_in_main, 0)], tn=512)
        gate_logits = matmul([(xn, w_in_gate, 0)], tn=LANE)
        return z, gate_logits

    def finish(x2d, conv_out, attn_out):
        h = matmul([(conv_out, w_o, 0), (attn_out, w_o, 1)], res=x2d)
        xt = rmsnorm_cast(h, norm_ffn_g[0])
        gate = router(xt, w_router, b_router)
        return moe(xt, gate, h, wg_bf, wu_bf, wd_bf)

    xp = x_prompt.reshape(n_p * t_p, d)
    z, glog = project(xp)
    q, kvc, kvs, kvw = postproj(z, q_norm_g[0], k_norm_g[0])
    conv_out, conv_last = conv_prompt(z, n_p, t_p, conv_w[0], out_norm_g[0][:CONV_DIM])
    qh = _to_heads(q, n_p, t_p).astype(BF16)
    ident = jnp.arange(n_p * (t_p // PAGE_SIZE), dtype=jnp.int32).reshape(n_p, t_p // PAGE_SIZE)
    kc, vc = compress(kvc.reshape(-1, HEAD_DIM), ident, w1cat, pe_cat, phi_w2[0], k_norm_g[0])
    o_cmp, sel = cmp_select(qh, kc, vc, _cover_matrix(t_p // L_SEL), tq=256, pos_base=0)
    o_sel = attn_prompt(qh, kvs, sel, _expand_matrix(t_p))
    o_win = attn_prompt(qh, kvw, None, None)
    attn_out = combine(o_cmp, o_sel, o_win, glog, out_norm_g[0][CONV_DIM:], bn=1, tt=256)
    y_prompt = finish(xp, conv_out, attn_out).reshape(n_p, t_p, d)
    kv_shape = (1, n_p, t_p, 2, N_KV_HEADS, HEAD_DIM)
    w_keep = min(WINDOW, t_p)
    prompt_win = kvw.reshape(kv_shape)[:, :, t_p - w_keep:]
    prompt_conv = conv_last[:, 8 - (CONV_WIDTH - 1):, :][None]

    xs = x_sample.reshape(n_s * t_s, d)
    z, glog = project(xs)
    q, kvc_s, kvs_s, kvw_s = postproj(z, q_norm_g[0], k_norm_g[0])
    conv_out, conv_state = conv_sample(z, n_s, t_s, state_conv[0], conv_w[0], out_norm_g[0][:CONV_DIM])
    pad_t = lambda a: jnp.pad(a, ((0, 0), (0, T_PAD - t_s), (0, 0)))
    qh = jnp.pad(_to_heads(q, n_s, t_s), ((0, 0), (0, 0), (0, T_PAD - t_s), (0, 0)))
    pool_cmp = cache_cmp_kv[0].reshape(-1, HEAD_DIM)
    pool_sel = cache_sel_kv[0].reshape(-1, PAGE_SIZE, kv_cols)
    kc, vc = compress(pool_cmp, page_table, w1cat, pe_cat, phi_w2[0], k_norm_g[0])
    o_cmp, sel = cmp_select(qh, kc, vc, _cover_matrix(PAST_LEN // L_SEL + 1), tq=T_PAD, pos_base=PAST_LEN)
    o_sel = attn_sample_sel(pool_sel, page_table, qh, sel, pad_t(kvs_s.reshape(n_s, t_s, kv_cols)),
                            _expand_matrix(PAST_LEN), t_s)
    win = state_win_kv[0].reshape(n_s, -1, kv_cols)
    o_win, win_new = attn_sample_win(win, qh, pad_t(kvw_s.reshape(n_s, t_s, kv_cols)), t_s)
    glog_pad = pad_t(glog.reshape(n_s, t_s, LANE)).reshape(n_s * T_PAD, LANE)
    attn_out = combine(o_cmp, o_sel, o_win, glog_pad, out_norm_g[0][CONV_DIM:], bn=32, tt=T_PAD)
    attn_out = attn_out.reshape(n_s, T_PAD, ATTN_DIM)[:, :t_s].reshape(n_s * t_s, ATTN_DIM)
    y_sample = finish(xs, conv_out, attn_out).reshape(n_s, t_s, d)
    s_shape = (1, n_s, t_s, 2, N_KV_HEADS, HEAD_DIM)

    return (y_prompt, y_sample, kvc.reshape(kv_shape), kvs.reshape(kv_shape), prompt_win, prompt_conv,
            kvc_s.reshape(s_shape), kvs_s.reshape(s_shape),
            win_new.reshape(1, n_s, -1, 2, N_KV_HEADS, HEAD_DIM), conv_state[None])
```

```python
import functools

import numpy as np
import jax
import jax.numpy as jnp
from jax import lax
from jax.experimental import pallas as pl
from jax.experimental.pallas import tpu as pltpu

F32 = jnp.float32
BF16 = jnp.bfloat16

D_MODEL = 4096
PAST_LEN = 2048
PAGE_SIZE = 128
HEAD_DIM = 128
CONV_DIM = 2048
N_Q_HEADS = 16
N_KV_HEADS = 4
GQA = 4
ATTN_DIM = 2048
KV_DIM = 512
CONV_WIDTH = 3
L_CMP = 32
CMP_STRIDE = 16
R_CMP = 2
L_SEL = 64
N_SEL = 8
WINDOW = 512
FORCE_BONUS = 1e3
SCALE = HEAD_DIM ** -0.5
N_GROUPS = 4
EXPERTS_PER_GROUP = 4
N_EXPERTS = 16
D_FF_EXPERT = 512
RMS_EPS = 1e-6
NEG_INF = -1e30
TINY = 1e-30
PICKED = -3e38

N_PAGES = PAST_LEN // PAGE_SIZE
N_CMP_BLK = 128
Z_MAIN = 3 * CONV_DIM + ATTN_DIM + 6 * KV_DIM
LANE = 128
VMEM_LIMIT = 56 * 1024 * 1024


def _cparams(*sem):
    return pltpu.CompilerParams(dimension_semantics=sem, vmem_limit_bytes=VMEM_LIMIT)


def _masked_softmax(s, mask):
    s = jnp.where(mask, s, NEG_INF)
    m = jnp.max(s, axis=-1, keepdims=True)
    e = jnp.where(mask, jnp.exp(s - m), 0.0)
    return e / jnp.maximum(jnp.sum(e, axis=-1, keepdims=True), TINY)


def _dot_nt(a, b):
    return lax.dot_general(a, b, (((1,), (1,)), ((), ())), preferred_element_type=F32)


def _dot(a, b):
    return jnp.dot(a, b, preferred_element_type=F32)


def _lane_pick(x, c):
    lane = lax.broadcasted_iota(jnp.int32, x.shape, 1)
    return jnp.sum(jnp.where(lane == c, x, 0.0), axis=-1, keepdims=True)


def _rmsnorm_body(x_ref, g_ref, o_ref):
    x = x_ref[...]
    inv = lax.rsqrt(jnp.mean(x * x, axis=-1, keepdims=True) + RMS_EPS)
    o_ref[...] = ((x * inv) * g_ref[...]).astype(o_ref.dtype)


def rmsnorm_cast(x, g, dtype=BF16, tm=256):
    m, d = x.shape
    return pl.pallas_call(
        _rmsnorm_body, grid=(m // tm,),
        in_specs=[pl.BlockSpec((tm, d), lambda i: (i, 0)), pl.BlockSpec((1, d), lambda i: (0, 0))],
        out_specs=pl.BlockSpec((tm, d), lambda i: (i, 0)),
        out_shape=jax.ShapeDtypeStruct((m, d), dtype),
        compiler_params=_cparams("arbitrary"), name="rmsnorm_cast",
    )(x, g.reshape(1, d))


def _matmul_body(*refs, n_pairs, has_res):
    n_in = 2 * n_pairs + (1 if has_res else 0)
    o_ref = refs[n_in]
    wbf = refs[n_in + 1:]

    @pl.when(pl.program_id(1) == 0)
    def _():
        for p in range(n_pairs):
            wbf[p][...] = refs[2 * p + 1][...].astype(BF16)

    acc = _dot(refs[0][...], wbf[0][...])
    for p in range(1, n_pairs):
        acc = acc + _dot(refs[2 * p][...], wbf[p][...])
    if has_res:
        acc = acc + refs[2 * n_pairs][...]
    o_ref[...] = acc


def matmul(pairs, res=None, tm=512, tn=512):
    m = pairs[0][0].shape[0]
    tm = min(tm, m)
    n = (pairs[0][1].shape[1] // tn) * tn
    in_specs, args, scratch = [], [], []
    for a, w, kb in pairs:
        k = a.shape[1]
        in_specs += [pl.BlockSpec((tm, k), lambda j, i: (i, 0)), pl.BlockSpec((k, tn), lambda j, i, kb=kb: (kb, j))]
        args += [a, w]
        scratch.append(pltpu.VMEM((k, tn), BF16))
    if res is not None:
        in_specs.append(pl.BlockSpec((tm, tn), lambda j, i: (i, j)))
        args.append(res)
    return pl.pallas_call(
        functools.partial(_matmul_body, n_pairs=len(pairs), has_res=res is not None),
        grid=(n // tn, m // tm), in_specs=in_specs,
        out_specs=pl.BlockSpec((tm, tn), lambda j, i: (i, j)),
        out_shape=jax.ShapeDtypeStruct((m, n), F32), scratch_shapes=scratch,
        compiler_params=_cparams("arbitrary", "arbitrary"), name="matmul",
    )(*args)


def _head_norm(x, g):
    inv = lax.rsqrt(jnp.mean(x * x, axis=-1, keepdims=True) + RMS_EPS)
    return (x * inv) * g


def _postproj_body(zq_ref, zc_ref, zs_ref, zw_ref, qg_ref, kg_ref, q_ref, kvc_ref, kvs_ref, kvw_ref):
    for h in range(N_Q_HEADS):
        sl = slice(h * HEAD_DIM, (h + 1) * HEAD_DIM)
        q_ref[:, sl] = _head_norm(zq_ref[:, sl], qg_ref[...]).astype(q_ref.dtype)
    kvc_ref[...] = zc_ref[...]
    for h in range(N_KV_HEADS):
        sl = slice(h * HEAD_DIM, (h + 1) * HEAD_DIM)
        kvs_ref[:, sl] = _head_norm(zs_ref[:, sl], kg_ref[1:2, :])
        kvw_ref[:, sl] = _head_norm(zw_ref[:, sl], kg_ref[2:3, :])
    kvs_ref[:, KV_DIM:] = zs_ref[:, KV_DIM:]
    kvw_ref[:, KV_DIM:] = zw_ref[:, KV_DIM:]


def postproj(z, q_norm_g, k_norm_g, tm=256):
    m = z.shape[0]
    kv = 2 * KV_DIM
    q0 = 3 * CONV_DIM // ATTN_DIM
    c0 = (3 * CONV_DIM + ATTN_DIM) // kv
    return pl.pallas_call(
        _postproj_body, grid=(m // tm,),
        in_specs=[pl.BlockSpec((tm, ATTN_DIM), lambda i: (i, q0)),
                  pl.BlockSpec((tm, kv), lambda i: (i, c0)),
                  pl.BlockSpec((tm, kv), lambda i: (i, c0 + 1)),
                  pl.BlockSpec((tm, kv), lambda i: (i, c0 + 2)),
                  pl.BlockSpec((1, HEAD_DIM), lambda i: (0, 0)),
                  pl.BlockSpec((3, HEAD_DIM), lambda i: (0, 0))],
        out_specs=[pl.BlockSpec((tm, ATTN_DIM), lambda i: (i, 0)),
                   pl.BlockSpec((tm, kv), lambda i: (i, 0)),
                   pl.BlockSpec((tm, kv), lambda i: (i, 0)),
                   pl.BlockSpec((tm, kv), lambda i: (i, 0))],
        out_shape=[jax.ShapeDtypeStruct((m, ATTN_DIM), F32)] + [jax.ShapeDtypeStruct((m, kv), F32)] * 3,
        compiler_params=_cparams("arbitrary"), name="postproj",
    )(z, z, z, z, q_norm_g.reshape(1, HEAD_DIM), k_norm_g)


N_SLOT = 2 * N_KV_HEADS


def _postproj_prompt_body(zq_ref, zc_ref, zs_ref, zw_ref, qg_ref, kg_ref,
                          q_ref, ks_ref, kw_ref, kvc_ref, kvs_ref, kvw_ref):
    tm = zq_ref.shape[0]
    for h in range(N_Q_HEADS):
        sl = slice(h * HEAD_DIM, (h + 1) * HEAD_DIM)
        q_ref[0, h] = _head_norm(zq_ref[:, sl], qg_ref[...]).astype(BF16)
    for c in range(N_SLOT):
        sl = slice(c * HEAD_DIM, (c + 1) * HEAD_DIM)
        rows = pl.ds(c, tm, stride=N_SLOT)
        kvc_ref[rows, :] = zc_ref[:, sl]
        xs, xw = zs_ref[:, sl], zw_ref[:, sl]
        if c < N_KV_HEADS:
            xs, xw = _head_norm(xs, kg_ref[1:2, :]), _head_norm(xw, kg_ref[2:3, :])
        kvs_ref[rows, :] = xs
        kvw_ref[rows, :] = xw
        ks_ref[0, c] = xs.astype(BF16)
        kw_ref[0, c] = xw.astype(BF16)


def postproj_prompt(z, n, t, q_norm_g, k_norm_g, tm=256):
    kv = 2 * KV_DIM
    nt = t // tm
    q0 = 3 * CONV_DIM // ATTN_DIM
    c0 = (3 * CONV_DIM + ATTN_DIM) // kv
    rows_spec = pl.BlockSpec((tm * N_SLOT, HEAD_DIM), lambda b, i: (b * nt + i, 0))
    slot_spec = pl.BlockSpec((1, N_SLOT, tm, HEAD_DIM), lambda b, i: (b, 0, i, 0))
    rows_sds = jax.ShapeDtypeStruct((n * t * N_SLOT, HEAD_DIM), F32)
    slot_sds = jax.ShapeDtypeStruct((n, N_SLOT, t, HEAD_DIM), BF16)
    return pl.pallas_call(
        _postproj_prompt_body, grid=(n, nt),
        in_specs=[pl.BlockSpec((tm, ATTN_DIM), lambda b, i: (b * nt + i, q0)),
                  pl.BlockSpec((tm, kv), lambda b, i: (b * nt + i, c0)),
                  pl.BlockSpec((tm, kv), lambda b, i: (b * nt + i, c0 + 1)),
                  pl.BlockSpec((tm, kv), lambda b, i: (b * nt + i, c0 + 2)),
                  pl.BlockSpec((1, HEAD_DIM), lambda b, i: (0, 0)),
                  pl.BlockSpec((3, HEAD_DIM), lambda b, i: (0, 0))],
        out_specs=[pl.BlockSpec((1, N_Q_HEADS, tm, HEAD_DIM), lambda b, i: (b, 0, i, 0)),
                   slot_spec, slot_spec, rows_spec, rows_spec, rows_spec],
        out_shape=[jax.ShapeDtypeStruct((n, N_Q_HEADS, t, HEAD_DIM), BF16), slot_sds, slot_sds,
                   rows_sds, rows_sds, rows_sds],
        compiler_params=_cparams("arbitrary", "arbitrary"), name="postproj_prompt",
    )(z, z, z, z, q_norm_g.reshape(1, HEAD_DIM), k_norm_g)


def _conv_finish(b, y, g):
    c = b * y
    inv = lax.rsqrt(jnp.mean(c * c, axis=-1, keepdims=True) + RMS_EPS)
    return ((c * inv) * g).astype(BF16)


def _conv_prompt_body(gb_ref, gc_ref, hc_ref, w_ref, g_ref, o_ref, st_ref, carry_ref):
    tt = gb_ref.shape[0]

    @pl.when(pl.program_id(1) == 0)
    def _():
        carry_ref[...] = jnp.zeros_like(carry_ref)

    u = gc_ref[...] * hc_ref[...]
    prev = carry_ref[...]
    p1, p2 = prev[7:8, :], prev[6:7, :]
    row = lax.broadcasted_iota(jnp.int32, u.shape, 0)
    u1 = jnp.where(row == 0, p1, pltpu.roll(u, 1, axis=0))
    u2 = jnp.where(row == 0, p2, jnp.where(row == 1, p1, pltpu.roll(u, 2, axis=0)))
    y = u2 * w_ref[0:1, :] + u1 * w_ref[1:2, :] + u * w_ref[2:3, :]
    o_ref[...] = _conv_finish(gb_ref[...], y, g_ref[...])
    last = u[tt - 8:tt, :]
    carry_ref[...] = last
    st_ref[0] = last


def conv_prompt(z, n, t, conv_w, gain, tt=256):
    nt = t // tt
    row = lambda b, i: (b * nt + i, 0)
    return pl.pallas_call(
        _conv_prompt_body, grid=(n, nt),
        in_specs=[pl.BlockSpec((tt, CONV_DIM), lambda b, i: (b * nt + i, 0)),
                  pl.BlockSpec((tt, CONV_DIM), lambda b, i: (b * nt + i, 1)),
                  pl.BlockSpec((tt, CONV_DIM), lambda b, i: (b * nt + i, 2)),
                  pl.BlockSpec((CONV_WIDTH, CONV_DIM), lambda b, i: (0, 0)),
                  pl.BlockSpec((1, CONV_DIM), lambda b, i: (0, 0))],
        out_specs=[pl.BlockSpec((tt, CONV_DIM), row),
                   pl.BlockSpec((1, 8, CONV_DIM), lambda b, i: (b, 0, 0))],
        out_shape=[jax.ShapeDtypeStruct((n * t, CONV_DIM), BF16), jax.ShapeDtypeStruct((n, 8, CONV_DIM), F32)],
        scratch_shapes=[pltpu.VMEM((8, CONV_DIM), F32)],
        compiler_params=_cparams("arbitrary", "arbitrary"), name="conv_prompt",
    )(z, z, z, conv_w, gain.reshape(1, CONV_DIM))


def _conv_sample_body(z_ref, pre_ref, w_ref, g_ref, o_ref, st_ref, *, t):
    up = [pre_ref[k] for k in range(CONV_WIDTH - 1)] + [z_ref[1, k] * z_ref[2, k] for k in range(t)]
    for k in range(t):
        y = up[k] * w_ref[0:1, :] + up[k + 1] * w_ref[1:2, :] + up[k + 2] * w_ref[2:3, :]
        o_ref[k] = _conv_finish(z_ref[0, k], y, g_ref[...])
    for k in range(CONV_WIDTH - 1):
        st_ref[k] = up[t + k]


def conv_sample(z, n, t, state, conv_w, gain):
    zt = z[:, :3 * CONV_DIM].reshape(n, t, 3, CONV_DIM).transpose(2, 1, 0, 3)
    whole = lambda shape: pl.BlockSpec(shape, lambda i: (0,) * len(shape))
    out, st = pl.pallas_call(
        functools.partial(_conv_sample_body, t=t), grid=(1,),
        in_specs=[whole((3, t, n, CONV_DIM)), whole((CONV_WIDTH - 1, n, CONV_DIM)),
                  whole((CONV_WIDTH, CONV_DIM)), whole((1, CONV_DIM))],
        out_specs=[whole((t, n, CONV_DIM)), whole((CONV_WIDTH - 1, n, CONV_DIM))],
        out_shape=[jax.ShapeDtypeStruct((t, n, CONV_DIM), BF16),
                   jax.ShapeDtypeStruct((CONV_WIDTH - 1, n, CONV_DIM), F32)],
        compiler_params=_cparams("arbitrary"), name="conv_sample",
    )(zt, state.transpose(1, 0, 2), conv_w, gain.reshape(1, CONV_DIM))
    return out.transpose(1, 0, 2).reshape(n * t, CONV_DIM), st.transpose(1, 0, 2)


def _compress_body(pt_ref, *refs):
    del pt_ref
    pages = refs[:N_PAGES]
    w1_ref, pe_ref, w2_ref, kg_ref, kc_ref, vc_ref = refs[N_PAGES:]
    cpp = PAGE_SIZE // CMP_STRIDE
    n_slot = 2 * N_KV_HEADS
    for v in range(2):
        lhs = []
        for j in range(N_KV_HEADS):
            slot = v * N_KV_HEADS + j
            cols = []
            for s in range(CMP_STRIDE):
                pieces = [pages[p][pl.ds(s * n_slot + slot, cpp, stride=CMP_STRIDE * n_slot), :]
                          for p in range(N_PAGES)]
                cols.append(jnp.concatenate(pieces, axis=0).astype(BF16))
            lhs.append(jnp.concatenate(cols, axis=1))
        lhs = jnp.concatenate(lhs, axis=0)
        w1 = w1_ref[v].astype(BF16)
        r = _dot(lhs, w1)
        hpre = r[:, :HEAD_DIM] + pltpu.roll(r[:, HEAD_DIM:], r.shape[0] - 1, axis=0)
        bias = jnp.zeros((8, HEAD_DIM), F32)
        for rr in range(R_CMP):
            pe = jnp.broadcast_to(pe_ref[v, rr:rr + 1, :], (8, CMP_STRIDE * HEAD_DIM)).astype(BF16)
            bias = bias + _dot(pe, w1[:, rr * HEAD_DIM:(rr + 1) * HEAD_DIM])
        hid = jax.nn.gelu(hpre + bias[0:1, :])
        out = _dot(hid.astype(BF16), w2_ref[v].astype(BF16))
        if v == 0:
            out = _head_norm(out, kg_ref[0:1, :])
        dst = kc_ref if v == 0 else vc_ref
        for j in range(N_KV_HEADS):
            dst[0, j] = out[j * N_CMP_BLK:(j + 1) * N_CMP_BLK, :].astype(BF16)


def compress(pool, page_table, w1cat, pe_cat, phi_w2, k_norm_g):
    n = page_table.shape[0]
    page_rows = PAGE_SIZE * 2 * N_KV_HEADS
    page_spec = lambda p: pl.BlockSpec((page_rows, HEAD_DIM), lambda i, pt, p=p: (pt[i, p], 0))
    const = lambda shape: pl.BlockSpec(shape, lambda i, pt: (0,) * len(shape))
    out_spec = pl.BlockSpec((1, N_KV_HEADS, N_CMP_BLK, HEAD_DIM), lambda i, pt: (i, 0, 0, 0))
    out_sds = jax.ShapeDtypeStruct((n, N_KV_HEADS, N_CMP_BLK, HEAD_DIM), BF16)
    return pl.pallas_call(
        _compress_body,
        grid_spec=pltpu.PrefetchScalarGridSpec(
            num_scalar_prefetch=1, grid=(n,),
            in_specs=[page_spec(p) for p in range(N_PAGES)]
            + [const(w1cat.shape), const(pe_cat.shape), const(phi_w2.shape), const(k_norm_g.shape)],
            out_specs=[out_spec, out_spec]),
        out_shape=[out_sds, out_sds],
        compiler_params=_cparams("arbitrary"), name="compress",
    )(page_table, *([pool] * N_PAGES), w1cat, pe_cat, phi_w2, k_norm_g)


def _cmp_select_body(q_ref, kc_ref, vc_ref, cov_ref, o_ref, sel_ref, *, bn, tq, pos_base):
    qt = pl.program_id(1)
    grp = GQA * tq
    pairs = [(b, j) for b in range(bn) for j in range(N_KV_HEADS)]
    s = jnp.concatenate(
        [_dot_nt(q_ref[b, j * GQA:(j + 1) * GQA].reshape(grp, HEAD_DIM).astype(BF16), kc_ref[b, j]) for b, j in pairs],
        axis=0) * SCALE
    rows = len(pairs) * grp
    pos = pos_base + qt * tq + (lax.broadcasted_iota(jnp.int32, (rows, N_CMP_BLK), 0) & (tq - 1))
    blk = lax.broadcasted_iota(jnp.int32, (rows, N_CMP_BLK), 1)
    valid = (blk < N_CMP_BLK - 1) & (blk * CMP_STRIDE + (L_CMP - 1) <= pos)
    p = _masked_softmax(s, valid)
    p_bf = p.astype(BF16)
    for i, (b, j) in enumerate(pairs):
        o_ref[b, j * GQA:(j + 1) * GQA] = _dot(p_bf[i * grp:(i + 1) * grp], vc_ref[b, j]).reshape(GQA, tq, HEAD_DIM)
    psum = jnp.sum(p.reshape(len(pairs), GQA, tq, N_CMP_BLK), axis=1).reshape(len(pairs) * tq, N_CMP_BLK)
    p_hi = psum.astype(BF16)
    p_lo = (psum - p_hi.astype(F32)).astype(BF16)
    imp = _dot(p_hi, cov_ref[...]) + _dot(p_lo, cov_ref[...])
    srows = len(pairs) * tq
    blk = lax.broadcasted_iota(jnp.int32, (srows, LANE), 1)
    cur = (pos_base + qt * tq + (lax.broadcasted_iota(jnp.int32, (srows, LANE), 0) & (tq - 1))) >> 6
    forced = (blk == 0) | (blk == cur) | (blk == cur - 1)
    score = jnp.where(blk <= cur, imp + jnp.where(forced, FORCE_BONUS, 0.0), NEG_INF)
    sel = jnp.zeros((srows, LANE), F32)
    for _ in range(N_SEL):
        m = jnp.max(score, axis=-1, keepdims=True)
        first = jnp.min(jnp.where(score == m, blk, LANE), axis=-1, keepdims=True)
        hit = blk == first
        sel = jnp.where(hit, 1.0, sel)
        score = jnp.where(hit, PICKED, score)
    for i, (b, j) in enumerate(pairs):
        sel_ref[b, j] = sel[i * tq:(i + 1) * tq]


def cmp_select(q, kc, vc, cover, bn, tq, pos_base):
    n, _, t, _ = q.shape
    assert L_SEL == 64 and tq & (tq - 1) == 0
    return pl.pallas_call(
        functools.partial(_cmp_select_body, bn=bn, tq=tq, pos_base=pos_base), grid=(n // bn, t // tq),
        in_specs=[pl.BlockSpec((bn, N_Q_HEADS, tq, HEAD_DIM), lambda b, i: (b, 0, i, 0)),
                  pl.BlockSpec((bn, N_KV_HEADS, N_CMP_BLK, HEAD_DIM), lambda b, i: (b, 0, 0, 0)),
                  pl.BlockSpec((bn, N_KV_HEADS, N_CMP_BLK, HEAD_DIM), lambda b, i: (b, 0, 0, 0)),
                  pl.BlockSpec((N_CMP_BLK, LANE), lambda b, i: (0, 0))],
        out_specs=[pl.BlockSpec((bn, N_Q_HEADS, tq, HEAD_DIM), lambda b, i: (b, 0, i, 0)),
                   pl.BlockSpec((bn, N_KV_HEADS, tq, LANE), lambda b, i: (b, 0, i, 0))],
        out_shape=[jax.ShapeDtypeStruct((n, N_Q_HEADS, t, HEAD_DIM), F32),
                   jax.ShapeDtypeStruct((n, N_KV_HEADS, t, LANE), F32)],
        compiler_params=_cparams("arbitrary", "arbitrary"), name="cmp_select",
    )(q, kc, vc, cover)


def _attn_prompt_body(*refs, tq, t, selected):
    if selected:
        q_ref, k_ref, v_ref, sel_ref, e_ref, o_ref = refs
    else:
        q_ref, k_ref, v_ref, o_ref = refs
    qt = pl.program_id(2)
    rows = GQA * tq
    q = q_ref[0].reshape(rows, HEAD_DIM)

    def attend(start, nk):
        k = k_ref[0, 0, pl.ds(start, nk), :]
        v = v_ref[0, 0, pl.ds(start, nk), :]
        s = _dot_nt(q, k) * SCALE
        qpos = qt * tq + (lax.broadcasted_iota(jnp.int32, (rows, nk), 0) & (tq - 1))
        kpos = start + lax.broadcasted_iota(jnp.int32, (rows, nk), 1)
        causal = kpos <= qpos
        if selected:
            picked = _dot(sel_ref[0, 0].astype(BF16), e_ref[:, 0:nk])
            picked = jnp.concatenate([picked] * GQA, axis=0)
            mask = causal & (picked > 0.5)
        else:
            mask = causal & (kpos > qpos - WINDOW)
        p = _masked_softmax(s, mask)
        o_ref[0] = _dot(p.astype(BF16), v).reshape(GQA, tq, HEAD_DIM)

    if selected:
        n_bucket = t // WINDOW
        per = WINDOW // tq
        for b in range(n_bucket):
            @pl.when(qt // per == b)
            def _(b=b):
                attend(0, (b + 1) * WINDOW)
    else:
        attend(pl.multiple_of(jnp.maximum(qt * tq - WINDOW, 0), tq), WINDOW + tq)


def attn_prompt(q, kv, sel, expand, tq=128):
    n, _, t, _ = q.shape
    selected = sel is not None
    in_specs = [pl.BlockSpec((1, GQA, tq, HEAD_DIM), lambda b, j, i: (b, j, i, 0)),
                pl.BlockSpec((1, 1, t, HEAD_DIM), lambda b, j, i: (b, j, 0, 0)),
                pl.BlockSpec((1, 1, t, HEAD_DIM), lambda b, j, i: (b, N_KV_HEADS + j, 0, 0))]
    args = [q, kv, kv]
    if selected:
        in_specs += [pl.BlockSpec((1, 1, tq, LANE), lambda b, j, i: (b, j, i, 0)),
                     pl.BlockSpec(expand.shape, lambda b, j, i: (0, 0))]
        args += [sel, expand]
    return pl.pallas_call(
        functools.partial(_attn_prompt_body, tq=tq, t=t, selected=selected), grid=(n, N_KV_HEADS, t // tq),
        in_specs=in_specs,
        out_specs=pl.BlockSpec((1, GQA, tq, HEAD_DIM), lambda b, j, i: (b, j, i, 0)),
        out_shape=jax.ShapeDtypeStruct((n, N_Q_HEADS, t, HEAD_DIM), F32),
        compiler_params=_cparams("arbitrary", "arbitrary", "arbitrary"),
        name="attn_prompt_sel" if selected else "attn_prompt_win",
    )(*args)


T_PAD = 8


def _pad_keys(x):
    return jnp.concatenate([x, jnp.zeros((PAGE_SIZE - x.shape[0], x.shape[1]), x.dtype)], axis=0)


def _attn_sample_sel_body(pt_ref, *refs, t_real):
    del pt_ref
    pages = refs[:N_PAGES]
    q_ref, sel_ref, new_ref, e_ref, o_ref = refs[N_PAGES:]
    rows = GQA * T_PAD
    tok = lax.broadcasted_iota(jnp.int32, (rows, PAGE_SIZE), 0) & (T_PAD - 1)
    col = lax.broadcasted_iota(jnp.int32, (rows, PAGE_SIZE), 1)
    for j in range(N_KV_HEADS):
        kc0, vc0 = j * HEAD_DIM, KV_DIM + j * HEAD_DIM
        q = q_ref[0, j * GQA:(j + 1) * GQA].reshape(rows, HEAD_DIM).astype(BF16)
        k_rows = pl.ds(j, PAGE_SIZE, stride=N_SLOT)
        v_rows = pl.ds(N_KV_HEADS + j, PAGE_SIZE, stride=N_SLOT)
        parts = [_dot_nt(q, pages[p][k_rows, :].astype(BF16)) for p in range(N_PAGES)]
        parts.append(_dot_nt(q, _pad_keys(new_ref[0, :, kc0:kc0 + HEAD_DIM]).astype(BF16)))
        s = jnp.concatenate(parts, axis=1) * SCALE
        sel = sel_ref[0, j]
        sel4 = jnp.concatenate([sel] * GQA, axis=0)
        picked = _dot(sel4.astype(BF16), e_ref[...])
        new_blk = PAST_LEN // L_SEL
        new_ok = (col < t_real) & (col <= tok)
        new_picked = jnp.where(new_ok, _lane_pick(sel4, new_blk), 0.0)
        mask = jnp.concatenate([picked, new_picked], axis=1) > 0.5
        p = _masked_softmax(s, mask).astype(BF16)
        o = _dot(p[:, PAST_LEN:], _pad_keys(new_ref[0, :, vc0:vc0 + HEAD_DIM]).astype(BF16))
        for pg in range(N_PAGES):
            o = o + _dot(p[:, pg * PAGE_SIZE:(pg + 1) * PAGE_SIZE], pages[pg][v_rows, :].astype(BF16))
        o_ref[0, j * GQA:(j + 1) * GQA] = o.reshape(GQA, T_PAD, HEAD_DIM)


def attn_sample_sel(pool, page_table, q, sel, new_rows, expand, t_real):
    n = page_table.shape[0]
    cols = new_rows.shape[-1]
    page_spec = lambda p: pl.BlockSpec((PAGE_SIZE * N_SLOT, HEAD_DIM), lambda i, pt, p=p: (pt[i, p], 0))
    return pl.pallas_call(
        functools.partial(_attn_sample_sel_body, t_real=t_real),
        grid_spec=pltpu.PrefetchScalarGridSpec(
            num_scalar_prefetch=1, grid=(n,),
            in_specs=[page_spec(p) for p in range(N_PAGES)]
            + [pl.BlockSpec((1, N_Q_HEADS, T_PAD, HEAD_DIM), lambda i, pt: (i, 0, 0, 0)),
               pl.BlockSpec((1, N_KV_HEADS, T_PAD, LANE), lambda i, pt: (i, 0, 0, 0)),
               pl.BlockSpec((1, T_PAD, cols), lambda i, pt: (i, 0, 0)),
               pl.BlockSpec(expand.shape, lambda i, pt: (0, 0))],
            out_specs=pl.BlockSpec((1, N_Q_HEADS, T_PAD, HEAD_DIM), lambda i, pt: (i, 0, 0, 0))),
        out_shape=jax.ShapeDtypeStruct((n, N_Q_HEADS, T_PAD, HEAD_DIM), F32),
        compiler_params=_cparams("arbitrary"), name="attn_sample_sel",
    )(page_table, *([pool] * N_PAGES), q, sel, new_rows, expand)


def _attn_sample_win_body(win_ref, q_ref, new_ref, newrows_ref, o_ref, wout_ref, *, t_real):
    rows = GQA * T_PAD
    w_buf = win_ref.shape[0] // N_SLOT
    tok_o = lax.broadcasted_iota(jnp.int32, (rows, w_buf), 0) & (T_PAD - 1)
    col_o = lax.broadcasted_iota(jnp.int32, (rows, w_buf), 1)
    tok_n = lax.broadcasted_iota(jnp.int32, (rows, PAGE_SIZE), 0) & (T_PAD - 1)
    col_n = lax.broadcasted_iota(jnp.int32, (rows, PAGE_SIZE), 1)
    old_ok = jnp.where(col_o + (WINDOW - w_buf) > tok_o, 1.0, 0.0)
    new_ok = jnp.where((col_n < t_real) & (col_n <= tok_n), 1.0, 0.0)
    mask = jnp.concatenate([old_ok, new_ok], axis=1) > 0.5
    for j in range(N_KV_HEADS):
        kc0, vc0 = j * HEAD_DIM, KV_DIM + j * HEAD_DIM
        q = q_ref[0, j * GQA:(j + 1) * GQA].reshape(rows, HEAD_DIM).astype(BF16)
        k_new = _pad_keys(new_ref[0, :, kc0:kc0 + HEAD_DIM]).astype(BF16)
        v_new = _pad_keys(new_ref[0, :, vc0:vc0 + HEAD_DIM]).astype(BF16)
        k_old = win_ref[pl.ds(j, w_buf, stride=N_SLOT), :].astype(BF16)
        v_old = win_ref[pl.ds(N_KV_HEADS + j, w_buf, stride=N_SLOT), :].astype(BF16)
        s = jnp.concatenate([_dot_nt(q, k_old), _dot_nt(q, k_new)], axis=1) * SCALE
        p = _masked_softmax(s, mask).astype(BF16)
        o = _dot(p[:, :w_buf], v_old) + _dot(p[:, w_buf:], v_new)
        o_ref[0, j * GQA:(j + 1) * GQA] = o.reshape(GQA, T_PAD, HEAD_DIM)
    keep = (w_buf - t_real) * N_SLOT
    wout_ref[0:keep, :] = win_ref[t_real * N_SLOT:, :]
    wout_ref[keep:, :] = newrows_ref[...]


def attn_sample_win(win, q, new_rows, new_cache_rows, t_real):
    n = q.shape[0]
    cols = new_rows.shape[-1]
    buf_rows = win.shape[0] // n
    return pl.pallas_call(
        functools.partial(_attn_sample_win_body, t_real=t_real), grid=(n,),
        in_specs=[pl.BlockSpec((buf_rows, HEAD_DIM), lambda i: (i, 0)),
                  pl.BlockSpec((1, N_Q_HEADS, T_PAD, HEAD_DIM), lambda i: (i, 0, 0, 0)),
                  pl.BlockSpec((1, T_PAD, cols), lambda i: (i, 0, 0)),
                  pl.BlockSpec((t_real * N_SLOT, HEAD_DIM), lambda i: (i, 0))],
        out_specs=[pl.BlockSpec((1, N_Q_HEADS, T_PAD, HEAD_DIM), lambda i: (i, 0, 0, 0)),
                   pl.BlockSpec((buf_rows, HEAD_DIM), lambda i: (i, 0))],
        out_shape=[jax.ShapeDtypeStruct((n, N_Q_HEADS, T_PAD, HEAD_DIM), F32),
                   jax.ShapeDtypeStruct(win.shape, F32)],
        compiler_params=_cparams("arbitrary"), name="attn_sample_win",
    )(win, q, new_rows, new_cache_rows)


def _combine_body(oc_ref, os_ref, ow_ref, gl_ref, g_ref, out_ref, *, rows):
    gates = jax.nn.sigmoid(gl_ref[...])
    outs = []
    sq = jnp.zeros((rows, 1), F32)
    for h in range(N_Q_HEADS):
        a = (_lane_pick(gates, h) * oc_ref[:, h].reshape(rows, HEAD_DIM)
             + _lane_pick(gates, N_Q_HEADS + h) * os_ref[:, h].reshape(rows, HEAD_DIM)
             + _lane_pick(gates, 2 * N_Q_HEADS + h) * ow_ref[:, h].reshape(rows, HEAD_DIM))
        outs.append(a)
        sq = sq + jnp.sum(a * a, axis=-1, keepdims=True)
    inv = lax.rsqrt(sq * (1.0 / ATTN_DIM) + RMS_EPS)
    for h in range(N_Q_HEADS):
        sl = slice(h * HEAD_DIM, (h + 1) * HEAD_DIM)
        out_ref[:, sl] = ((outs[h] * inv) * g_ref[:, sl]).astype(out_ref.dtype)


def combine(o_cmp, o_sel, o_win, gate_logits, gain, bn, tt):
    n, _, t, _ = o_cmp.shape
    nt = t // tt
    rows = bn * tt
    o_spec = pl.BlockSpec((bn, N_Q_HEADS, tt, HEAD_DIM), lambda b, i: (b, 0, i, 0))
    return pl.pallas_call(
        functools.partial(_combine_body, rows=rows), grid=(n // bn, nt),
        in_specs=[o_spec, o_spec, o_spec,
                  pl.BlockSpec((rows, LANE), lambda b, i: (b * nt + i, 0)),
                  pl.BlockSpec((1, ATTN_DIM), lambda b, i: (0, 0))],
        out_specs=pl.BlockSpec((rows, ATTN_DIM), lambda b, i: (b * nt + i, 0)),
        out_shape=jax.ShapeDtypeStruct((n * t, ATTN_DIM), BF16),
        compiler_params=_cparams("arbitrary", "arbitrary"), name="combine",
    )(o_cmp, o_sel, o_win, gate_logits, gain.reshape(1, ATTN_DIM))


def _router_body(x_ref, w_ref, b_ref, ids_ref, wts_ref):
    logits = _dot(x_ref[...].astype(BF16), w_ref[...].astype(BF16)) + b_ref[...]
    lane = lax.broadcasted_iota(jnp.int32, logits.shape, 1)
    is_grp = (lane >= N_EXPERTS) & (lane < N_EXPERTS + N_GROUPS)
    gl = jnp.where(is_grp, logits, NEG_INF)
    ge = jnp.where(is_grp, jnp.exp(gl - jnp.max(gl, axis=-1, keepdims=True)), 0.0)
    p_grp = ge / jnp.sum(ge, axis=-1, keepdims=True)
    g_val = jnp.max(p_grp, axis=-1, keepdims=True)
    g_idx = jnp.min(jnp.where(is_grp & (p_grp == g_val), lane, 2 * LANE), axis=-1, keepdims=True) - N_EXPERTS
    lo = g_idx * EXPERTS_PER_GROUP
    in_grp = (lane >= lo) & (lane < lo + EXPERTS_PER_GROUP)
    el = jnp.where(in_grp, logits, NEG_INF)
    ee = jnp.where(in_grp, jnp.exp(el - jnp.max(el, axis=-1, keepdims=True)), 0.0)
    p_e = ee / jnp.sum(ee, axis=-1, keepdims=True)
    cand = jnp.where(in_grp, p_e, -1.0)
    e1 = jnp.max(cand, axis=-1, keepdims=True)
    i1 = jnp.min(jnp.where(cand == e1, lane, 2 * LANE), axis=-1, keepdims=True)
    cand = jnp.where(lane == i1, -1.0, cand)
    e2 = jnp.max(cand, axis=-1, keepdims=True)
    i2 = jnp.min(jnp.where(cand == e2, lane, 2 * LANE), axis=-1, keepdims=True)
    tot = e1 + e2
    ids_ref[...] = jnp.where(lane == 0, i1, jnp.where(lane == 1, i2, 0))
    wts_ref[...] = jnp.where(lane == 0, g_val * e1 / tot, jnp.where(lane == 1, g_val * e2 / tot, 0.0))


def router(xt, w_router, b_router, tm=256):
    m, d = xt.shape
    out_spec = pl.BlockSpec((tm, LANE), lambda i: (i, 0))
    return pl.pallas_call(
        _router_body, grid=(m // tm,),
        in_specs=[pl.BlockSpec((tm, d), lambda i: (i, 0)),
                  pl.BlockSpec((d, LANE), lambda i: (0, 0)),
                  pl.BlockSpec((1, LANE), lambda i: (0, 0))],
        out_specs=[out_spec, out_spec],
        out_shape=[jax.ShapeDtypeStruct((m, LANE), jnp.int32), jax.ShapeDtypeStruct((m, LANE), F32)],
        compiler_params=_cparams("arbitrary"), name="router",
    )(xt, w_router, b_router)


def _gather_rows(idx_ref, n_rows, src_hbm, dst, sem, wait):
    def body(r, c):
        src = 0 if wait else idx_ref[0, 0, r]
        cp = pltpu.make_async_copy(src_hbm.at[pl.ds(src, 1), :], dst.at[pl.ds(r, 1), :], sem)
        cp.wait() if wait else cp.start()
        return c
    lax.fori_loop(0, n_rows, body, 0, unroll=8)


def _moe_ffn_body(te_ref, cur_ref, nxt_ref, x_hbm, rw_ref, wg_ref, wu_ref, wd_ref, ys_ref, xbuf, sem):
    del te_ref
    i = pl.program_id(0)
    n = pl.num_programs(0)
    tm = xbuf.shape[1]
    slot = lax.rem(i, 2)

    @pl.when(i == 0)
    def _():
        _gather_rows(cur_ref, tm, x_hbm, xbuf.at[0], sem.at[0], wait=False)

    @pl.when(i + 1 < n)
    def _():
        _gather_rows(nxt_ref, tm, x_hbm, xbuf.at[1 - slot], sem.at[1 - slot], wait=False)

    _gather_rows(cur_ref, tm, x_hbm, xbuf.at[slot], sem.at[slot], wait=True)
    x = xbuf[slot].astype(BF16)
    hid = jax.nn.silu(_dot(x, wg_ref[0])) * _dot(x, wu_ref[0])
    gate = jnp.concatenate([rw_ref[...]] * (hid.shape[1] // LANE), axis=1)
    ys_ref[...] = _dot((hid * gate).astype(BF16), wd_ref[0])


def moe_ffn(xt, tile_expert, row_token, row_weight, w_gate, w_up, w_down, tm):
    n_tiles = tile_expert.shape[0]
    d = xt.shape[1]
    f = w_gate.shape[-1]
    smem_rows = lambda fn: pl.BlockSpec((1, 1, tm), fn, memory_space=pltpu.SMEM)
    return pl.pallas_call(
        _moe_ffn_body,
        grid_spec=pltpu.PrefetchScalarGridSpec(
            num_scalar_prefetch=1, grid=(n_tiles,),
            in_specs=[smem_rows(lambda i, te: (i, 0, 0)),
                      smem_rows(lambda i, te: (jnp.minimum(i + 1, n_tiles - 1), 0, 0)),
                      pl.BlockSpec(memory_space=pl.ANY),
                      pl.BlockSpec((tm, LANE), lambda i, te: (i, 0)),
                      pl.BlockSpec((1, d, f), lambda i, te: (te[i], 0, 0)),
                      pl.BlockSpec((1, d, f), lambda i, te: (te[i], 0, 0)),
                      pl.BlockSpec((1, f, d), lambda i, te: (te[i], 0, 0))],
            out_specs=pl.BlockSpec((tm, d), lambda i, te: (i, 0)),
            scratch_shapes=[pltpu.VMEM((2, tm, d), F32), pltpu.SemaphoreType.DMA((2,))]),
        out_shape=jax.ShapeDtypeStruct((n_tiles * tm, d), F32),
        compiler_params=_cparams("arbitrary"), name="moe_ffn",
    )(tile_expert, row_token, row_token, xt, row_weight, w_gate, w_up, w_down)


def _moe_combine_body(cur_ref, nxt_ref, ys_hbm, h_ref, y_ref, buf, sem):
    i = pl.program_id(0)
    n = pl.num_programs(0)
    rows = buf.shape[1]
    tm = h_ref.shape[0]
    slot = lax.rem(i, 2)

    @pl.when(i == 0)
    def _():
        _gather_rows(cur_ref, rows, ys_hbm, buf.at[0], sem.at[0], wait=False)

    @pl.when(i + 1 < n)
    def _():
        _gather_rows(nxt_ref, rows, ys_hbm, buf.at[1 - slot], sem.at[1 - slot], wait=False)

    _gather_rows(cur_ref, rows, ys_hbm, buf.at[slot], sem.at[slot], wait=True)
    y_ref[...] = h_ref[...] + buf[slot, 0:tm, :] + buf[slot, tm:rows, :]


def moe_combine(ys, pair_row, h, tm=128):
    m, d = h.shape
    n_tiles = m // tm
    smem_rows = lambda fn: pl.BlockSpec((1, 1, 2 * tm), fn, memory_space=pltpu.SMEM)
    return pl.pallas_call(
        _moe_combine_body, grid=(n_tiles,),
        in_specs=[smem_rows(lambda i: (i, 0, 0)),
                  smem_rows(lambda i: (jnp.minimum(i + 1, n_tiles - 1), 0, 0)),
                  pl.BlockSpec(memory_space=pl.ANY),
                  pl.BlockSpec((tm, d), lambda i: (i, 0))],
        out_specs=pl.BlockSpec((tm, d), lambda i: (i, 0)),
        out_shape=jax.ShapeDtypeStruct((m, d), F32),
        scratch_shapes=[pltpu.VMEM((2, 2 * tm, d), F32), pltpu.SemaphoreType.DMA((2,))],
        compiler_params=_cparams("arbitrary"), name="moe_combine",
    )(pair_row, pair_row, ys, h)


def moe_routed(xt, ids, wts, h, w_gate, w_up, w_down, tm):
    m = xt.shape[0]
    n_pair = 2 * m
    n_tiles = (n_pair + N_EXPERTS * (tm - 1)) // tm + 1
    flat_e = ids[:, :2].reshape(n_pair)
    flat_w = wts[:, :2].reshape(n_pair)
    order = jnp.argsort(flat_e, stable=True).astype(jnp.int32)
    sorted_e = flat_e[order]
    counts = jnp.sum(flat_e[:, None] == jnp.arange(N_EXPERTS, dtype=jnp.int32)[None, :], axis=0, dtype=jnp.int32)
    padded = ((counts + tm - 1) // tm) * tm
    pad_end = jnp.cumsum(padded)
    start = jnp.cumsum(counts) - counts
    dest = (pad_end - padded)[sorted_e] + jnp.arange(n_pair, dtype=jnp.int32) - start[sorted_e]
    row_token = jnp.zeros((n_tiles * tm,), jnp.int32).at[dest].set(order // 2)
    row_weight = jnp.zeros((n_tiles * tm,), F32).at[dest].set(flat_w[order])
    pair_row = jnp.zeros((n_pair,), jnp.int32).at[order].set(dest).reshape(m, 2)
    tile_start = jnp.arange(n_tiles, dtype=jnp.int32) * tm
    tile_expert = jnp.minimum(jnp.sum(tile_start[:, None] >= pad_end[None, :], axis=1), N_EXPERTS - 1).astype(jnp.int32)
    ys = moe_ffn(xt, tile_expert, row_token.reshape(n_tiles, 1, tm),
                 jnp.broadcast_to(row_weight[:, None], (n_tiles * tm, LANE)), w_gate, w_up, w_down, tm)
    tc = 128
    pair_tiles = pair_row.reshape(m // tc, tc, 2).transpose(0, 2, 1).reshape(m // tc, 1, 2 * tc)
    return moe_combine(ys, pair_tiles, h, tm=tc)


def _cover_matrix(nsb):
    i = np.arange(N_CMP_BLK)[:, None]
    j = np.arange(LANE)[None, :]
    m = np.zeros((N_CMP_BLK, LANE), np.float32)
    for a in range(L_SEL // CMP_STRIDE):
        for c in range(R_CMP):
            m += (i == (L_SEL // CMP_STRIDE) * j + a - c)
    m[N_CMP_BLK - 1:, :] = 0.0
    m[:, nsb:] = 0.0
    return jnp.asarray(m, BF16)


def _expand_matrix(n_keys):
    b = np.arange(LANE)[:, None]
    k = np.arange(n_keys)[None, :]
    return jnp.asarray((k // L_SEL == b).astype(np.float32), BF16)


def _to_heads(q, n, t):
    return q.reshape(n, t, N_Q_HEADS, HEAD_DIM).transpose(0, 2, 1, 3)


def kernel(x_prompt, x_sample, cache_cmp_kv, cache_sel_kv, state_win_kv, state_conv, page_table, norm_mix_g, w_in,
           conv_w, q_norm_g, k_norm_g, phi_pe, phi_w1, phi_w2, out_norm_g, w_out, norm_ffn_g, w_group_router,
           b_group_router, w_expert_router, b_expert_router, w_gate, w_up, w_down):
    n_p, t_p, d = x_prompt.shape
    n_s, t_s, _ = x_sample.shape
    assert w_in.shape[0] == 1 and t_s < CMP_STRIDE and t_s <= T_PAD and t_p % WINDOW == 0
    kv_cols = 2 * KV_DIM
    w_in_main = w_in[0]
    w_in_gate = jnp.pad(w_in[0][:, Z_MAIN:], ((0, 0), (0, LANE - 3 * N_Q_HEADS)))
    w_o = w_out[0]
    w1cat = phi_w1[0].reshape(2, R_CMP, CMP_STRIDE, HEAD_DIM, HEAD_DIM).transpose(0, 2, 3, 1, 4)
    w1cat = w1cat.reshape(2, CMP_STRIDE * HEAD_DIM, R_CMP * HEAD_DIM)
    pe_cat = phi_pe[0].reshape(2, R_CMP, CMP_STRIDE * HEAD_DIM)
    w_router = jnp.pad(jnp.concatenate([w_expert_router[0], w_group_router[0]], axis=1),
                       ((0, 0), (0, LANE - N_EXPERTS - N_GROUPS)))
    b_router = jnp.pad(jnp.concatenate([b_expert_router[0], b_group_router[0]]),
                       (0, LANE - N_EXPERTS - N_GROUPS)).reshape(1, LANE)
    wg_bf, wu_bf, wd_bf = w_gate[0].astype(BF16), w_up[0].astype(BF16), w_down[0].astype(BF16)

    def project(x2d):
        xn = rmsnorm_cast(x2d, norm_mix_g[0])
        z = matmul([(xn, w_in_main, 0)], tn=512)
        gate_logits = matmul([(xn, w_in_gate, 0)], tn=LANE)
        return z, gate_logits

    def finish(x2d, conv_out, attn_out, tm):
        h = matmul([(conv_out, w_o, 0), (attn_out, w_o, 1)], res=x2d)
        xt = rmsnorm_cast(h, norm_ffn_g[0], dtype=F32)
        ids, wts = router(xt, w_router, b_router)
        return moe_routed(xt, ids, wts, h, wg_bf, wu_bf, wd_bf, tm)

    xp = x_prompt.reshape(n_p * t_p, d)
    z, glog = project(xp)
    qh, ks_h, kw_h, kvc, kvs, kvw = postproj_prompt(z, n_p, t_p, q_norm_g[0], k_norm_g[0])
    conv_out, conv_last = conv_prompt(z, n_p, t_p, conv_w[0], out_norm_g[0][:CONV_DIM])
    ident = jnp.arange(n_p * (t_p // PAGE_SIZE), dtype=jnp.int32).reshape(n_p, t_p // PAGE_SIZE)
    kc, vc = compress(kvc, ident, w1cat, pe_cat, phi_w2[0], k_norm_g[0])
    o_cmp, sel = cmp_select(qh, kc, vc, _cover_matrix(t_p // L_SEL), bn=1, tq=256, pos_base=0)
    o_sel = attn_prompt(qh, ks_h, sel, _expand_matrix(t_p))
    o_win = attn_prompt(qh, kw_h, None, None)
    attn_out = combine(o_cmp, o_sel, o_win, glog, out_norm_g[0][CONV_DIM:], bn=1, tt=256)
    y_prompt = finish(xp, conv_out, attn_out, 256).reshape(n_p, t_p, d)
    kv_shape = (1, n_p, t_p, 2, N_KV_HEADS, HEAD_DIM)
    w_keep = min(WINDOW, t_p)
    prompt_win = kvw.reshape(kv_shape)[:, :, t_p - w_keep:]
    prompt_conv = conv_last[:, 8 - (CONV_WIDTH - 1):, :][None]

    xs = x_sample.reshape(n_s * t_s, d)
    z, glog = project(xs)
    q, kvc_s, kvs_s, kvw_s = postproj(z, q_norm_g[0], k_norm_g[0])
    conv_out, conv_state = conv_sample(z, n_s, t_s, state_conv[0], conv_w[0], out_norm_g[0][:CONV_DIM])
    pad_t = lambda a: jnp.pad(a, ((0, 0), (0, T_PAD - t_s), (0, 0)))
    qh = jnp.pad(_to_heads(q, n_s, t_s), ((0, 0), (0, 0), (0, T_PAD - t_s), (0, 0)))
    pool_cmp = cache_cmp_kv[0].reshape(-1, HEAD_DIM)
    pool_sel = cache_sel_kv[0].reshape(-1, HEAD_DIM)
    kc, vc = compress(pool_cmp, page_table, w1cat, pe_cat, phi_w2[0], k_norm_g[0])
    o_cmp, sel = cmp_select(qh, kc, vc, _cover_matrix(PAST_LEN // L_SEL + 1), bn=16, tq=T_PAD, pos_base=PAST_LEN)
    o_sel = attn_sample_sel(pool_sel, page_table, qh, sel, pad_t(kvs_s.reshape(n_s, t_s, kv_cols)),
                            _expand_matrix(PAST_LEN), t_s)
    win = state_win_kv[0].reshape(-1, HEAD_DIM)
    o_win, win_new = attn_sample_win(win, qh, pad_t(kvw_s.reshape(n_s, t_s, kv_cols)),
                                     kvw_s.reshape(-1, HEAD_DIM), t_s)
    glog_pad = pad_t(glog.reshape(n_s, t_s, LANE)).reshape(n_s * T_PAD, LANE)
    attn_out = combine(o_cmp, o_sel, o_win, glog_pad, out_norm_g[0][CONV_DIM:], bn=32, tt=T_PAD)
    attn_out = attn_out.reshape(n_s, T_PAD, ATTN_DIM)[:, :t_s].reshape(n_s * t_s, ATTN_DIM)
    y_sample = finish(xs, conv_out, attn_out, 128).reshape(n_s, t_s, d)
    s_shape = (1, n_s, t_s, 2, N_KV_HEADS, HEAD_DIM)

    return (y_prompt, y_sample, kvc.reshape(kv_shape), kvs.reshape(kv_shape), prompt_win, prompt_conv,
            kvc_s.reshape(s_shape), kvs_s.reshape(s_shape),
            win_new.reshape(1, n_s, -1, 2, N_KV_HEADS, HEAD_DIM), conv_state[None])
```

```python
import functools

import numpy as np
import jax
import jax.numpy as jnp
from jax import lax
from jax.experimental import pallas as pl
from jax.experimental.pallas import tpu as pltpu

F32 = jnp.float32
BF16 = jnp.bfloat16

D_MODEL = 4096
PAST_LEN = 2048
PAGE_SIZE = 128
HEAD_DIM = 128
CONV_DIM = 2048
N_Q_HEADS = 16
N_KV_HEADS = 4
GQA = 4
ATTN_DIM = 2048
KV_DIM = 512
CONV_WIDTH = 3
L_CMP = 32
CMP_STRIDE = 16
R_CMP = 2
L_SEL = 64
N_SEL = 8
WINDOW = 512
FORCE_BONUS = 1e3
SCALE = HEAD_DIM ** -0.5
N_GROUPS = 4
EXPERTS_PER_GROUP = 4
N_EXPERTS = 16
D_FF_EXPERT = 512
RMS_EPS = 1e-6
NEG_INF = -1e30
TINY = 1e-30
PICKED = -3e38

N_PAGES = PAST_LEN // PAGE_SIZE
N_CMP_BLK = 128
Z_MAIN = 3 * CONV_DIM + ATTN_DIM + 6 * KV_DIM
LANE = 128
VMEM_LIMIT = 56 * 1024 * 1024


def _cparams(*sem):
    return pltpu.CompilerParams(dimension_semantics=sem, vmem_limit_bytes=VMEM_LIMIT)


def _masked_softmax(s, mask):
    s = jnp.where(mask, s, NEG_INF)
    m = jnp.max(s, axis=-1, keepdims=True)
    e = jnp.where(mask, jnp.exp(s - m), 0.0)
    return e / jnp.maximum(jnp.sum(e, axis=-1, keepdims=True), TINY)


def _dot_nt(a, b):
    return lax.dot_general(a, b, (((1,), (1,)), ((), ())), preferred_element_type=F32)


def _dot(a, b):
    return jnp.dot(a, b, preferred_element_type=F32)


def _lane_pick(x, c):
    lane = lax.broadcasted_iota(jnp.int32, x.shape, 1)
    return jnp.sum(jnp.where(lane == c, x, 0.0), axis=-1, keepdims=True)


def _rmsnorm_body(x_ref, g_ref, o_ref):
    x = x_ref[...]
    inv = lax.rsqrt(jnp.mean(x * x, axis=-1, keepdims=True) + RMS_EPS)
    o_ref[...] = ((x * inv) * g_ref[...]).astype(o_ref.dtype)


def rmsnorm_cast(x, g, dtype=BF16, tm=256):
    m, d = x.shape
    return pl.pallas_call(
        _rmsnorm_body, grid=(m // tm,),
        in_specs=[pl.BlockSpec((tm, d), lambda i: (i, 0)), pl.BlockSpec((1, d), lambda i: (0, 0))],
        out_specs=pl.BlockSpec((tm, d), lambda i: (i, 0)),
        out_shape=jax.ShapeDtypeStruct((m, d), dtype),
        compiler_params=_cparams("arbitrary"), name="rmsnorm_cast",
    )(x, g.reshape(1, d))


def _matmul_body(*refs, n_w, has_res, tiles):
    per = n_w + (1 if has_res else 0)
    n_groups = len(tiles) - 1
    o_ref = refs[n_w + n_groups * per]
    wbf = refs[n_w + n_groups * per + 1:]
    i = pl.program_id(1)

    @pl.when(i == 0)
    def _():
        for p in range(n_w):
            wbf[p][...] = refs[p][...].astype(BF16)

    for g in range(n_groups):
        grp = refs[n_w + g * per:n_w + (g + 1) * per]

        def compute(grp=grp):
            acc = _dot(grp[0][...], wbf[0][...])
            for p in range(1, n_w):
                acc = acc + _dot(grp[p][...], wbf[p][...])
            if has_res:
                acc = acc + grp[n_w][...]
            o_ref[...] = acc

        if n_groups == 1:
            compute()
        else:
            pl.when((i >= tiles[g]) & (i < tiles[g + 1]))(compute)


def matmul(groups, weights, tm=512, tn=512):
    ms = [g[0][0].shape[0] for g in groups]
    tm = min([tm] + ms)
    assert all(m % tm == 0 for m in ms)
    n = (weights[0][0].shape[1] // tn) * tn
    has_res = groups[0][1] is not None
    tiles = [0]
    for m in ms:
        tiles.append(tiles[-1] + m // tm)
    in_specs, args, scratch = [], [], []
    ks = [a.shape[1] for a in groups[0][0]]
    for (w, kb), k in zip(weights, ks):
        in_specs.append(pl.BlockSpec((k, tn), lambda j, i, kb=kb: (kb, j)))
        args.append(w)
        scratch.append(pltpu.VMEM((k, tn), BF16))
    for g, (a_list, res) in enumerate(groups):
        lo, hi = tiles[g], tiles[g + 1]
        row = lambda i, lo=lo, hi=hi: jnp.clip(i, lo, hi - 1) - lo
        for a, k in zip(a_list, ks):
            in_specs.append(pl.BlockSpec((tm, k), lambda j, i, row=row: (row(i), 0)))
            args.append(a)
        if has_res:
            in_specs.append(pl.BlockSpec((tm, tn), lambda j, i, row=row: (row(i), j)))
            args.append(res)
    return pl.pallas_call(
        functools.partial(_matmul_body, n_w=len(weights), has_res=has_res, tiles=tuple(tiles)),
        grid=(n // tn, tiles[-1]), in_specs=in_specs,
        out_specs=pl.BlockSpec((tm, tn), lambda j, i: (i, j)),
        out_shape=jax.ShapeDtypeStruct((sum(ms), n), F32), scratch_shapes=scratch,
        compiler_params=_cparams("arbitrary", "arbitrary"), name="matmul",
    )(*args)


def _head_norm(x, g):
    inv = lax.rsqrt(jnp.mean(x * x, axis=-1, keepdims=True) + RMS_EPS)
    return (x * inv) * g


def _postproj_body(zq_ref, zc_ref, zs_ref, zw_ref, qg_ref, kg_ref, q_ref, kvc_ref, kvs_ref, kvw_ref):
    for h in range(N_Q_HEADS):
        sl = slice(h * HEAD_DIM, (h + 1) * HEAD_DIM)
        q_ref[:, sl] = _head_norm(zq_ref[:, sl], qg_ref[...]).astype(q_ref.dtype)
    kvc_ref[...] = zc_ref[...]
    for h in range(N_KV_HEADS):
        sl = slice(h * HEAD_DIM, (h + 1) * HEAD_DIM)
        kvs_ref[:, sl] = _head_norm(zs_ref[:, sl], kg_ref[1:2, :])
        kvw_ref[:, sl] = _head_norm(zw_ref[:, sl], kg_ref[2:3, :])
    kvs_ref[:, KV_DIM:] = zs_ref[:, KV_DIM:]
    kvw_ref[:, KV_DIM:] = zw_ref[:, KV_DIM:]


def postproj(z, q_norm_g, k_norm_g, tm=256):
    m = z.shape[0]
    kv = 2 * KV_DIM
    q0 = 3 * CONV_DIM // ATTN_DIM
    c0 = (3 * CONV_DIM + ATTN_DIM) // kv
    return pl.pallas_call(
        _postproj_body, grid=(m // tm,),
        in_specs=[pl.BlockSpec((tm, ATTN_DIM), lambda i: (i, q0)),
                  pl.BlockSpec((tm, kv), lambda i: (i, c0)),
                  pl.BlockSpec((tm, kv), lambda i: (i, c0 + 1)),
                  pl.BlockSpec((tm, kv), lambda i: (i, c0 + 2)),
                  pl.BlockSpec((1, HEAD_DIM), lambda i: (0, 0)),
                  pl.BlockSpec((3, HEAD_DIM), lambda i: (0, 0))],
        out_specs=[pl.BlockSpec((tm, ATTN_DIM), lambda i: (i, 0)),
                   pl.BlockSpec((tm, kv), lambda i: (i, 0)),
                   pl.BlockSpec((tm, kv), lambda i: (i, 0)),
                   pl.BlockSpec((tm, kv), lambda i: (i, 0))],
        out_shape=[jax.ShapeDtypeStruct((m, ATTN_DIM), F32)] + [jax.ShapeDtypeStruct((m, kv), F32)] * 3,
        compiler_params=_cparams("arbitrary"), name="postproj",
    )(z, z, z, z, q_norm_g.reshape(1, HEAD_DIM), k_norm_g)


N_SLOT = 2 * N_KV_HEADS


def _postproj_prompt_body(zq_ref, zc_ref, zs_ref, zw_ref, qg_ref, kg_ref,
                          q_ref, ks_ref, kw_ref, kvc_ref, kvs_ref, kvw_ref):
    tm = zq_ref.shape[0]
    for h in range(N_Q_HEADS):
        sl = slice(h * HEAD_DIM, (h + 1) * HEAD_DIM)
        q_ref[0, h] = _head_norm(zq_ref[:, sl], qg_ref[...]).astype(BF16)
    for c in range(N_SLOT):
        sl = slice(c * HEAD_DIM, (c + 1) * HEAD_DIM)
        rows = pl.ds(c, tm, stride=N_SLOT)
        kvc_ref[rows, :] = zc_ref[:, sl]
        xs, xw = zs_ref[:, sl], zw_ref[:, sl]
        if c < N_KV_HEADS:
            xs, xw = _head_norm(xs, kg_ref[1:2, :]), _head_norm(xw, kg_ref[2:3, :])
        kvs_ref[rows, :] = xs
        kvw_ref[rows, :] = xw
        ks_ref[0, c] = xs.astype(BF16)
        kw_ref[0, c] = xw.astype(BF16)


def postproj_prompt(z, n, t, q_norm_g, k_norm_g, tm=256):
    kv = 2 * KV_DIM
    nt = t // tm
    q0 = 3 * CONV_DIM // ATTN_DIM
    c0 = (3 * CONV_DIM + ATTN_DIM) // kv
    rows_spec = pl.BlockSpec((tm * N_SLOT, HEAD_DIM), lambda b, i: (b * nt + i, 0))
    slot_spec = pl.BlockSpec((1, N_SLOT, tm, HEAD_DIM), lambda b, i: (b, 0, i, 0))
    rows_sds = jax.ShapeDtypeStruct((n * t * N_SLOT, HEAD_DIM), F32)
    slot_sds = jax.ShapeDtypeStruct((n, N_SLOT, t, HEAD_DIM), BF16)
    return pl.pallas_call(
        _postproj_prompt_body, grid=(n, nt),
        in_specs=[pl.BlockSpec((tm, ATTN_DIM), lambda b, i: (b * nt + i, q0)),
                  pl.BlockSpec((tm, kv), lambda b, i: (b * nt + i, c0)),
                  pl.BlockSpec((tm, kv), lambda b, i: (b * nt + i, c0 + 1)),
                  pl.BlockSpec((tm, kv), lambda b, i: (b * nt + i, c0 + 2)),
                  pl.BlockSpec((1, HEAD_DIM), lambda b, i: (0, 0)),
                  pl.BlockSpec((3, HEAD_DIM), lambda b, i: (0, 0))],
        out_specs=[pl.BlockSpec((1, N_Q_HEADS, tm, HEAD_DIM), lambda b, i: (b, 0, i, 0)),
                   slot_spec, slot_spec, rows_spec, rows_spec, rows_spec],
        out_shape=[jax.ShapeDtypeStruct((n, N_Q_HEADS, t, HEAD_DIM), BF16), slot_sds, slot_sds,
                   rows_sds, rows_sds, rows_sds],
        compiler_params=_cparams("arbitrary", "arbitrary"), name="postproj_prompt",
    )(z, z, z, z, q_norm_g.reshape(1, HEAD_DIM), k_norm_g)


def _conv_finish(b, y, g):
    c = b * y
    inv = lax.rsqrt(jnp.mean(c * c, axis=-1, keepdims=True) + RMS_EPS)
    return ((c * inv) * g).astype(BF16)


def _conv_prompt_body(gb_ref, gc_ref, hc_ref, w_ref, g_ref, o_ref, st_ref, carry_ref):
    tt = gb_ref.shape[0]

    @pl.when(pl.program_id(1) == 0)
    def _():
        carry_ref[...] = jnp.zeros_like(carry_ref)

    u = gc_ref[...] * hc_ref[...]
    prev = carry_ref[...]
    p1, p2 = prev[7:8, :], prev[6:7, :]
    row = lax.broadcasted_iota(jnp.int32, u.shape, 0)
    u1 = jnp.where(row == 0, p1, pltpu.roll(u, 1, axis=0))
    u2 = jnp.where(row == 0, p2, jnp.where(row == 1, p1, pltpu.roll(u, 2, axis=0)))
    y = u2 * w_ref[0:1, :] + u1 * w_ref[1:2, :] + u * w_ref[2:3, :]
    o_ref[...] = _conv_finish(gb_ref[...], y, g_ref[...])
    last = u[tt - 8:tt, :]
    carry_ref[...] = last
    st_ref[0] = last


def conv_prompt(z, n, t, conv_w, gain, tt=256):
    nt = t // tt
    row = lambda b, i: (b * nt + i, 0)
    return pl.pallas_call(
        _conv_prompt_body, grid=(n, nt),
        in_specs=[pl.BlockSpec((tt, CONV_DIM), lambda b, i: (b * nt + i, 0)),
                  pl.BlockSpec((tt, CONV_DIM), lambda b, i: (b * nt + i, 1)),
                  pl.BlockSpec((tt, CONV_DIM), lambda b, i: (b * nt + i, 2)),
                  pl.BlockSpec((CONV_WIDTH, CONV_DIM), lambda b, i: (0, 0)),
                  pl.BlockSpec((1, CONV_DIM), lambda b, i: (0, 0))],
        out_specs=[pl.BlockSpec((tt, CONV_DIM), row),
                   pl.BlockSpec((1, 8, CONV_DIM), lambda b, i: (b, 0, 0))],
        out_shape=[jax.ShapeDtypeStruct((n * t, CONV_DIM), BF16), jax.ShapeDtypeStruct((n, 8, CONV_DIM), F32)],
        scratch_shapes=[pltpu.VMEM((8, CONV_DIM), F32)],
        compiler_params=_cparams("arbitrary", "arbitrary"), name="conv_prompt",
    )(z, z, z, conv_w, gain.reshape(1, CONV_DIM))


def _conv_sample_body(z_ref, pre_ref, w_ref, g_ref, o_ref, st_ref, *, t):
    up = [pre_ref[k] for k in range(CONV_WIDTH - 1)] + [z_ref[1, k] * z_ref[2, k] for k in range(t)]
    for k in range(t):
        y = up[k] * w_ref[0:1, :] + up[k + 1] * w_ref[1:2, :] + up[k + 2] * w_ref[2:3, :]
        o_ref[k] = _conv_finish(z_ref[0, k], y, g_ref[...])
    for k in range(CONV_WIDTH - 1):
        st_ref[k] = up[t + k]


def conv_sample(z, n, t, state, conv_w, gain):
    zt = z[:, :3 * CONV_DIM].reshape(n, t, 3, CONV_DIM).transpose(2, 1, 0, 3)
    whole = lambda shape: pl.BlockSpec(shape, lambda i: (0,) * len(shape))
    out, st = pl.pallas_call(
        functools.partial(_conv_sample_body, t=t), grid=(1,),
        in_specs=[whole((3, t, n, CONV_DIM)), whole((CONV_WIDTH - 1, n, CONV_DIM)),
                  whole((CONV_WIDTH, CONV_DIM)), whole((1, CONV_DIM))],
        out_specs=[whole((t, n, CONV_DIM)), whole((CONV_WIDTH - 1, n, CONV_DIM))],
        out_shape=[jax.ShapeDtypeStruct((t, n, CONV_DIM), BF16),
                   jax.ShapeDtypeStruct((CONV_WIDTH - 1, n, CONV_DIM), F32)],
        compiler_params=_cparams("arbitrary"), name="conv_sample",
    )(zt, state.transpose(1, 0, 2), conv_w, gain.reshape(1, CONV_DIM))
    return out.transpose(1, 0, 2).reshape(n * t, CONV_DIM), st.transpose(1, 0, 2)


def _compress_body(pt_ref, *refs):
    del pt_ref
    pages = refs[:N_PAGES]
    wkv_ref, pe_ref, w2_ref, kg_ref, out_ref = refs[N_PAGES:]
    cpp = PAGE_SIZE // CMP_STRIDE
    x4 = [pages[p][...].reshape(cpp, CMP_STRIDE, N_SLOT, HEAD_DIM) for p in range(N_PAGES)]
    n_rows = N_PAGES * cpp * N_SLOT
    is_key = (lax.broadcasted_iota(jnp.int32, (n_rows, HEAD_DIM), 0) & (N_SLOT - 1)) < N_KV_HEADS

    def split(x):
        return [jnp.where(is_key, x, 0.0).astype(BF16), jnp.where(is_key, 0.0, x).astype(BF16)]

    pieces = []
    for s in range(CMP_STRIDE):
        pieces += split(jnp.concatenate([x4[p][:, s].reshape(cpp * N_SLOT, HEAD_DIM) for p in range(N_PAGES)], axis=0))
    wkv = wkv_ref[...].astype(BF16)
    r = _dot(jnp.concatenate(pieces, axis=1), wkv)
    hpre = r[:, :HEAD_DIM] + pltpu.roll(r[:, HEAD_DIM:], n_rows - N_SLOT, axis=0)
    bias = []
    for v in range(2):
        b = jnp.zeros((8, HEAD_DIM), F32)
        for rr in range(R_CMP):
            pe = jnp.broadcast_to(pe_ref[v, rr:rr + 1, :], (8, wkv.shape[0])).astype(BF16)
            b = b + _dot(pe, wkv[:, rr * HEAD_DIM:(rr + 1) * HEAD_DIM])
        bias.append(b[0:1, :])
    hid = jax.nn.gelu(hpre + jnp.where(is_key, bias[0], bias[1]))
    out = _dot(jnp.concatenate(split(hid), axis=1), w2_ref[...].astype(BF16))
    out_ref[0] = jnp.where(is_key, _head_norm(out, kg_ref[0:1, :]), out)


def compress(pool, page_table, wkv, pe_kv, w2cat, k_norm_g):
    n = page_table.shape[0]
    page_rows = PAGE_SIZE * N_SLOT
    page_spec = lambda p: pl.BlockSpec((page_rows, HEAD_DIM), lambda i, pt, p=p: (pt[i, p], 0))
    const = lambda shape: pl.BlockSpec(shape, lambda i, pt: (0,) * len(shape))
    return pl.pallas_call(
        _compress_body,
        grid_spec=pltpu.PrefetchScalarGridSpec(
            num_scalar_prefetch=1, grid=(n,),
            in_specs=[page_spec(p) for p in range(N_PAGES)]
            + [const(wkv.shape), const(pe_kv.shape), const(w2cat.shape), const(k_norm_g.shape)],
            out_specs=pl.BlockSpec((1, N_CMP_BLK * N_SLOT, HEAD_DIM), lambda i, pt: (i, 0, 0))),
        out_shape=jax.ShapeDtypeStruct((n, N_CMP_BLK * N_SLOT, HEAD_DIM), F32),
        compiler_params=_cparams("arbitrary"), name="compress",
    )(page_table, *([pool] * N_PAGES), wkv, pe_kv, w2cat, k_norm_g)


def _cmp_select_body(q_ref, kvc_ref, cov_ref, o_ref, sel_ref, *, bn, tq, pos_base):
    qt = pl.program_id(1)
    grp = GQA * tq
    pairs = [(b, j) for b in range(bn) for j in range(N_KV_HEADS)]

    def slot(b, c):
        return kvc_ref[b, pl.ds(c, N_CMP_BLK, stride=N_SLOT), :].astype(BF16)

    s = jnp.concatenate(
        [_dot_nt(q_ref[b, j * GQA:(j + 1) * GQA].reshape(grp, HEAD_DIM).astype(BF16), slot(b, j)) for b, j in pairs],
        axis=0) * SCALE
    rows = len(pairs) * grp
    pos = pos_base + qt * tq + (lax.broadcasted_iota(jnp.int32, (rows, N_CMP_BLK), 0) & (tq - 1))
    blk = lax.broadcasted_iota(jnp.int32, (rows, N_CMP_BLK), 1)
    valid = (blk < N_CMP_BLK - 1) & (blk * CMP_STRIDE + (L_CMP - 1) <= pos)
    p = _masked_softmax(s, valid)
    p_bf = p.astype(BF16)
    for i, (b, j) in enumerate(pairs):
        o_ref[b, j * GQA:(j + 1) * GQA] = _dot(p_bf[i * grp:(i + 1) * grp],
                                               slot(b, N_KV_HEADS + j)).reshape(GQA, tq, HEAD_DIM)
    psum = jnp.sum(p.reshape(len(pairs), GQA, tq, N_CMP_BLK), axis=1).reshape(len(pairs) * tq, N_CMP_BLK)
    p_hi = psum.astype(BF16)
    p_lo = (psum - p_hi.astype(F32)).astype(BF16)
    imp = _dot(p_hi, cov_ref[...]) + _dot(p_lo, cov_ref[...])
    srows = len(pairs) * tq
    blk = lax.broadcasted_iota(jnp.int32, (srows, LANE), 1)
    cur = (pos_base + qt * tq + (lax.broadcasted_iota(jnp.int32, (srows, LANE), 0) & (tq - 1))) >> 6
    forced = (blk == 0) | (blk == cur) | (blk == cur - 1)
    score = jnp.where(blk <= cur, imp + jnp.where(forced, FORCE_BONUS, 0.0), NEG_INF)
    sel = jnp.zeros((srows, LANE), F32)
    for _ in range(N_SEL):
        m = jnp.max(score, axis=-1, keepdims=True)
        first = jnp.min(jnp.where(score == m, blk, LANE), axis=-1, keepdims=True)
        hit = blk == first
        sel = jnp.where(hit, 1.0, sel)
        score = jnp.where(hit, PICKED, score)
    for i, (b, j) in enumerate(pairs):
        sel_ref[b, j] = sel[i * tq:(i + 1) * tq]


def cmp_select(q, kvc, cover, bn, tq, pos_base):
    n, _, t, _ = q.shape
    assert L_SEL == 64 and tq & (tq - 1) == 0
    return pl.pallas_call(
        functools.partial(_cmp_select_body, bn=bn, tq=tq, pos_base=pos_base), grid=(n // bn, t // tq),
        in_specs=[pl.BlockSpec((bn, N_Q_HEADS, tq, HEAD_DIM), lambda b, i: (b, 0, i, 0)),
                  pl.BlockSpec((bn, N_CMP_BLK * N_SLOT, HEAD_DIM), lambda b, i: (b, 0, 0)),
                  pl.BlockSpec((N_CMP_BLK, LANE), lambda b, i: (0, 0))],
        out_specs=[pl.BlockSpec((bn, N_Q_HEADS, tq, HEAD_DIM), lambda b, i: (b, 0, i, 0)),
                   pl.BlockSpec((bn, N_KV_HEADS, tq, LANE), lambda b, i: (b, 0, i, 0))],
        out_shape=[jax.ShapeDtypeStruct((n, N_Q_HEADS, t, HEAD_DIM), F32),
                   jax.ShapeDtypeStruct((n, N_KV_HEADS, t, LANE), F32)],
        compiler_params=_cparams("arbitrary", "arbitrary"), name="cmp_select",
    )(q, kvc, cover)


SOFTMAX_ROWS = 16


def _attn_prompt_body(*refs, tq, t, selected):
    if selected:
        q_ref, k_ref, v_ref, sel_ref, e_ref, o_ref, s_ref, bias_ref, p_ref = refs
    else:
        q_ref, k_ref, v_ref, o_ref, s_ref, bias_ref, p_ref = refs
    qt = pl.program_id(2)
    rows = GQA * tq
    q = q_ref[0].reshape(rows, HEAD_DIM)

    def attend(start, nk):
        k = k_ref[0, 0, pl.ds(start, nk), :]
        v = v_ref[0, 0, pl.ds(start, nk), :]
        s_ref[:, 0:nk] = _dot_nt(q, k)
        qpos = qt * tq + lax.broadcasted_iota(jnp.int32, (tq, nk), 0)
        kpos = start + lax.broadcasted_iota(jnp.int32, (tq, nk), 1)
        ok = kpos <= qpos
        if selected:
            ok = ok & (_dot(sel_ref[0, 0].astype(BF16), e_ref[:, 0:nk]) > 0.5)
        else:
            ok = ok & (kpos > qpos - WINDOW)
        bias_ref[:, 0:nk] = jnp.where(ok, 0.0, NEG_INF)

        def chunk(c, carry):
            r0 = pl.multiple_of(c * SOFTMAX_ROWS, SOFTMAX_ROWS)
            b0 = pl.multiple_of(lax.rem(c * SOFTMAX_ROWS, tq), SOFTMAX_ROWS)
            sb = s_ref[pl.ds(r0, SOFTMAX_ROWS), 0:nk] * SCALE + bias_ref[pl.ds(b0, SOFTMAX_ROWS), 0:nk]
            m = jnp.max(sb, axis=-1, keepdims=True)
            e = jnp.exp(sb - m)
            inv = jnp.where(m > 0.5 * NEG_INF, 1.0 / jnp.maximum(jnp.sum(e, axis=-1, keepdims=True), TINY), 0.0)
            p_ref[pl.ds(r0, SOFTMAX_ROWS), 0:nk] = (e * inv).astype(BF16)
            return carry

        lax.fori_loop(0, rows // SOFTMAX_ROWS, chunk, 0)
        o_ref[0] = _dot(p_ref[:, 0:nk], v).reshape(GQA, tq, HEAD_DIM)

    if selected:
        n_bucket = t // WINDOW
        per = WINDOW // tq
        for b in range(n_bucket):
            @pl.when(qt // per == b)
            def _(b=b):
                attend(0, (b + 1) * WINDOW)
    else:
        attend(pl.multiple_of(jnp.maximum(qt * tq - WINDOW, 0), tq), WINDOW + tq)


def attn_prompt(q, kv, sel, expand, tq=128):
    n, _, t, _ = q.shape
    selected = sel is not None
    in_specs = [pl.BlockSpec((1, GQA, tq, HEAD_DIM), lambda b, j, i: (b, j, i, 0)),
                pl.BlockSpec((1, 1, t, HEAD_DIM), lambda b, j, i: (b, j, 0, 0)),
                pl.BlockSpec((1, 1, t, HEAD_DIM), lambda b, j, i: (b, N_KV_HEADS + j, 0, 0))]
    args = [q, kv, kv]
    if selected:
        in_specs += [pl.BlockSpec((1, 1, tq, LANE), lambda b, j, i: (b, j, i, 0)),
                     pl.BlockSpec(expand.shape, lambda b, j, i: (0, 0))]
        args += [sel, expand]
    nk_max = t if selected else WINDOW + tq
    return pl.pallas_call(
        functools.partial(_attn_prompt_body, tq=tq, t=t, selected=selected), grid=(n, N_KV_HEADS, t // tq),
        in_specs=in_specs,
        out_specs=pl.BlockSpec((1, GQA, tq, HEAD_DIM), lambda b, j, i: (b, j, i, 0)),
        out_shape=jax.ShapeDtypeStruct((n, N_Q_HEADS, t, HEAD_DIM), F32),
        scratch_shapes=[pltpu.VMEM((GQA * tq, nk_max), F32), pltpu.VMEM((tq, nk_max), F32),
                        pltpu.VMEM((GQA * tq, nk_max), BF16)],
        compiler_params=_cparams("arbitrary", "arbitrary", "arbitrary"),
        name="attn_prompt_sel" if selected else "attn_prompt_win",
    )(*args)


T_PAD = 8


def _pad_keys(x):
    return jnp.concatenate([x, jnp.zeros((PAGE_SIZE - x.shape[0], x.shape[1]), x.dtype)], axis=0)


def _attn_sample_sel_body(pt_ref, *refs, t_real):
    del pt_ref
    pages = refs[:N_PAGES]
    q_ref, sel_ref, new_ref, e_ref, o_ref = refs[N_PAGES:]
    rows = GQA * T_PAD
    tok = lax.broadcasted_iota(jnp.int32, (rows, PAGE_SIZE), 0) & (T_PAD - 1)
    col = lax.broadcasted_iota(jnp.int32, (rows, PAGE_SIZE), 1)
    for j in range(N_KV_HEADS):
        kc0, vc0 = j * HEAD_DIM, KV_DIM + j * HEAD_DIM
        q = q_ref[0, j * GQA:(j + 1) * GQA].reshape(rows, HEAD_DIM).astype(BF16)
        k_rows = pl.ds(j, PAGE_SIZE, stride=N_SLOT)
        v_rows = pl.ds(N_KV_HEADS + j, PAGE_SIZE, stride=N_SLOT)
        parts = [_dot_nt(q, pages[p][k_rows, :].astype(BF16)) for p in range(N_PAGES)]
        parts.append(_dot_nt(q, _pad_keys(new_ref[0, :, kc0:kc0 + HEAD_DIM]).astype(BF16)))
        s = jnp.concatenate(parts, axis=1) * SCALE
        sel = sel_ref[0, j]
        sel4 = jnp.concatenate([sel] * GQA, axis=0)
        picked = _dot(sel4.astype(BF16), e_ref[...])
        new_blk = PAST_LEN // L_SEL
        new_ok = (col < t_real) & (col <= tok)
        new_picked = jnp.where(new_ok, _lane_pick(sel4, new_blk), 0.0)
        mask = jnp.concatenate([picked, new_picked], axis=1) > 0.5
        p = _masked_softmax(s, mask).astype(BF16)
        o = _dot(p[:, PAST_LEN:], _pad_keys(new_ref[0, :, vc0:vc0 + HEAD_DIM]).astype(BF16))
        for pg in range(N_PAGES):
            o = o + _dot(p[:, pg * PAGE_SIZE:(pg + 1) * PAGE_SIZE], pages[pg][v_rows, :].astype(BF16))
        o_ref[0, j * GQA:(j + 1) * GQA] = o.reshape(GQA, T_PAD, HEAD_DIM)


def attn_sample_sel(pool, page_table, q, sel, new_rows, expand, t_real):
    n = page_table.shape[0]
    cols = new_rows.shape[-1]
    page_spec = lambda p: pl.BlockSpec((PAGE_SIZE * N_SLOT, HEAD_DIM), lambda i, pt, p=p: (pt[i, p], 0))
    return pl.pallas_call(
        functools.partial(_attn_sample_sel_body, t_real=t_real),
        grid_spec=pltpu.PrefetchScalarGridSpec(
            num_scalar_prefetch=1, grid=(n,),
            in_specs=[page_spec(p) for p in range(N_PAGES)]
            + [pl.BlockSpec((1, N_Q_HEADS, T_PAD, HEAD_DIM), lambda i, pt: (i, 0, 0, 0)),
               pl.BlockSpec((1, N_KV_HEADS, T_PAD, LANE), lambda i, pt: (i, 0, 0, 0)),
               pl.BlockSpec((1, T_PAD, cols), lambda i, pt: (i, 0, 0)),
               pl.BlockSpec(expand.shape, lambda i, pt: (0, 0))],
            out_specs=pl.BlockSpec((1, N_Q_HEADS, T_PAD, HEAD_DIM), lambda i, pt: (i, 0, 0, 0))),
        out_shape=jax.ShapeDtypeStruct((n, N_Q_HEADS, T_PAD, HEAD_DIM), F32),
        compiler_params=_cparams("arbitrary"), name="attn_sample_sel",
    )(page_table, *([pool] * N_PAGES), q, sel, new_rows, expand)


def _attn_sample_win_body(win_ref, q_ref, new_ref, newrows_ref, o_ref, wout_ref, *, t_real):
    rows = GQA * T_PAD
    w_buf = win_ref.shape[0] // N_SLOT
    tok_o = lax.broadcasted_iota(jnp.int32, (rows, w_buf), 0) & (T_PAD - 1)
    col_o = lax.broadcasted_iota(jnp.int32, (rows, w_buf), 1)
    tok_n = lax.broadcasted_iota(jnp.int32, (rows, PAGE_SIZE), 0) & (T_PAD - 1)
    col_n = lax.broadcasted_iota(jnp.int32, (rows, PAGE_SIZE), 1)
    old_ok = jnp.where(col_o + (WINDOW - w_buf) > tok_o, 1.0, 0.0)
    new_ok = jnp.where((col_n < t_real) & (col_n <= tok_n), 1.0, 0.0)
    mask = jnp.concatenate([old_ok, new_ok], axis=1) > 0.5
    for j in range(N_KV_HEADS):
        kc0, vc0 = j * HEAD_DIM, KV_DIM + j * HEAD_DIM
        q = q_ref[0, j * GQA:(j + 1) * GQA].reshape(rows, HEAD_DIM).astype(BF16)
        k_new = _pad_keys(new_ref[0, :, kc0:kc0 + HEAD_DIM]).astype(BF16)
        v_new = _pad_keys(new_ref[0, :, vc0:vc0 + HEAD_DIM]).astype(BF16)
        k_old = win_ref[pl.ds(j, w_buf, stride=N_SLOT), :].astype(BF16)
        v_old = win_ref[pl.ds(N_KV_HEADS + j, w_buf, stride=N_SLOT), :].astype(BF16)
        s = jnp.concatenate([_dot_nt(q, k_old), _dot_nt(q, k_new)], axis=1) * SCALE
        p = _masked_softmax(s, mask).astype(BF16)
        o = _dot(p[:, :w_buf], v_old) + _dot(p[:, w_buf:], v_new)
        o_ref[0, j * GQA:(j + 1) * GQA] = o.reshape(GQA, T_PAD, HEAD_DIM)
    keep = (w_buf - t_real) * N_SLOT
    wout_ref[0:keep, :] = win_ref[t_real * N_SLOT:, :]
    wout_ref[keep:, :] = newrows_ref[...]


def attn_sample_win(win, q, new_rows, new_cache_rows, t_real):
    n = q.shape[0]
    cols = new_rows.shape[-1]
    buf_rows = win.shape[0] // n
    return pl.pallas_call(
        functools.partial(_attn_sample_win_body, t_real=t_real), grid=(n,),
        in_specs=[pl.BlockSpec((buf_rows, HEAD_DIM), lambda i: (i, 0)),
                  pl.BlockSpec((1, N_Q_HEADS, T_PAD, HEAD_DIM), lambda i: (i, 0, 0, 0)),
                  pl.BlockSpec((1, T_PAD, cols), lambda i: (i, 0, 0)),
                  pl.BlockSpec((t_real * N_SLOT, HEAD_DIM), lambda i: (i, 0))],
        out_specs=[pl.BlockSpec((1, N_Q_HEADS, T_PAD, HEAD_DIM), lambda i: (i, 0, 0, 0)),
                   pl.BlockSpec((buf_rows, HEAD_DIM), lambda i: (i, 0))],
        out_shape=[jax.ShapeDtypeStruct((n, N_Q_HEADS, T_PAD, HEAD_DIM), F32),
                   jax.ShapeDtypeStruct(win.shape, F32)],
        compiler_params=_cparams("arbitrary"), name="attn_sample_win",
    )(win, q, new_rows, new_cache_rows)


def _combine_body(oc_ref, os_ref, ow_ref, gl_ref, g_ref, out_ref, *, rows):
    gates = jax.nn.sigmoid(gl_ref[...])
    outs = []
    sq = jnp.zeros((rows, 1), F32)
    for h in range(N_Q_HEADS):
        a = (_lane_pick(gates, h) * oc_ref[:, h].reshape(rows, HEAD_DIM)
             + _lane_pick(gates, N_Q_HEADS + h) * os_ref[:, h].reshape(rows, HEAD_DIM)
             + _lane_pick(gates, 2 * N_Q_HEADS + h) * ow_ref[:, h].reshape(rows, HEAD_DIM))
        outs.append(a)
        sq = sq + jnp.sum(a * a, axis=-1, keepdims=True)
    inv = lax.rsqrt(sq * (1.0 / ATTN_DIM) + RMS_EPS)
    for h in range(N_Q_HEADS):
        sl = slice(h * HEAD_DIM, (h + 1) * HEAD_DIM)
        out_ref[:, sl] = ((outs[h] * inv) * g_ref[:, sl]).astype(out_ref.dtype)


def combine(o_cmp, o_sel, o_win, gate_logits, gain, bn, tt):
    n, _, t, _ = o_cmp.shape
    nt = t // tt
    rows = bn * tt
    o_spec = pl.BlockSpec((bn, N_Q_HEADS, tt, HEAD_DIM), lambda b, i: (b, 0, i, 0))
    return pl.pallas_call(
        functools.partial(_combine_body, rows=rows), grid=(n // bn, nt),
        in_specs=[o_spec, o_spec, o_spec,
                  pl.BlockSpec((rows, LANE), lambda b, i: (b * nt + i, 0)),
                  pl.BlockSpec((1, ATTN_DIM), lambda b, i: (0, 0))],
        out_specs=pl.BlockSpec((rows, ATTN_DIM), lambda b, i: (b * nt + i, 0)),
        out_shape=jax.ShapeDtypeStruct((n * t, ATTN_DIM), BF16),
        compiler_params=_cparams("arbitrary", "arbitrary"), name="combine",
    )(o_cmp, o_sel, o_win, gate_logits, gain.reshape(1, ATTN_DIM))


def _router_body(x_ref, w_ref, b_ref, ids_ref, wts_ref):
    logits = _dot(x_ref[...].astype(BF16), w_ref[...].astype(BF16)) + b_ref[...]
    lane = lax.broadcasted_iota(jnp.int32, logits.shape, 1)
    is_grp = (lane >= N_EXPERTS) & (lane < N_EXPERTS + N_GROUPS)
    gl = jnp.where(is_grp, logits, NEG_INF)
    ge = jnp.where(is_grp, jnp.exp(gl - jnp.max(gl, axis=-1, keepdims=True)), 0.0)
    p_grp = ge / jnp.sum(ge, axis=-1, keepdims=True)
    g_val = jnp.max(p_grp, axis=-1, keepdims=True)
    g_idx = jnp.min(jnp.where(is_grp & (p_grp == g_val), lane, 2 * LANE), axis=-1, keepdims=True) - N_EXPERTS
    lo = g_idx * EXPERTS_PER_GROUP
    in_grp = (lane >= lo) & (lane < lo + EXPERTS_PER_GROUP)
    el = jnp.where(in_grp, logits, NEG_INF)
    ee = jnp.where(in_grp, jnp.exp(el - jnp.max(el, axis=-1, keepdims=True)), 0.0)
    p_e = ee / jnp.sum(ee, axis=-1, keepdims=True)
    cand = jnp.where(in_grp, p_e, -1.0)
    e1 = jnp.max(cand, axis=-1, keepdims=True)
    i1 = jnp.min(jnp.where(cand == e1, lane, 2 * LANE), axis=-1, keepdims=True)
    cand = jnp.where(lane == i1, -1.0, cand)
    e2 = jnp.max(cand, axis=-1, keepdims=True)
    i2 = jnp.min(jnp.where(cand == e2, lane, 2 * LANE), axis=-1, keepdims=True)
    tot = e1 + e2
    ids_ref[...] = jnp.where(lane == 0, i1, jnp.where(lane == 1, i2, 0))
    wts_ref[...] = jnp.where(lane == 0, g_val * e1 / tot, jnp.where(lane == 1, g_val * e2 / tot, 0.0))


def router(xt, w_router, b_router, tm=256):
    m, d = xt.shape
    out_spec = pl.BlockSpec((tm, LANE), lambda i: (i, 0))
    return pl.pallas_call(
        _router_body, grid=(m // tm,),
        in_specs=[pl.BlockSpec((tm, d), lambda i: (i, 0)),
                  pl.BlockSpec((d, LANE), lambda i: (0, 0)),
                  pl.BlockSpec((1, LANE), lambda i: (0, 0))],
        out_specs=[out_spec, out_spec],
        out_shape=[jax.ShapeDtypeStruct((m, LANE), jnp.int32), jax.ShapeDtypeStruct((m, LANE), F32)],
        compiler_params=_cparams("arbitrary"), name="router",
    )(xt, w_router, b_router)


def _gather_rows(idx_ref, n_rows, src_hbm, dst, sem, wait):
    def body(r, c):
        src = 0 if wait else idx_ref[0, 0, r]
        cp = pltpu.make_async_copy(src_hbm.at[pl.ds(src, 1), :], dst.at[pl.ds(r, 1), :], sem)
        cp.wait() if wait else cp.start()
        return c
    lax.fori_loop(0, n_rows, body, 0, unroll=8)


def _moe_ffn_body(te_ref, cur_ref, nxt_ref, x_hbm, rw_ref, wg_ref, wu_ref, wd_ref, ys_ref, xbuf, sem):
    del te_ref
    i = pl.program_id(0)
    n = pl.num_programs(0)
    tm = xbuf.shape[1]
    slot = lax.rem(i, 2)

    @pl.when(i == 0)
    def _():
        _gather_rows(cur_ref, tm, x_hbm, xbuf.at[0], sem.at[0], wait=False)

    @pl.when(i + 1 < n)
    def _():
        _gather_rows(nxt_ref, tm, x_hbm, xbuf.at[1 - slot], sem.at[1 - slot], wait=False)

    _gather_rows(cur_ref, tm, x_hbm, xbuf.at[slot], sem.at[slot], wait=True)
    x = xbuf[slot].astype(BF16)
    hid = jax.nn.silu(_dot(x, wg_ref[0])) * _dot(x, wu_ref[0])
    gate = jnp.concatenate([rw_ref[...]] * (hid.shape[1] // LANE), axis=1)
    ys_ref[...] = _dot((hid * gate).astype(BF16), wd_ref[0])


def moe_ffn(xt, tile_expert, row_token, row_weight, w_gate, w_up, w_down, tm):
    n_tiles = tile_expert.shape[0]
    d = xt.shape[1]
    f = w_gate.shape[-1]
    smem_rows = lambda fn: pl.BlockSpec((1, 1, tm), fn, memory_space=pltpu.SMEM)
    return pl.pallas_call(
        _moe_ffn_body,
        grid_spec=pltpu.PrefetchScalarGridSpec(
            num_scalar_prefetch=1, grid=(n_tiles,),
            in_specs=[smem_rows(lambda i, te: (i, 0, 0)),
                      smem_rows(lambda i, te: (jnp.minimum(i + 1, n_tiles - 1), 0, 0)),
                      pl.BlockSpec(memory_space=pl.ANY),
                      pl.BlockSpec((tm, LANE), lambda i, te: (i, 0)),
                      pl.BlockSpec((1, d, f), lambda i, te: (te[i], 0, 0)),
                      pl.BlockSpec((1, d, f), lambda i, te: (te[i], 0, 0)),
                      pl.BlockSpec((1, f, d), lambda i, te: (te[i], 0, 0))],
            out_specs=pl.BlockSpec((tm, d), lambda i, te: (i, 0)),
            scratch_shapes=[pltpu.VMEM((2, tm, d), F32), pltpu.SemaphoreType.DMA((2,))]),
        out_shape=jax.ShapeDtypeStruct((n_tiles * tm, d), F32),
        compiler_params=_cparams("arbitrary"), name="moe_ffn",
    )(tile_expert, row_token, row_token, xt, row_weight, w_gate, w_up, w_down)


def _moe_combine_body(cur_ref, nxt_ref, ys_hbm, h_ref, ya_ref, yb_ref, buf, sem, *, tiles_a):
    i = pl.program_id(0)
    n = pl.num_programs(0)
    rows = buf.shape[1]
    tm = h_ref.shape[0]
    slot = lax.rem(i, 2)

    @pl.when(i == 0)
    def _():
        _gather_rows(cur_ref, rows, ys_hbm, buf.at[0], sem.at[0], wait=False)

    @pl.when(i + 1 < n)
    def _():
        _gather_rows(nxt_ref, rows, ys_hbm, buf.at[1 - slot], sem.at[1 - slot], wait=False)

    _gather_rows(cur_ref, rows, ys_hbm, buf.at[slot], sem.at[slot], wait=True)
    y = h_ref[...] + buf[slot, 0:tm, :] + buf[slot, tm:rows, :]

    @pl.when(i < tiles_a)
    def _():
        ya_ref[...] = y

    @pl.when(i >= tiles_a)
    def _():
        yb_ref[...] = y


def moe_combine(ys, pair_row, h, rows_a, tm=128):
    m, d = h.shape
    n_tiles = m // tm
    tiles_a = rows_a // tm
    assert rows_a % tm == 0 and 0 < tiles_a < n_tiles
    smem_rows = lambda fn: pl.BlockSpec((1, 1, 2 * tm), fn, memory_space=pltpu.SMEM)
    return pl.pallas_call(
        functools.partial(_moe_combine_body, tiles_a=tiles_a), grid=(n_tiles,),
        in_specs=[smem_rows(lambda i: (i, 0, 0)),
                  smem_rows(lambda i: (jnp.minimum(i + 1, n_tiles - 1), 0, 0)),
                  pl.BlockSpec(memory_space=pl.ANY),
                  pl.BlockSpec((tm, d), lambda i: (i, 0))],
        out_specs=[pl.BlockSpec((tm, d), lambda i: (jnp.minimum(i, tiles_a - 1), 0)),
                   pl.BlockSpec((tm, d), lambda i: (jnp.maximum(i - tiles_a, 0), 0))],
        out_shape=[jax.ShapeDtypeStruct((rows_a, d), F32), jax.ShapeDtypeStruct((m - rows_a, d), F32)],
        scratch_shapes=[pltpu.VMEM((2, 2 * tm, d), F32), pltpu.SemaphoreType.DMA((2,))],
        compiler_params=_cparams("arbitrary"), name="moe_combine",
    )(pair_row, pair_row, ys, h)


def moe_routed(xt, ids, wts, h, w_gate, w_up, w_down, tm, rows_a):
    m = xt.shape[0]
    n_pair = 2 * m
    n_tiles = (n_pair + N_EXPERTS * (tm - 1)) // tm + 1
    flat_e = ids[:, :2].reshape(n_pair)
    flat_w = wts[:, :2].reshape(n_pair)
    order = jnp.argsort(flat_e, stable=True).astype(jnp.int32)
    sorted_e = flat_e[order]
    counts = jnp.sum(flat_e[:, None] == jnp.arange(N_EXPERTS, dtype=jnp.int32)[None, :], axis=0, dtype=jnp.int32)
    padded = ((counts + tm - 1) // tm) * tm
    pad_end = jnp.cumsum(padded)
    start = jnp.cumsum(counts) - counts
    dest = (pad_end - padded)[sorted_e] + jnp.arange(n_pair, dtype=jnp.int32) - start[sorted_e]
    row_token = jnp.zeros((n_tiles * tm,), jnp.int32).at[dest].set(order // 2)
    row_weight = jnp.zeros((n_tiles * tm,), F32).at[dest].set(flat_w[order])
    pair_row = jnp.zeros((n_pair,), jnp.int32).at[order].set(dest).reshape(m, 2)
    tile_start = jnp.arange(n_tiles, dtype=jnp.int32) * tm
    tile_expert = jnp.minimum(jnp.sum(tile_start[:, None] >= pad_end[None, :], axis=1), N_EXPERTS - 1).astype(jnp.int32)
    ys = moe_ffn(xt, tile_expert, row_token.reshape(n_tiles, 1, tm),
                 jnp.broadcast_to(row_weight[:, None], (n_tiles * tm, LANE)), w_gate, w_up, w_down, tm)
    tc = 128
    pair_tiles = pair_row.reshape(m // tc, tc, 2).transpose(0, 2, 1).reshape(m // tc, 1, 2 * tc)
    return moe_combine(ys, pair_tiles, h, rows_a, tm=tc)


def _cover_matrix(nsb):
    i = np.arange(N_CMP_BLK)[:, None]
    j = np.arange(LANE)[None, :]
    m = np.zeros((N_CMP_BLK, LANE), np.float32)
    for a in range(L_SEL // CMP_STRIDE):
        for c in range(R_CMP):
            m += (i == (L_SEL // CMP_STRIDE) * j + a - c)
    m[N_CMP_BLK - 1:, :] = 0.0
    m[:, nsb:] = 0.0
    return jnp.asarray(m, BF16)


def _expand_matrix(n_keys):
    b = np.arange(LANE)[:, None]
    k = np.arange(n_keys)[None, :]
    return jnp.asarray((k // L_SEL == b).astype(np.float32), BF16)


def _to_heads(q, n, t):
    return q.reshape(n, t, N_Q_HEADS, HEAD_DIM).transpose(0, 2, 1, 3)


def kernel(x_prompt, x_sample, cache_cmp_kv, cache_sel_kv, state_win_kv, state_conv, page_table, norm_mix_g, w_in,
           conv_w, q_norm_g, k_norm_g, phi_pe, phi_w1, phi_w2, out_norm_g, w_out, norm_ffn_g, w_group_router,
           b_group_router, w_expert_router, b_expert_router, w_gate, w_up, w_down):
    n_p, t_p, d = x_prompt.shape
    n_s, t_s, _ = x_sample.shape
    assert w_in.shape[0] == 1 and t_s < CMP_STRIDE and t_s <= T_PAD and t_p % WINDOW == 0
    kv_cols = 2 * KV_DIM
    w_in_main = w_in[0]
    w_in_gate = jnp.pad(w_in[0][:, Z_MAIN:], ((0, 0), (0, LANE - 3 * N_Q_HEADS)))
    w_o = w_out[0]
    wkv = phi_w1[0].reshape(2, R_CMP, CMP_STRIDE, HEAD_DIM, HEAD_DIM).transpose(2, 0, 3, 1, 4)
    wkv = wkv.reshape(CMP_STRIDE * 2 * HEAD_DIM, R_CMP * HEAD_DIM)
    pe5 = phi_pe[0].reshape(2, R_CMP, CMP_STRIDE, 1, HEAD_DIM)
    pe_kv = (pe5 * jnp.eye(2, dtype=F32).reshape(2, 1, 1, 2, 1)).reshape(2, R_CMP, CMP_STRIDE * 2 * HEAD_DIM)
    w2cat = phi_w2[0].reshape(2 * HEAD_DIM, HEAD_DIM)
    w_router = jnp.pad(jnp.concatenate([w_expert_router[0], w_group_router[0]], axis=1),
                       ((0, 0), (0, LANE - N_EXPERTS - N_GROUPS)))
    b_router = jnp.pad(jnp.concatenate([b_expert_router[0], b_group_router[0]]),
                       (0, LANE - N_EXPERTS - N_GROUPS)).reshape(1, LANE)
    wg_bf, wu_bf, wd_bf = w_gate[0].astype(BF16), w_up[0].astype(BF16), w_down[0].astype(BF16)

    def project(x2d):
        xn = rmsnorm_cast(x2d, norm_mix_g[0])
        z = matmul([([xn], None)], [(w_in_main, 0)], tm=1024, tn=512)
        gate_logits = matmul([([xn], None)], [(w_in_gate, 0)], tn=LANE)
        return z, gate_logits

    m_p = n_p * t_p

    xp = x_prompt.reshape(n_p * t_p, d)
    z, glog = project(xp)
    qh, ks_h, kw_h, kvc, kvs, kvw = postproj_prompt(z, n_p, t_p, q_norm_g[0], k_norm_g[0])
    conv_out, conv_last = conv_prompt(z, n_p, t_p, conv_w[0], out_norm_g[0][:CONV_DIM])
    ident = jnp.arange(n_p * (t_p // PAGE_SIZE), dtype=jnp.int32).reshape(n_p, t_p // PAGE_SIZE)
    kv_cmp = compress(kvc, ident, wkv, pe_kv, w2cat, k_norm_g[0])
    o_cmp, sel = cmp_select(qh, kv_cmp, _cover_matrix(t_p // L_SEL), bn=1, tq=256, pos_base=0)
    o_sel = attn_prompt(qh, ks_h, sel, _expand_matrix(t_p))
    o_win = attn_prompt(qh, kw_h, None, None)
    attn_out = combine(o_cmp, o_sel, o_win, glog, out_norm_g[0][CONV_DIM:], bn=1, tt=256)
    mixed_p = ([conv_out, attn_out], xp)
    kv_shape = (1, n_p, t_p, 2, N_KV_HEADS, HEAD_DIM)
    w_keep = min(WINDOW, t_p)
    prompt_win = kvw.reshape(kv_shape)[:, :, t_p - w_keep:]
    prompt_conv = conv_last[:, 8 - (CONV_WIDTH - 1):, :][None]

    xs = x_sample.reshape(n_s * t_s, d)
    z, glog = project(xs)
    q, kvc_s, kvs_s, kvw_s = postproj(z, q_norm_g[0], k_norm_g[0])
    conv_out, conv_state = conv_sample(z, n_s, t_s, state_conv[0], conv_w[0], out_norm_g[0][:CONV_DIM])
    pad_t = lambda a: jnp.pad(a, ((0, 0), (0, T_PAD - t_s), (0, 0)))
    qh = jnp.pad(_to_heads(q, n_s, t_s), ((0, 0), (0, 0), (0, T_PAD - t_s), (0, 0)))
    pool_cmp = cache_cmp_kv[0].reshape(-1, HEAD_DIM)
    pool_sel = cache_sel_kv[0].reshape(-1, HEAD_DIM)
    kv_cmp = compress(pool_cmp, page_table, wkv, pe_kv, w2cat, k_norm_g[0])
    o_cmp, sel = cmp_select(qh, kv_cmp, _cover_matrix(PAST_LEN // L_SEL + 1), bn=16, tq=T_PAD, pos_base=PAST_LEN)
    o_sel = attn_sample_sel(pool_sel, page_table, qh, sel, pad_t(kvs_s.reshape(n_s, t_s, kv_cols)),
                            _expand_matrix(PAST_LEN), t_s)
    win = state_win_kv[0].reshape(-1, HEAD_DIM)
    o_win, win_new = attn_sample_win(win, qh, pad_t(kvw_s.reshape(n_s, t_s, kv_cols)),
                                     kvw_s.reshape(-1, HEAD_DIM), t_s)
    glog_pad = pad_t(glog.reshape(n_s, t_s, LANE)).reshape(n_s * T_PAD, LANE)
    attn_out = combine(o_cmp, o_sel, o_win, glog_pad, out_norm_g[0][CONV_DIM:], bn=32, tt=T_PAD)
    attn_out = attn_out.reshape(n_s, T_PAD, ATTN_DIM)[:, :t_s].reshape(n_s * t_s, ATTN_DIM)
    s_shape = (1, n_s, t_s, 2, N_KV_HEADS, HEAD_DIM)

    h_all = matmul([mixed_p, ([conv_out, attn_out], xs)], [(w_o, 0), (w_o, 1)])
    xt = rmsnorm_cast(h_all, norm_ffn_g[0], dtype=F32)
    ids, wts = router(xt, w_router, b_router)
    y_prompt, y_sample = moe_routed(xt, ids, wts, h_all, wg_bf, wu_bf, wd_bf, 256, m_p)

    return (y_prompt.reshape(n_p, t_p, d), y_sample.reshape(n_s, t_s, d), kvc.reshape(kv_shape), kvs.reshape(kv_shape), prompt_win, prompt_conv,
            kvc_s.reshape(s_shape), kvs_s.reshape(s_shape),
            win_new.reshape(1, n_s, -1, 2, N_KV_HEADS, HEAD_DIM), conv_state[None])
```

```python
import functools

import numpy as np
import jax
import jax.numpy as jnp
from jax import lax
from jax.experimental import pallas as pl
from jax.experimental.pallas import tpu as pltpu

F32 = jnp.float32
BF16 = jnp.bfloat16

D_MODEL = 4096
PAST_LEN = 2048
PAGE_SIZE = 128
HEAD_DIM = 128
CONV_DIM = 2048
N_Q_HEADS = 16
N_KV_HEADS = 4
GQA = 4
ATTN_DIM = 2048
KV_DIM = 512
CONV_WIDTH = 3
L_CMP = 32
CMP_STRIDE = 16
R_CMP = 2
L_SEL = 64
N_SEL = 8
WINDOW = 512
FORCE_BONUS = 1e3
SCALE = HEAD_DIM ** -0.5
N_GROUPS = 4
EXPERTS_PER_GROUP = 4
N_EXPERTS = 16
D_FF_EXPERT = 512
RMS_EPS = 1e-6
NEG_INF = -1e30
TINY = 1e-30
PICKED = -3e38
LOG2_E = 1.4426950408889634

N_PAGES = PAST_LEN // PAGE_SIZE
N_CMP_BLK = 128
Z_MAIN = 3 * CONV_DIM + ATTN_DIM + 6 * KV_DIM
LANE = 128
VMEM_LIMIT = 56 * 1024 * 1024


def _cparams(*sem):
    return pltpu.CompilerParams(dimension_semantics=sem, vmem_limit_bytes=VMEM_LIMIT)


def _masked_softmax(s, mask):
    s = jnp.where(mask, s, NEG_INF)
    m = jnp.max(s, axis=-1, keepdims=True)
    e = jnp.where(mask, jnp.exp(s - m), 0.0)
    return e / jnp.maximum(jnp.sum(e, axis=-1, keepdims=True), TINY)


def _dot_nt(a, b):
    return lax.dot_general(a, b, (((1,), (1,)), ((), ())), preferred_element_type=F32)


def _dot(a, b):
    return jnp.dot(a, b, preferred_element_type=F32)


def _lane_pick(x, c):
    lane = lax.broadcasted_iota(jnp.int32, x.shape, 1)
    return jnp.sum(jnp.where(lane == c, x, 0.0), axis=-1, keepdims=True)


def _rmsnorm_body(x_ref, g_ref, o_ref):
    x = x_ref[...]
    inv = lax.rsqrt(jnp.mean(x * x, axis=-1, keepdims=True) + RMS_EPS)
    o_ref[...] = ((x * inv) * g_ref[...]).astype(o_ref.dtype)


def rmsnorm_cast(x, g, dtype=BF16, tm=256):
    m, d = x.shape
    return pl.pallas_call(
        _rmsnorm_body, grid=(m // tm,),
        in_specs=[pl.BlockSpec((tm, d), lambda i: (i, 0)), pl.BlockSpec((1, d), lambda i: (0, 0))],
        out_specs=pl.BlockSpec((tm, d), lambda i: (i, 0)),
        out_shape=jax.ShapeDtypeStruct((m, d), dtype),
        compiler_params=_cparams("arbitrary"), name="rmsnorm_cast",
    )(x, g.reshape(1, d))


def _matmul_body(*refs, n_w, has_res, tiles):
    per = n_w + (1 if has_res else 0)
    n_groups = len(tiles) - 1
    o_ref = refs[n_w + n_groups * per]
    wbf = refs[n_w + n_groups * per + 1:]
    i = pl.program_id(1)

    @pl.when(i == 0)
    def _():
        for p in range(n_w):
            wbf[p][...] = refs[p][...].astype(BF16)

    for g in range(n_groups):
        grp = refs[n_w + g * per:n_w + (g + 1) * per]

        def compute(grp=grp):
            acc = _dot(grp[0][...], wbf[0][...])
            for p in range(1, n_w):
                acc = acc + _dot(grp[p][...], wbf[p][...])
            if has_res:
                acc = acc + grp[n_w][...]
            o_ref[...] = acc

        if n_groups == 1:
            compute()
        else:
            pl.when((i >= tiles[g]) & (i < tiles[g + 1]))(compute)


def matmul(groups, weights, tm=512, tn=512):
    ms = [g[0][0].shape[0] for g in groups]
    tm = min([tm] + ms)
    assert all(m % tm == 0 for m in ms)
    n = (weights[0][0].shape[-1] // tn) * tn
    has_res = groups[0][1] is not None
    tiles = [0]
    for m in ms:
        tiles.append(tiles[-1] + m // tm)
    in_specs, args, scratch = [], [], []
    ks = [a.shape[1] for a in groups[0][0]]
    for (w, kb), k in zip(weights, ks):
        if w.ndim == 3:
            in_specs.append(pl.BlockSpec((None, k, tn), lambda j, i, kb=kb: (0, kb, j)))
        else:
            in_specs.append(pl.BlockSpec((k, tn), lambda j, i, kb=kb: (kb, j)))
        args.append(w)
        scratch.append(pltpu.VMEM((k, tn), BF16))
    for g, (a_list, res) in enumerate(groups):
        lo, hi = tiles[g], tiles[g + 1]
        row = lambda i, lo=lo, hi=hi: jnp.clip(i, lo, hi - 1) - lo
        for a, k in zip(a_list, ks):
            in_specs.append(pl.BlockSpec((tm, k), lambda j, i, row=row: (row(i), 0)))
            args.append(a)
        if has_res:
            in_specs.append(pl.BlockSpec((tm, tn), lambda j, i, row=row: (row(i), j)))
            args.append(res)
    return pl.pallas_call(
        functools.partial(_matmul_body, n_w=len(weights), has_res=has_res, tiles=tuple(tiles)),
        grid=(n // tn, tiles[-1]), in_specs=in_specs,
        out_specs=pl.BlockSpec((tm, tn), lambda j, i: (i, j)),
        out_shape=jax.ShapeDtypeStruct((sum(ms), n), F32), scratch_shapes=scratch,
        compiler_params=_cparams("arbitrary", "arbitrary"), name="matmul",
    )(*args)


def _head_norm(x, g):
    inv = lax.rsqrt(jnp.mean(x * x, axis=-1, keepdims=True) + RMS_EPS)
    return (x * inv) * g


def _postproj_body(zq_ref, zc_ref, zs_ref, zw_ref, qg_ref, kg_ref, q_ref, kvc_ref, kvs_ref, kvw_ref):
    for h in range(N_Q_HEADS):
        sl = slice(h * HEAD_DIM, (h + 1) * HEAD_DIM)
        q_ref[:, sl] = _head_norm(zq_ref[:, sl], qg_ref[...]).astype(q_ref.dtype)
    kvc_ref[...] = zc_ref[...]
    for h in range(N_KV_HEADS):
        sl = slice(h * HEAD_DIM, (h + 1) * HEAD_DIM)
        kvs_ref[:, sl] = _head_norm(zs_ref[:, sl], kg_ref[1:2, :])
        kvw_ref[:, sl] = _head_norm(zw_ref[:, sl], kg_ref[2:3, :])
    kvs_ref[:, KV_DIM:] = zs_ref[:, KV_DIM:]
    kvw_ref[:, KV_DIM:] = zw_ref[:, KV_DIM:]


def postproj(z, q_norm_g, k_norm_g, tm=256):
    m = z.shape[0]
    kv = 2 * KV_DIM
    q0 = 3 * CONV_DIM // ATTN_DIM
    c0 = (3 * CONV_DIM + ATTN_DIM) // kv
    return pl.pallas_call(
        _postproj_body, grid=(m // tm,),
        in_specs=[pl.BlockSpec((tm, ATTN_DIM), lambda i: (i, q0)),
                  pl.BlockSpec((tm, kv), lambda i: (i, c0)),
                  pl.BlockSpec((tm, kv), lambda i: (i, c0 + 1)),
                  pl.BlockSpec((tm, kv), lambda i: (i, c0 + 2)),
                  pl.BlockSpec((1, HEAD_DIM), lambda i: (0, 0)),
                  pl.BlockSpec((3, HEAD_DIM), lambda i: (0, 0))],
        out_specs=[pl.BlockSpec((tm, ATTN_DIM), lambda i: (i, 0)),
                   pl.BlockSpec((tm, kv), lambda i: (i, 0)),
                   pl.BlockSpec((tm, kv), lambda i: (i, 0)),
                   pl.BlockSpec((tm, kv), lambda i: (i, 0))],
        out_shape=[jax.ShapeDtypeStruct((m, ATTN_DIM), F32)] + [jax.ShapeDtypeStruct((m, kv), F32)] * 3,
        compiler_params=_cparams("arbitrary"), name="postproj",
    )(z, z, z, z, q_norm_g.reshape(1, HEAD_DIM), k_norm_g)


N_SLOT = 2 * N_KV_HEADS


def _postproj_prompt_body(zq_ref, zc_ref, zs_ref, zw_ref, qg_ref, kg_ref,
                          q_ref, ks_ref, kw_ref, kvc_ref, kvs_ref, kvw_ref):
    tm = zq_ref.shape[0]
    for h in range(N_Q_HEADS):
        sl = slice(h * HEAD_DIM, (h + 1) * HEAD_DIM)
        q_ref[0, h] = _head_norm(zq_ref[:, sl], qg_ref[...]).astype(BF16)
    for c in range(N_SLOT):
        sl = slice(c * HEAD_DIM, (c + 1) * HEAD_DIM)
        rows = pl.ds(c, tm, stride=N_SLOT)
        kvc_ref[rows, :] = zc_ref[:, sl]
        xs, xw = zs_ref[:, sl], zw_ref[:, sl]
        if c < N_KV_HEADS:
            xs, xw = _head_norm(xs, kg_ref[1:2, :]), _head_norm(xw, kg_ref[2:3, :])
        kvs_ref[rows, :] = xs
        kvw_ref[rows, :] = xw
        ks_ref[0, c] = xs.astype(BF16)
        kw_ref[0, c] = xw.astype(BF16)


def postproj_prompt(z, n, t, q_norm_g, k_norm_g, tm=256):
    kv = 2 * KV_DIM
    nt = t // tm
    q0 = 3 * CONV_DIM // ATTN_DIM
    c0 = (3 * CONV_DIM + ATTN_DIM) // kv
    rows_spec = pl.BlockSpec((tm * N_SLOT, HEAD_DIM), lambda b, i: (b * nt + i, 0))
    slot_spec = pl.BlockSpec((1, N_SLOT, tm, HEAD_DIM), lambda b, i: (b, 0, i, 0))
    rows_sds = jax.ShapeDtypeStruct((n * t * N_SLOT, HEAD_DIM), F32)
    slot_sds = jax.ShapeDtypeStruct((n, N_SLOT, t, HEAD_DIM), BF16)
    return pl.pallas_call(
        _postproj_prompt_body, grid=(n, nt),
        in_specs=[pl.BlockSpec((tm, ATTN_DIM), lambda b, i: (b * nt + i, q0)),
                  pl.BlockSpec((tm, kv), lambda b, i: (b * nt + i, c0)),
                  pl.BlockSpec((tm, kv), lambda b, i: (b * nt + i, c0 + 1)),
                  pl.BlockSpec((tm, kv), lambda b, i: (b * nt + i, c0 + 2)),
                  pl.BlockSpec((1, HEAD_DIM), lambda b, i: (0, 0)),
                  pl.BlockSpec((3, HEAD_DIM), lambda b, i: (0, 0))],
        out_specs=[pl.BlockSpec((1, N_Q_HEADS, tm, HEAD_DIM), lambda b, i: (b, 0, i, 0)),
                   slot_spec, slot_spec, rows_spec, rows_spec, rows_spec],
        out_shape=[jax.ShapeDtypeStruct((n, N_Q_HEADS, t, HEAD_DIM), BF16), slot_sds, slot_sds,
                   rows_sds, rows_sds, rows_sds],
        compiler_params=_cparams("arbitrary", "arbitrary"), name="postproj_prompt",
    )(z, z, z, z, q_norm_g.reshape(1, HEAD_DIM), k_norm_g)


def _conv_finish(b, y, g):
    c = b * y
    inv = lax.rsqrt(jnp.mean(c * c, axis=-1, keepdims=True) + RMS_EPS)
    return ((c * inv) * g).astype(BF16)


def _conv_prompt_body(gb_ref, gc_ref, hc_ref, w_ref, g_ref, o_ref, st_ref, carry_ref):
    tt = gb_ref.shape[0]

    @pl.when(pl.program_id(1) == 0)
    def _():
        carry_ref[...] = jnp.zeros_like(carry_ref)

    u = gc_ref[...] * hc_ref[...]
    prev = carry_ref[...]
    p1, p2 = prev[7:8, :], prev[6:7, :]
    row = lax.broadcasted_iota(jnp.int32, u.shape, 0)
    u1 = jnp.where(row == 0, p1, pltpu.roll(u, 1, axis=0))
    u2 = jnp.where(row == 0, p2, jnp.where(row == 1, p1, pltpu.roll(u, 2, axis=0)))
    y = u2 * w_ref[0:1, :] + u1 * w_ref[1:2, :] + u * w_ref[2:3, :]
    o_ref[...] = _conv_finish(gb_ref[...], y, g_ref[...])
    last = u[tt - 8:tt, :]
    carry_ref[...] = last
    st_ref[0] = last


def conv_prompt(z, n, t, conv_w, gain, tt=256):
    nt = t // tt
    row = lambda b, i: (b * nt + i, 0)
    return pl.pallas_call(
        _conv_prompt_body, grid=(n, nt),
        in_specs=[pl.BlockSpec((tt, CONV_DIM), lambda b, i: (b * nt + i, 0)),
                  pl.BlockSpec((tt, CONV_DIM), lambda b, i: (b * nt + i, 1)),
                  pl.BlockSpec((tt, CONV_DIM), lambda b, i: (b * nt + i, 2)),
                  pl.BlockSpec((CONV_WIDTH, CONV_DIM), lambda b, i: (0, 0)),
                  pl.BlockSpec((1, CONV_DIM), lambda b, i: (0, 0))],
        out_specs=[pl.BlockSpec((tt, CONV_DIM), row),
                   pl.BlockSpec((1, 8, CONV_DIM), lambda b, i: (b, 0, 0))],
        out_shape=[jax.ShapeDtypeStruct((n * t, CONV_DIM), BF16), jax.ShapeDtypeStruct((n, 8, CONV_DIM), F32)],
        scratch_shapes=[pltpu.VMEM((8, CONV_DIM), F32)],
        compiler_params=_cparams("arbitrary", "arbitrary"), name="conv_prompt",
    )(z, z, z, conv_w, gain.reshape(1, CONV_DIM))


def _conv_sample_body(z_ref, pre_ref, w_ref, g_ref, o_ref, st_ref, *, t):
    up = [pre_ref[k] for k in range(CONV_WIDTH - 1)] + [z_ref[1, k] * z_ref[2, k] for k in range(t)]
    for k in range(t):
        y = up[k] * w_ref[0:1, :] + up[k + 1] * w_ref[1:2, :] + up[k + 2] * w_ref[2:3, :]
        o_ref[k] = _conv_finish(z_ref[0, k], y, g_ref[...])
    for k in range(CONV_WIDTH - 1):
        st_ref[k] = up[t + k]


def conv_sample(z, n, t, state, conv_w, gain):
    zt = z[:, :3 * CONV_DIM].reshape(n, t, 3, CONV_DIM).transpose(2, 1, 0, 3)
    whole = lambda shape: pl.BlockSpec(shape, lambda i: (0,) * len(shape))
    out, st = pl.pallas_call(
        functools.partial(_conv_sample_body, t=t), grid=(1,),
        in_specs=[whole((3, t, n, CONV_DIM)), whole((CONV_WIDTH - 1, n, CONV_DIM)),
                  whole((CONV_WIDTH, CONV_DIM)), whole((1, CONV_DIM))],
        out_specs=[whole((t, n, CONV_DIM)), whole((CONV_WIDTH - 1, n, CONV_DIM))],
        out_shape=[jax.ShapeDtypeStruct((t, n, CONV_DIM), BF16),
                   jax.ShapeDtypeStruct((CONV_WIDTH - 1, n, CONV_DIM), F32)],
        compiler_params=_cparams("arbitrary"), name="conv_sample",
    )(zt, state.transpose(1, 0, 2), conv_w, gain.reshape(1, CONV_DIM))
    return out.transpose(1, 0, 2).reshape(n * t, CONV_DIM), st.transpose(1, 0, 2)


def _compress_body(pt_ref, *refs):
    del pt_ref
    pages = refs[:N_PAGES]
    wkv_ref, pe_ref, w2_ref, kg_ref, out_ref = refs[N_PAGES:]
    cpp = PAGE_SIZE // CMP_STRIDE
    x4 = [pages[p][...].reshape(cpp, CMP_STRIDE, N_SLOT, HEAD_DIM) for p in range(N_PAGES)]
    n_rows = N_PAGES * cpp * N_SLOT
    is_key = (lax.broadcasted_iota(jnp.int32, (n_rows, HEAD_DIM), 0) & (N_SLOT - 1)) < N_KV_HEADS

    def split(x):
        return [jnp.where(is_key, x, 0.0).astype(BF16), jnp.where(is_key, 0.0, x).astype(BF16)]

    pieces = []
    for s in range(CMP_STRIDE):
        pieces += split(jnp.concatenate([x4[p][:, s].reshape(cpp * N_SLOT, HEAD_DIM) for p in range(N_PAGES)], axis=0))
    wkv = wkv_ref[...].astype(BF16)
    r = _dot(jnp.concatenate(pieces, axis=1), wkv)
    hpre = r[:, :HEAD_DIM] + pltpu.roll(r[:, HEAD_DIM:], n_rows - N_SLOT, axis=0)
    bias = []
    for v in range(2):
        b = jnp.zeros((8, HEAD_DIM), F32)
        for rr in range(R_CMP):
            pe = jnp.broadcast_to(pe_ref[v, rr:rr + 1, :], (8, wkv.shape[0])).astype(BF16)
            b = b + _dot(pe, wkv[:, rr * HEAD_DIM:(rr + 1) * HEAD_DIM])
        bias.append(b[0:1, :])
    hid = jax.nn.gelu(hpre + jnp.where(is_key, bias[0], bias[1]))
    out = _dot(jnp.concatenate(split(hid), axis=1), w2_ref[...].astype(BF16))
    out_ref[0] = jnp.where(is_key, _head_norm(out, kg_ref[0:1, :]), out)


def compress(pool, page_table, wkv, pe_kv, w2cat, k_norm_g):
    n = page_table.shape[0]
    page_rows = PAGE_SIZE * N_SLOT
    page_spec = lambda p: pl.BlockSpec((page_rows, HEAD_DIM), lambda i, pt, p=p: (pt[i, p], 0))
    const = lambda shape: pl.BlockSpec(shape, lambda i, pt: (0,) * len(shape))
    return pl.pallas_call(
        _compress_body,
        grid_spec=pltpu.PrefetchScalarGridSpec(
            num_scalar_prefetch=1, grid=(n,),
            in_specs=[page_spec(p) for p in range(N_PAGES)]
            + [const(wkv.shape), const(pe_kv.shape), const(w2cat.shape), const(k_norm_g.shape)],
            out_specs=pl.BlockSpec((1, N_CMP_BLK * N_SLOT, HEAD_DIM), lambda i, pt: (i, 0, 0))),
        out_shape=jax.ShapeDtypeStruct((n, N_CMP_BLK * N_SLOT, HEAD_DIM), F32),
        compiler_params=_cparams("arbitrary"), name="compress",
    )(page_table, *([pool] * N_PAGES), wkv, pe_kv, w2cat, k_norm_g)


def _cmp_select_body(q_ref, kvc_ref, cov_ref, o_ref, sel_ref, *, bn, tq, pos_base):
    qt = pl.program_id(1)
    grp = GQA * tq
    pairs = [(b, j) for b in range(bn) for j in range(N_KV_HEADS)]

    def slot(b, c):
        return kvc_ref[b, pl.ds(c, N_CMP_BLK, stride=N_SLOT), :].astype(BF16)

    s = jnp.concatenate(
        [_dot_nt(q_ref[b, j * GQA:(j + 1) * GQA].reshape(grp, HEAD_DIM).astype(BF16), slot(b, j)) for b, j in pairs],
        axis=0) * SCALE
    rows = len(pairs) * grp
    pos = pos_base + qt * tq + (lax.broadcasted_iota(jnp.int32, (rows, N_CMP_BLK), 0) & (tq - 1))
    blk = lax.broadcasted_iota(jnp.int32, (rows, N_CMP_BLK), 1)
    valid = (blk < N_CMP_BLK - 1) & (blk * CMP_STRIDE + (L_CMP - 1) <= pos)
    p = _masked_softmax(s, valid)
    p_bf = p.astype(BF16)
    for i, (b, j) in enumerate(pairs):
        o_ref[b, j * GQA:(j + 1) * GQA] = _dot(p_bf[i * grp:(i + 1) * grp],
                                               slot(b, N_KV_HEADS + j)).reshape(GQA, tq, HEAD_DIM)
    psum = jnp.sum(p.reshape(len(pairs), GQA, tq, N_CMP_BLK), axis=1).reshape(len(pairs) * tq, N_CMP_BLK)
    p_hi = psum.astype(BF16)
    p_lo = (psum - p_hi.astype(F32)).astype(BF16)
    imp = _dot(p_hi, cov_ref[...]) + _dot(p_lo, cov_ref[...])
    srows = len(pairs) * tq
    blk = lax.broadcasted_iota(jnp.int32, (srows, LANE), 1)
    cur = (pos_base + qt * tq + (lax.broadcasted_iota(jnp.int32, (srows, LANE), 0) & (tq - 1))) >> 6
    forced = (blk == 0) | (blk == cur) | (blk == cur - 1)
    score = jnp.where(blk <= cur, imp + jnp.where(forced, FORCE_BONUS, 0.0), NEG_INF)
    sel = jnp.zeros((srows, LANE), F32)
    for _ in range(N_SEL):
        m = jnp.max(score, axis=-1, keepdims=True)
        first = jnp.min(jnp.where(score == m, blk, LANE), axis=-1, keepdims=True)
        hit = blk == first
        sel = jnp.where(hit, 1.0, sel)
        score = jnp.where(hit, PICKED, score)
    for i, (b, j) in enumerate(pairs):
        sel_ref[b, j] = sel[i * tq:(i + 1) * tq]


def cmp_select(q, kvc, cover, bn, tq, pos_base):
    n, _, t, _ = q.shape
    assert L_SEL == 64 and tq & (tq - 1) == 0
    return pl.pallas_call(
        functools.partial(_cmp_select_body, bn=bn, tq=tq, pos_base=pos_base), grid=(n // bn, t // tq),
        in_specs=[pl.BlockSpec((bn, N_Q_HEADS, tq, HEAD_DIM), lambda b, i: (b, 0, i, 0)),
                  pl.BlockSpec((bn, N_CMP_BLK * N_SLOT, HEAD_DIM), lambda b, i: (b, 0, 0)),
                  pl.BlockSpec((N_CMP_BLK, LANE), lambda b, i: (0, 0))],
        out_specs=[pl.BlockSpec((bn, N_Q_HEADS, tq, HEAD_DIM), lambda b, i: (b, 0, i, 0)),
                   pl.BlockSpec((bn, N_KV_HEADS, tq, LANE), lambda b, i: (b, 0, i, 0))],
        out_shape=[jax.ShapeDtypeStruct((n, N_Q_HEADS, t, HEAD_DIM), F32),
                   jax.ShapeDtypeStruct((n, N_KV_HEADS, t, LANE), F32)],
        compiler_params=_cparams("arbitrary", "arbitrary"), name="cmp_select",
    )(q, kvc, cover)


SOFTMAX_ROWS = 16


def _attn_prompt_body(*refs, tq, t, selected):
    if selected:
        q_ref, k_ref, v_ref, sel_ref, e_ref, o_ref, s_ref, bias_ref, p_ref, inv_ref = refs
    else:
        q_ref, k_ref, v_ref, o_ref, s_ref, bias_ref, p_ref, inv_ref = refs
    qt = pl.program_id(2)
    rows = GQA * tq
    q = q_ref[0].reshape(rows, HEAD_DIM)

    def attend(start, nk):
        k = k_ref[0, 0, pl.ds(start, nk), :]
        v = v_ref[0, 0, pl.ds(start, nk), :]
        s_ref[:, 0:nk] = _dot_nt(q, k)
        qpos = qt * tq + lax.broadcasted_iota(jnp.int32, (tq, nk), 0)
        kpos = start + lax.broadcasted_iota(jnp.int32, (tq, nk), 1)
        ok = kpos <= qpos
        if selected:
            ok = ok & (_dot(sel_ref[0, 0].astype(BF16), e_ref[:, 0:nk]) > 0.5)
        else:
            ok = ok & (kpos > qpos - WINDOW)
        bias_ref[:, 0:nk] = jnp.where(ok, 0.0, NEG_INF)

        for r0 in range(0, rows, SOFTMAX_ROWS):
            b0 = r0 % tq
            sb = s_ref[r0:r0 + SOFTMAX_ROWS, 0:nk] * (SCALE * LOG2_E) + bias_ref[b0:b0 + SOFTMAX_ROWS, 0:nk]
            m = jnp.max(sb, axis=-1, keepdims=True)
            e = jnp.exp2(sb - m)
            p_ref[r0:r0 + SOFTMAX_ROWS, 0:nk] = e.astype(BF16)
            total = jnp.maximum(jnp.sum(e, axis=-1, keepdims=True), TINY)
            inv_ref[r0:r0 + SOFTMAX_ROWS, :] = jnp.broadcast_to(jnp.where(m > 0.5 * NEG_INF, 1.0 / total, 0.0),
                                                                (SOFTMAX_ROWS, HEAD_DIM))
        o_ref[0] = (_dot(p_ref[:, 0:nk], v) * inv_ref[...]).reshape(GQA, tq, HEAD_DIM)

    if selected:
        n_bucket = t // WINDOW
        per = WINDOW // tq
        for b in range(n_bucket):
            @pl.when(qt // per == b)
            def _(b=b):
                attend(0, (b + 1) * WINDOW)
    else:
        attend(pl.multiple_of(jnp.maximum(qt * tq - WINDOW, 0), tq), WINDOW + tq)


def attn_prompt(q, kv, sel, expand, tq=128):
    n, _, t, _ = q.shape
    selected = sel is not None
    in_specs = [pl.BlockSpec((1, GQA, tq, HEAD_DIM), lambda b, j, i: (b, j, i, 0)),
                pl.BlockSpec((1, 1, t, HEAD_DIM), lambda b, j, i: (b, j, 0, 0)),
                pl.BlockSpec((1, 1, t, HEAD_DIM), lambda b, j, i: (b, N_KV_HEADS + j, 0, 0))]
    args = [q, kv, kv]
    if selected:
        in_specs += [pl.BlockSpec((1, 1, tq, LANE), lambda b, j, i: (b, j, i, 0)),
                     pl.BlockSpec(expand.shape, lambda b, j, i: (0, 0))]
        args += [sel, expand]
    nk_max = t if selected else WINDOW + tq
    return pl.pallas_call(
        functools.partial(_attn_prompt_body, tq=tq, t=t, selected=selected), grid=(n, N_KV_HEADS, t // tq),
        in_specs=in_specs,
        out_specs=pl.BlockSpec((1, GQA, tq, HEAD_DIM), lambda b, j, i: (b, j, i, 0)),
        out_shape=jax.ShapeDtypeStruct((n, N_Q_HEADS, t, HEAD_DIM), F32),
        scratch_shapes=[pltpu.VMEM((GQA * tq, nk_max), F32), pltpu.VMEM((tq, nk_max), F32),
                        pltpu.VMEM((GQA * tq, nk_max), BF16), pltpu.VMEM((GQA * tq, HEAD_DIM), F32)],
        compiler_params=_cparams("arbitrary", "arbitrary", "arbitrary"),
        name="attn_prompt_sel" if selected else "attn_prompt_win",
    )(*args)


T_PAD = 8


def _pad_keys(x):
    return jnp.concatenate([x, jnp.zeros((PAGE_SIZE - x.shape[0], x.shape[1]), x.dtype)], axis=0)


def _attn_sample_sel_body(pt_ref, *refs, t_real):
    del pt_ref
    pages = refs[:N_PAGES]
    q_ref, sel_ref, new_ref, e_ref, o_ref = refs[N_PAGES:]
    rows = GQA * T_PAD
    tok = lax.broadcasted_iota(jnp.int32, (rows, PAGE_SIZE), 0) & (T_PAD - 1)
    col = lax.broadcasted_iota(jnp.int32, (rows, PAGE_SIZE), 1)
    for j in range(N_KV_HEADS):
        kc0, vc0 = j * HEAD_DIM, KV_DIM + j * HEAD_DIM
        q = q_ref[0, j * GQA:(j + 1) * GQA].reshape(rows, HEAD_DIM).astype(BF16)
        k_rows = pl.ds(j, PAGE_SIZE, stride=N_SLOT)
        v_rows = pl.ds(N_KV_HEADS + j, PAGE_SIZE, stride=N_SLOT)
        parts = [_dot_nt(q, pages[p][k_rows, :].astype(BF16)) for p in range(N_PAGES)]
        parts.append(_dot_nt(q, _pad_keys(new_ref[0, :, kc0:kc0 + HEAD_DIM]).astype(BF16)))
        s = jnp.concatenate(parts, axis=1) * SCALE
        sel = sel_ref[0, j]
        sel4 = jnp.concatenate([sel] * GQA, axis=0)
        picked = _dot(sel4.astype(BF16), e_ref[...])
        new_blk = PAST_LEN // L_SEL
        new_ok = (col < t_real) & (col <= tok)
        new_picked = jnp.where(new_ok, _lane_pick(sel4, new_blk), 0.0)
        mask = jnp.concatenate([picked, new_picked], axis=1) > 0.5
        p = _masked_softmax(s, mask).astype(BF16)
        o = _dot(p[:, PAST_LEN:], _pad_keys(new_ref[0, :, vc0:vc0 + HEAD_DIM]).astype(BF16))
        for pg in range(N_PAGES):
            o = o + _dot(p[:, pg * PAGE_SIZE:(pg + 1) * PAGE_SIZE], pages[pg][v_rows, :].astype(BF16))
        o_ref[0, j * GQA:(j + 1) * GQA] = o.reshape(GQA, T_PAD, HEAD_DIM)


def attn_sample_sel(pool, page_table, q, sel, new_rows, expand, t_real):
    n = page_table.shape[0]
    cols = new_rows.shape[-1]
    page_spec = lambda p: pl.BlockSpec((PAGE_SIZE * N_SLOT, HEAD_DIM), lambda i, pt, p=p: (pt[i, p], 0))
    return pl.pallas_call(
        functools.partial(_attn_sample_sel_body, t_real=t_real),
        grid_spec=pltpu.PrefetchScalarGridSpec(
            num_scalar_prefetch=1, grid=(n,),
            in_specs=[page_spec(p) for p in range(N_PAGES)]
            + [pl.BlockSpec((1, N_Q_HEADS, T_PAD, HEAD_DIM), lambda i, pt: (i, 0, 0, 0)),
               pl.BlockSpec((1, N_KV_HEADS, T_PAD, LANE), lambda i, pt: (i, 0, 0, 0)),
               pl.BlockSpec((1, T_PAD, cols), lambda i, pt: (i, 0, 0)),
               pl.BlockSpec(expand.shape, lambda i, pt: (0, 0))],
            out_specs=pl.BlockSpec((1, N_Q_HEADS, T_PAD, HEAD_DIM), lambda i, pt: (i, 0, 0, 0))),
        out_shape=jax.ShapeDtypeStruct((n, N_Q_HEADS, T_PAD, HEAD_DIM), F32),
        compiler_params=_cparams("arbitrary"), name="attn_sample_sel",
    )(page_table, *([pool] * N_PAGES), q, sel, new_rows, expand)


def _attn_sample_win_body(win_ref, q_ref, new_ref, newrows_ref, o_ref, wout_ref, *, t_real):
    rows = GQA * T_PAD
    w_buf = win_ref.shape[0] // N_SLOT
    tok_o = lax.broadcasted_iota(jnp.int32, (rows, w_buf), 0) & (T_PAD - 1)
    col_o = lax.broadcasted_iota(jnp.int32, (rows, w_buf), 1)
    tok_n = lax.broadcasted_iota(jnp.int32, (rows, PAGE_SIZE), 0) & (T_PAD - 1)
    col_n = lax.broadcasted_iota(jnp.int32, (rows, PAGE_SIZE), 1)
    old_ok = jnp.where(col_o + (WINDOW - w_buf) > tok_o, 1.0, 0.0)
    new_ok = jnp.where((col_n < t_real) & (col_n <= tok_n), 1.0, 0.0)
    mask = jnp.concatenate([old_ok, new_ok], axis=1) > 0.5
    for j in range(N_KV_HEADS):
        kc0, vc0 = j * HEAD_DIM, KV_DIM + j * HEAD_DIM
        q = q_ref[0, j * GQA:(j + 1) * GQA].reshape(rows, HEAD_DIM).astype(BF16)
        k_new = _pad_keys(new_ref[0, :, kc0:kc0 + HEAD_DIM]).astype(BF16)
        v_new = _pad_keys(new_ref[0, :, vc0:vc0 + HEAD_DIM]).astype(BF16)
        k_old = win_ref[pl.ds(j, w_buf, stride=N_SLOT), :].astype(BF16)
        v_old = win_ref[pl.ds(N_KV_HEADS + j, w_buf, stride=N_SLOT), :].astype(BF16)
        s = jnp.concatenate([_dot_nt(q, k_old), _dot_nt(q, k_new)], axis=1) * SCALE
        p = _masked_softmax(s, mask).astype(BF16)
        o = _dot(p[:, :w_buf], v_old) + _dot(p[:, w_buf:], v_new)
        o_ref[0, j * GQA:(j + 1) * GQA] = o.reshape(GQA, T_PAD, HEAD_DIM)
    keep = (w_buf - t_real) * N_SLOT
    wout_ref[0:keep, :] = win_ref[t_real * N_SLOT:, :]
    wout_ref[keep:, :] = newrows_ref[...]


def attn_sample_win(win, q, new_rows, new_cache_rows, t_real):
    n = q.shape[0]
    cols = new_rows.shape[-1]
    buf_rows = win.shape[0] // n
    return pl.pallas_call(
        functools.partial(_attn_sample_win_body, t_real=t_real), grid=(n,),
        in_specs=[pl.BlockSpec((buf_rows, HEAD_DIM), lambda i: (i, 0)),
                  pl.BlockSpec((1, N_Q_HEADS, T_PAD, HEAD_DIM), lambda i: (i, 0, 0, 0)),
                  pl.BlockSpec((1, T_PAD, cols), lambda i: (i, 0, 0)),
                  pl.BlockSpec((t_real * N_SLOT, HEAD_DIM), lambda i: (i, 0))],
        out_specs=[pl.BlockSpec((1, N_Q_HEADS, T_PAD, HEAD_DIM), lambda i: (i, 0, 0, 0)),
                   pl.BlockSpec((buf_rows, HEAD_DIM), lambda i: (i, 0))],
        out_shape=[jax.ShapeDtypeStruct((n, N_Q_HEADS, T_PAD, HEAD_DIM), F32),
                   jax.ShapeDtypeStruct(win.shape, F32)],
        compiler_params=_cparams("arbitrary"), name="attn_sample_win",
    )(win, q, new_rows, new_cache_rows)


def _combine_body(oc_ref, os_ref, ow_ref, gl_ref, g_ref, out_ref, *, rows):
    gates = jax.nn.sigmoid(gl_ref[...])
    outs = []
    sq = jnp.zeros((rows, 1), F32)
    for h in range(N_Q_HEADS):
        a = (_lane_pick(gates, h) * oc_ref[:, h].reshape(rows, HEAD_DIM)
             + _lane_pick(gates, N_Q_HEADS + h) * os_ref[:, h].reshape(rows, HEAD_DIM)
             + _lane_pick(gates, 2 * N_Q_HEADS + h) * ow_ref[:, h].reshape(rows, HEAD_DIM))
        outs.append(a)
        sq = sq + jnp.sum(a * a, axis=-1, keepdims=True)
    inv = lax.rsqrt(sq * (1.0 / ATTN_DIM) + RMS_EPS)
    for h in range(N_Q_HEADS):
        sl = slice(h * HEAD_DIM, (h + 1) * HEAD_DIM)
        out_ref[:, sl] = ((outs[h] * inv) * g_ref[:, sl]).astype(out_ref.dtype)


def combine(o_cmp, o_sel, o_win, gate_logits, gain, bn, tt):
    n, _, t, _ = o_cmp.shape
    nt = t // tt
    rows = bn * tt
    o_spec = pl.BlockSpec((bn, N_Q_HEADS, tt, HEAD_DIM), lambda b, i: (b, 0, i, 0))
    return pl.pallas_call(
        functools.partial(_combine_body, rows=rows), grid=(n // bn, nt),
        in_specs=[o_spec, o_spec, o_spec,
                  pl.BlockSpec((rows, LANE), lambda b, i: (b * nt + i, 0)),
                  pl.BlockSpec((1, ATTN_DIM), lambda b, i: (0, 0))],
        out_specs=pl.BlockSpec((rows, ATTN_DIM), lambda b, i: (b * nt + i, 0)),
        out_shape=jax.ShapeDtypeStruct((n * t, ATTN_DIM), BF16),
        compiler_params=_cparams("arbitrary", "arbitrary"), name="combine",
    )(o_cmp, o_sel, o_win, gate_logits, gain.reshape(1, ATTN_DIM))


def _router_body(x_ref, w_ref, b_ref, ids_ref, wts_ref):
    logits = _dot(x_ref[...].astype(BF16), w_ref[...].astype(BF16)) + b_ref[...]
    lane = lax.broadcasted_iota(jnp.int32, logits.shape, 1)
    is_grp = (lane >= N_EXPERTS) & (lane < N_EXPERTS + N_GROUPS)
    gl = jnp.where(is_grp, logits, NEG_INF)
    ge = jnp.where(is_grp, jnp.exp(gl - jnp.max(gl, axis=-1, keepdims=True)), 0.0)
    p_grp = ge / jnp.sum(ge, axis=-1, keepdims=True)
    g_val = jnp.max(p_grp, axis=-1, keepdims=True)
    g_idx = jnp.min(jnp.where(is_grp & (p_grp == g_val), lane, 2 * LANE), axis=-1, keepdims=True) - N_EXPERTS
    lo = g_idx * EXPERTS_PER_GROUP
    in_grp = (lane >= lo) & (lane < lo + EXPERTS_PER_GROUP)
    el = jnp.where(in_grp, logits, NEG_INF)
    ee = jnp.where(in_grp, jnp.exp(el - jnp.max(el, axis=-1, keepdims=True)), 0.0)
    p_e = ee / jnp.sum(ee, axis=-1, keepdims=True)
    cand = jnp.where(in_grp, p_e, -1.0)
    e1 = jnp.max(cand, axis=-1, keepdims=True)
    i1 = jnp.min(jnp.where(cand == e1, lane, 2 * LANE), axis=-1, keepdims=True)
    cand = jnp.where(lane == i1, -1.0, cand)
    e2 = jnp.max(cand, axis=-1, keepdims=True)
    i2 = jnp.min(jnp.where(cand == e2, lane, 2 * LANE), axis=-1, keepdims=True)
    tot = e1 + e2
    ids_ref[...] = jnp.where(lane == 0, i1, jnp.where(lane == 1, i2, 0))
    wts_ref[...] = jnp.where(lane == 0, g_val * e1 / tot, jnp.where(lane == 1, g_val * e2 / tot, 0.0))


def router(xt, w_router, b_router, tm=256):
    m, d = xt.shape
    out_spec = pl.BlockSpec((tm, LANE), lambda i: (i, 0))
    return pl.pallas_call(
        _router_body, grid=(m // tm,),
        in_specs=[pl.BlockSpec((tm, d), lambda i: (i, 0)),
                  pl.BlockSpec((d, LANE), lambda i: (0, 0)),
                  pl.BlockSpec((1, LANE), lambda i: (0, 0))],
        out_specs=[out_spec, out_spec],
        out_shape=[jax.ShapeDtypeStruct((m, LANE), jnp.int32), jax.ShapeDtypeStruct((m, LANE), F32)],
        compiler_params=_cparams("arbitrary"), name="router",
    )(xt, w_router, b_router)


def _gather_rows(idx_ref, n_rows, src_hbm, dst, sem, wait):
    def copy(r, src):
        return pltpu.make_async_copy(src_hbm.at[pl.ds(src, 1), :], dst.at[pl.ds(r, 1), :], sem)

    if wait:
        def body(r, c):
            copy(r, 0).wait()
            return c
        lax.fori_loop(0, n_rows, body, 0, unroll=8)
    else:
        for r in range(n_rows):
            copy(r, idx_ref[0, 0, r]).start()


def _moe_ffn_body(te_ref, cur_ref, nxt_ref, x_hbm, rw_ref, wg_ref, wu_ref, wd_ref, ys_ref, xbuf, sem):
    del te_ref
    i = pl.program_id(0)
    n = pl.num_programs(0)
    tm = xbuf.shape[1]
    slot = lax.rem(i, 2)

    @pl.when(i == 0)
    def _():
        _gather_rows(cur_ref, tm, x_hbm, xbuf.at[0], sem.at[0], wait=False)

    _gather_rows(cur_ref, tm, x_hbm, xbuf.at[slot], sem.at[slot], wait=True)
    _gather_rows(nxt_ref, tm, x_hbm, xbuf.at[1 - slot], sem.at[1 - slot], wait=False)
    x = xbuf[slot].astype(BF16)
    hid = jax.nn.silu(_dot(x, wg_ref[0])) * _dot(x, wu_ref[0])
    gate = jnp.concatenate([rw_ref[...]] * (hid.shape[1] // LANE), axis=1)
    ys_ref[...] = _dot((hid * gate).astype(BF16), wd_ref[0])

    @pl.when(i == n - 1)
    def _():
        _gather_rows(cur_ref, tm, x_hbm, xbuf.at[1 - slot], sem.at[1 - slot], wait=True)


def moe_ffn(xt, tile_expert, row_token, row_weight, w_gate, w_up, w_down, tm):
    n_tiles = tile_expert.shape[0]
    d = xt.shape[1]
    f = w_gate.shape[-1]
    smem_rows = lambda fn: pl.BlockSpec((1, 1, tm), fn, memory_space=pltpu.SMEM)
    return pl.pallas_call(
        _moe_ffn_body,
        grid_spec=pltpu.PrefetchScalarGridSpec(
            num_scalar_prefetch=1, grid=(n_tiles,),
            in_specs=[smem_rows(lambda i, te: (i, 0, 0)),
                      smem_rows(lambda i, te: (jnp.minimum(i + 1, n_tiles - 1), 0, 0)),
                      pl.BlockSpec(memory_space=pl.ANY),
                      pl.BlockSpec((tm, LANE), lambda i, te: (i, 0)),
                      pl.BlockSpec((1, d, f), lambda i, te: (te[i], 0, 0)),
                      pl.BlockSpec((1, d, f), lambda i, te: (te[i], 0, 0)),
                      pl.BlockSpec((1, f, d), lambda i, te: (te[i], 0, 0))],
            out_specs=pl.BlockSpec((tm, d), lambda i, te: (i, 0)),
            scratch_shapes=[pltpu.VMEM((2, tm, d), F32), pltpu.SemaphoreType.DMA((2,))]),
        out_shape=jax.ShapeDtypeStruct((n_tiles * tm, d), F32),
        compiler_params=_cparams("arbitrary"), name="moe_ffn",
    )(tile_expert, row_token, row_token, xt, row_weight, w_gate, w_up, w_down)


def _moe_combine_body(cur_ref, nxt_ref, ys_hbm, h_ref, ya_ref, yb_ref, buf, sem, *, tiles_a):
    i = pl.program_id(0)
    n = pl.num_programs(0)
    rows = buf.shape[1]
    tm = h_ref.shape[0]
    slot = lax.rem(i, 2)

    @pl.when(i == 0)
    def _():
        _gather_rows(cur_ref, rows, ys_hbm, buf.at[0], sem.at[0], wait=False)

    _gather_rows(cur_ref, rows, ys_hbm, buf.at[slot], sem.at[slot], wait=True)
    _gather_rows(nxt_ref, rows, ys_hbm, buf.at[1 - slot], sem.at[1 - slot], wait=False)
    y = h_ref[...] + buf[slot, 0:tm, :] + buf[slot, tm:rows, :]

    @pl.when(i == n - 1)
    def _():
        _gather_rows(cur_ref, rows, ys_hbm, buf.at[1 - slot], sem.at[1 - slot], wait=True)

    @pl.when(i < tiles_a)
    def _():
        ya_ref[...] = y

    @pl.when(i >= tiles_a)
    def _():
        yb_ref[...] = y


def moe_combine(ys, pair_row, h, rows_a, tm=128):
    m, d = h.shape
    n_tiles = m // tm
    tiles_a = rows_a // tm
    assert rows_a % tm == 0 and 0 < tiles_a < n_tiles
    smem_rows = lambda fn: pl.BlockSpec((1, 1, 2 * tm), fn, memory_space=pltpu.SMEM)
    return pl.pallas_call(
        functools.partial(_moe_combine_body, tiles_a=tiles_a), grid=(n_tiles,),
        in_specs=[smem_rows(lambda i: (i, 0, 0)),
                  smem_rows(lambda i: (jnp.minimum(i + 1, n_tiles - 1), 0, 0)),
                  pl.BlockSpec(memory_space=pl.ANY),
                  pl.BlockSpec((tm, d), lambda i: (i, 0))],
        out_specs=[pl.BlockSpec((tm, d), lambda i: (jnp.minimum(i, tiles_a - 1), 0)),
                   pl.BlockSpec((tm, d), lambda i: (jnp.maximum(i - tiles_a, 0), 0))],
        out_shape=[jax.ShapeDtypeStruct((rows_a, d), F32), jax.ShapeDtypeStruct((m - rows_a, d), F32)],
        scratch_shapes=[pltpu.VMEM((2, 2 * tm, d), F32), pltpu.SemaphoreType.DMA((2,))],
        compiler_params=_cparams("arbitrary"), name="moe_combine",
    )(pair_row, pair_row, ys, h)


def moe_routed(xt, ids, wts, h, w_gate, w_up, w_down, tm, rows_a):
    m = xt.shape[0]
    n_pair = 2 * m
    n_tiles = (n_pair + N_EXPERTS * (tm - 1)) // tm + 1
    flat_e = ids[:, :2].reshape(n_pair)
    flat_w = wts[:, :2].reshape(n_pair)
    order = jnp.argsort(flat_e, stable=True).astype(jnp.int32)
    rank = jnp.argsort(order).astype(jnp.int32)
    counts = jnp.sum(flat_e[:, None] == jnp.arange(N_EXPERTS, dtype=jnp.int32)[None, :], axis=0, dtype=jnp.int32)
    padded = ((counts + tm - 1) // tm) * tm
    pad_end = jnp.cumsum(padded)
    pad_start = pad_end - padded
    start = jnp.cumsum(counts) - counts
    tile_start = jnp.arange(n_tiles, dtype=jnp.int32) * tm
    tile_expert = jnp.minimum(jnp.sum(tile_start[:, None] >= pad_end[None, :], axis=1), N_EXPERTS - 1).astype(jnp.int32)
    row_e = jnp.repeat(tile_expert, tm)
    offs = jnp.arange(n_tiles * tm, dtype=jnp.int32) - pad_start[row_e]
    used = offs < counts[row_e]
    src = order[jnp.clip(start[row_e] + offs, 0, n_pair - 1)]
    row_token = jnp.where(used, src // 2, 0)
    row_weight = jnp.where(used, flat_w[src], 0.0)
    pair_row = (pad_start[flat_e] + rank - start[flat_e]).reshape(m, 2)
    ys = moe_ffn(xt, tile_expert, row_token.reshape(n_tiles, 1, tm),
                 jnp.broadcast_to(row_weight[:, None], (n_tiles * tm, LANE)), w_gate, w_up, w_down, tm)
    tc = 128
    pair_tiles = pair_row.reshape(m // tc, tc, 2).transpose(0, 2, 1).reshape(m // tc, 1, 2 * tc)
    return moe_combine(ys, pair_tiles, h, rows_a, tm=tc)


def _cover_matrix(nsb):
    i = np.arange(N_CMP_BLK)[:, None]
    j = np.arange(LANE)[None, :]
    m = np.zeros((N_CMP_BLK, LANE), np.float32)
    for a in range(L_SEL // CMP_STRIDE):
        for c in range(R_CMP):
            m += (i == (L_SEL // CMP_STRIDE) * j + a - c)
    m[N_CMP_BLK - 1:, :] = 0.0
    m[:, nsb:] = 0.0
    return jnp.asarray(m, BF16)


def _expand_matrix(n_keys):
    b = np.arange(LANE)[:, None]
    k = np.arange(n_keys)[None, :]
    return jnp.asarray((k // L_SEL == b).astype(np.float32), BF16)


def _to_heads(q, n, t):
    return q.reshape(n, t, N_Q_HEADS, HEAD_DIM).transpose(0, 2, 1, 3)


def kernel(x_prompt, x_sample, cache_cmp_kv, cache_sel_kv, state_win_kv, state_conv, page_table, norm_mix_g, w_in,
           conv_w, q_norm_g, k_norm_g, phi_pe, phi_w1, phi_w2, out_norm_g, w_out, norm_ffn_g, w_group_router,
           b_group_router, w_expert_router, b_expert_router, w_gate, w_up, w_down):
    n_p, t_p, d = x_prompt.shape
    n_s, t_s, _ = x_sample.shape
    assert w_in.shape[0] == 1 and t_s < CMP_STRIDE and t_s <= T_PAD and t_p % WINDOW == 0
    kv_cols = 2 * KV_DIM
    w_in_main = w_in
    w_in_gate = jnp.pad(w_in[0][:, Z_MAIN:], ((0, 0), (0, LANE - 3 * N_Q_HEADS)))
    w_o = w_out
    wkv = phi_w1[0].reshape(2, R_CMP, CMP_STRIDE, HEAD_DIM, HEAD_DIM).transpose(2, 0, 3, 1, 4)
    wkv = wkv.reshape(CMP_STRIDE * 2 * HEAD_DIM, R_CMP * HEAD_DIM)
    pe5 = phi_pe[0].reshape(2, R_CMP, CMP_STRIDE, 1, HEAD_DIM)
    pe_kv = (pe5 * jnp.eye(2, dtype=F32).reshape(2, 1, 1, 2, 1)).reshape(2, R_CMP, CMP_STRIDE * 2 * HEAD_DIM)
    w2cat = phi_w2[0].reshape(2 * HEAD_DIM, HEAD_DIM)
    w_router = jnp.pad(jnp.concatenate([w_expert_router[0], w_group_router[0]], axis=1),
                       ((0, 0), (0, LANE - N_EXPERTS - N_GROUPS)))
    b_router = jnp.pad(jnp.concatenate([b_expert_router[0], b_group_router[0]]),
                       (0, LANE - N_EXPERTS - N_GROUPS)).reshape(1, LANE)
    wg_bf, wu_bf, wd_bf = w_gate[0].astype(BF16), w_up[0].astype(BF16), w_down[0].astype(BF16)

    def project(x2d):
        xn = rmsnorm_cast(x2d, norm_mix_g[0])
        z = matmul([([xn], None)], [(w_in_main, 0)], tm=1024, tn=512)
        gate_logits = matmul([([xn], None)], [(w_in_gate, 0)], tn=LANE)
        return z, gate_logits

    m_p = n_p * t_p

    xp = x_prompt.reshape(n_p * t_p, d)
    z, glog = project(xp)
    qh, ks_h, kw_h, kvc, kvs, kvw = postproj_prompt(z, n_p, t_p, q_norm_g[0], k_norm_g[0])
    conv_out, conv_last = conv_prompt(z, n_p, t_p, conv_w[0], out_norm_g[0][:CONV_DIM])
    ident = jnp.arange(n_p * (t_p // PAGE_SIZE), dtype=jnp.int32).reshape(n_p, t_p // PAGE_SIZE)
    kv_cmp = compress(kvc, ident, wkv, pe_kv, w2cat, k_norm_g[0])
    o_cmp, sel = cmp_select(qh, kv_cmp, _cover_matrix(t_p // L_SEL), bn=1, tq=256, pos_base=0)
    o_sel = attn_prompt(qh, ks_h, sel, _expand_matrix(t_p))
    o_win = attn_prompt(qh, kw_h, None, None)
    attn_out = combine(o_cmp, o_sel, o_win, glog, out_norm_g[0][CONV_DIM:], bn=1, tt=256)
    mixed_p = ([conv_out, attn_out], xp)
    kv_shape = (1, n_p, t_p, 2, N_KV_HEADS, HEAD_DIM)
    w_keep = min(WINDOW, t_p)
    prompt_win = kvw.reshape(kv_shape)[:, :, t_p - w_keep:]
    prompt_conv = conv_last[:, 8 - (CONV_WIDTH - 1):, :][None]

    xs = x_sample.reshape(n_s * t_s, d)
    z, glog = project(xs)
    q, kvc_s, kvs_s, kvw_s = postproj(z, q_norm_g[0], k_norm_g[0])
    conv_out, conv_state = conv_sample(z, n_s, t_s, state_conv[0], conv_w[0], out_norm_g[0][:CONV_DIM])
    pad_t = lambda a: jnp.pad(a, ((0, 0), (0, T_PAD - t_s), (0, 0)))
    qh = jnp.pad(_to_heads(q, n_s, t_s), ((0, 0), (0, 0), (0, T_PAD - t_s), (0, 0)))
    pool_cmp = cache_cmp_kv[0].reshape(-1, HEAD_DIM)
    pool_sel = cache_sel_kv[0].reshape(-1, HEAD_DIM)
    kv_cmp = compress(pool_cmp, page_table, wkv, pe_kv, w2cat, k_norm_g[0])
    o_cmp, sel = cmp_select(qh, kv_cmp, _cover_matrix(PAST_LEN // L_SEL + 1), bn=16, tq=T_PAD, pos_base=PAST_LEN)
    o_sel = attn_sample_sel(pool_sel, page_table, qh, sel, pad_t(kvs_s.reshape(n_s, t_s, kv_cols)),
                            _expand_matrix(PAST_LEN), t_s)
    win = state_win_kv[0].reshape(-1, HEAD_DIM)
    o_win, win_new = attn_sample_win(win, qh, pad_t(kvw_s.reshape(n_s, t_s, kv_cols)),
                                     kvw_s.reshape(-1, HEAD_DIM), t_s)
    glog_pad = pad_t(glog.reshape(n_s, t_s, LANE)).reshape(n_s * T_PAD, LANE)
    attn_out = combine(o_cmp, o_sel, o_win, glog_pad, out_norm_g[0][CONV_DIM:], bn=32, tt=T_PAD)
    attn_out = attn_out.reshape(n_s, T_PAD, ATTN_DIM)[:, :t_s].reshape(n_s * t_s, ATTN_DIM)
    s_shape = (1, n_s, t_s, 2, N_KV_HEADS, HEAD_DIM)

    h_all = matmul([mixed_p, ([conv_out, attn_out], xs)], [(w_o, 0), (w_o, 1)])
    xt = rmsnorm_cast(h_all, norm_ffn_g[0], dtype=F32)
    ids, wts = router(xt, w_router, b_router)
    y_prompt, y_sample = moe_routed(xt, ids, wts, h_all, wg_bf, wu_bf, wd_bf, 256, m_p)

    return (y_prompt.reshape(n_p, t_p, d), y_sample.reshape(n_s, t_s, d), kvc.reshape(kv_shape), kvs.reshape(kv_shape), prompt_win, prompt_conv,
            kvc_s.reshape(s_shape), kvs_s.reshape(s_shape),
            win_new.reshape(1, n_s, -1, 2, N_KV_HEADS, HEAD_DIM), conv_state[None])
```

```python
import functools

import numpy as np
import jax
import jax.numpy as jnp
from jax import lax
from jax.experimental import pallas as pl
from jax.experimental.pallas import tpu as pltpu

F32 = jnp.float32
BF16 = jnp.bfloat16

D_MODEL = 4096
PAST_LEN = 2048
PAGE_SIZE = 128
HEAD_DIM = 128
CONV_DIM = 2048
N_Q_HEADS = 16
N_KV_HEADS = 4
GQA = 4
ATTN_DIM = 2048
KV_DIM = 512
CONV_WIDTH = 3
L_CMP = 32
CMP_STRIDE = 16
R_CMP = 2
L_SEL = 64
N_SEL = 8
WINDOW = 512
FORCE_BONUS = 1e3
SCALE = HEAD_DIM ** -0.5
N_GROUPS = 4
EXPERTS_PER_GROUP = 4
N_EXPERTS = 16
D_FF_EXPERT = 512
RMS_EPS = 1e-6
NEG_INF = -1e30
TINY = 1e-30
PICKED = -3e38
LOG2_E = 1.4426950408889634

N_PAGES = PAST_LEN // PAGE_SIZE
N_CMP_BLK = 128
Z_MAIN = 3 * CONV_DIM + ATTN_DIM + 6 * KV_DIM
LANE = 128
VMEM_LIMIT = 56 * 1024 * 1024


def _cparams(*sem):
    return pltpu.CompilerParams(dimension_semantics=sem, vmem_limit_bytes=VMEM_LIMIT)


def _masked_softmax(s, mask):
    s = jnp.where(mask, s, NEG_INF)
    m = jnp.max(s, axis=-1, keepdims=True)
    e = jnp.where(mask, jnp.exp(s - m), 0.0)
    return e / jnp.maximum(jnp.sum(e, axis=-1, keepdims=True), TINY)


def _dot_nt(a, b):
    return lax.dot_general(a, b, (((1,), (1,)), ((), ())), preferred_element_type=F32)


def _dot(a, b):
    return jnp.dot(a, b, preferred_element_type=F32)


def _lane_pick(x, c):
    lane = lax.broadcasted_iota(jnp.int32, x.shape, 1)
    return jnp.sum(jnp.where(lane == c, x, 0.0), axis=-1, keepdims=True)


def _rmsnorm_body(x_ref, g_ref, o_ref):
    x = x_ref[...]
    inv = lax.rsqrt(jnp.mean(x * x, axis=-1, keepdims=True) + RMS_EPS)
    o_ref[...] = ((x * inv) * g_ref[...]).astype(o_ref.dtype)


def rmsnorm_cast(x, g, dtype=BF16, tm=256):
    m, d = x.shape
    return pl.pallas_call(
        _rmsnorm_body, grid=(m // tm,),
        in_specs=[pl.BlockSpec((tm, d), lambda i: (i, 0)), pl.BlockSpec((1, d), lambda i: (0, 0))],
        out_specs=pl.BlockSpec((tm, d), lambda i: (i, 0)),
        out_shape=jax.ShapeDtypeStruct((m, d), dtype),
        compiler_params=_cparams("arbitrary"), name="rmsnorm_cast",
    )(x, g.reshape(1, d))


def _matmul_body(*refs, n_w, has_res, tiles, w_transposed):
    per = n_w + (1 if has_res else 0)
    n_groups = len(tiles) - 1
    o_ref = refs[n_w + n_groups * per]
    wbf = refs[n_w + n_groups * per + 1:]
    i = pl.program_id(1)
    dot = _dot_nt if w_transposed else _dot

    @pl.when(i == 0)
    def _():
        for p in range(n_w):
            wbf[p][...] = refs[p][...].astype(BF16)

    for g in range(n_groups):
        grp = refs[n_w + g * per:n_w + (g + 1) * per]

        def compute(grp=grp):
            acc = dot(grp[0][...], wbf[0][...])
            for p in range(1, n_w):
                acc = acc + dot(grp[p][...], wbf[p][...])
            if has_res:
                acc = acc + grp[n_w][...]
            o_ref[...] = acc

        if n_groups == 1:
            compute()
        else:
            pl.when((i >= tiles[g]) & (i < tiles[g + 1]))(compute)


def matmul(groups, weights, tm=512, tn=512, w_transposed=False):
    ms = [g[0][0].shape[0] for g in groups]
    tm = min([tm] + ms)
    assert all(m % tm == 0 for m in ms)
    n = (weights[0][0].shape[0 if w_transposed else -1] // tn) * tn
    has_res = groups[0][1] is not None
    tiles = [0]
    for m in ms:
        tiles.append(tiles[-1] + m // tm)
    in_specs, args, scratch = [], [], []
    ks = [a.shape[1] for a in groups[0][0]]
    for (w, kb), k in zip(weights, ks):
        if w_transposed:
            in_specs.append(pl.BlockSpec((tn, k), lambda j, i, kb=kb: (j, kb)))
        elif w.ndim == 3:
            in_specs.append(pl.BlockSpec((None, k, tn), lambda j, i, kb=kb: (0, kb, j)))
        else:
            in_specs.append(pl.BlockSpec((k, tn), lambda j, i, kb=kb: (kb, j)))
        args.append(w)
        scratch.append(pltpu.VMEM((tn, k) if w_transposed else (k, tn), BF16))
    for g, (a_list, res) in enumerate(groups):
        lo, hi = tiles[g], tiles[g + 1]
        row = lambda i, lo=lo, hi=hi: jnp.clip(i, lo, hi - 1) - lo
        for a, k in zip(a_list, ks):
            in_specs.append(pl.BlockSpec((tm, k), lambda j, i, row=row: (row(i), 0)))
            args.append(a)
        if has_res:
            in_specs.append(pl.BlockSpec((tm, tn), lambda j, i, row=row: (row(i), j)))
            args.append(res)
    return pl.pallas_call(
        functools.partial(_matmul_body, n_w=len(weights), has_res=has_res, tiles=tuple(tiles),
                          w_transposed=w_transposed),
        grid=(n // tn, tiles[-1]), in_specs=in_specs,
        out_specs=pl.BlockSpec((tm, tn), lambda j, i: (i, j)),
        out_shape=jax.ShapeDtypeStruct((sum(ms), n), F32), scratch_shapes=scratch,
        compiler_params=_cparams("arbitrary", "arbitrary"), name="matmul",
    )(*args)


def _head_norm(x, g):
    inv = lax.rsqrt(jnp.mean(x * x, axis=-1, keepdims=True) + RMS_EPS)
    return (x * inv) * g


def _postproj_body(zq_ref, zc_ref, zs_ref, zw_ref, qg_ref, kg_ref, q_ref, kvc_ref, kvs_ref, kvw_ref):
    for h in range(N_Q_HEADS):
        sl = slice(h * HEAD_DIM, (h + 1) * HEAD_DIM)
        q_ref[:, sl] = _head_norm(zq_ref[:, sl], qg_ref[...]).astype(q_ref.dtype)
    kvc_ref[...] = zc_ref[...]
    for h in range(N_KV_HEADS):
        sl = slice(h * HEAD_DIM, (h + 1) * HEAD_DIM)
        kvs_ref[:, sl] = _head_norm(zs_ref[:, sl], kg_ref[1:2, :])
        kvw_ref[:, sl] = _head_norm(zw_ref[:, sl], kg_ref[2:3, :])
    kvs_ref[:, KV_DIM:] = zs_ref[:, KV_DIM:]
    kvw_ref[:, KV_DIM:] = zw_ref[:, KV_DIM:]


def postproj(z, q_norm_g, k_norm_g, tm=256):
    m = z.shape[0]
    kv = 2 * KV_DIM
    q0 = 3 * CONV_DIM // ATTN_DIM
    c0 = (3 * CONV_DIM + ATTN_DIM) // kv
    return pl.pallas_call(
        _postproj_body, grid=(m // tm,),
        in_specs=[pl.BlockSpec((tm, ATTN_DIM), lambda i: (i, q0)),
                  pl.BlockSpec((tm, kv), lambda i: (i, c0)),
                  pl.BlockSpec((tm, kv), lambda i: (i, c0 + 1)),
                  pl.BlockSpec((tm, kv), lambda i: (i, c0 + 2)),
                  pl.BlockSpec((1, HEAD_DIM), lambda i: (0, 0)),
                  pl.BlockSpec((3, HEAD_DIM), lambda i: (0, 0))],
        out_specs=[pl.BlockSpec((tm, ATTN_DIM), lambda i: (i, 0)),
                   pl.BlockSpec((tm, kv), lambda i: (i, 0)),
                   pl.BlockSpec((tm, kv), lambda i: (i, 0)),
                   pl.BlockSpec((tm, kv), lambda i: (i, 0))],
        out_shape=[jax.ShapeDtypeStruct((m, ATTN_DIM), F32)] + [jax.ShapeDtypeStruct((m, kv), F32)] * 3,
        compiler_params=_cparams("arbitrary"), name="postproj",
    )(z, z, z, z, q_norm_g.reshape(1, HEAD_DIM), k_norm_g)


N_SLOT = 2 * N_KV_HEADS


def _postproj_prompt_body(zq_ref, zc_ref, zs_ref, zw_ref, qg_ref, kg_ref,
                          q_ref, ks_ref, kw_ref, kvc_ref, kvs_ref, kvw_ref):
    tm = zq_ref.shape[0]
    for h in range(N_Q_HEADS):
        sl = slice(h * HEAD_DIM, (h + 1) * HEAD_DIM)
        q_ref[0, h] = _head_norm(zq_ref[:, sl], qg_ref[...]).astype(BF16)
    for c in range(N_SLOT):
        sl = slice(c * HEAD_DIM, (c + 1) * HEAD_DIM)
        rows = pl.ds(c, tm, stride=N_SLOT)
        kvc_ref[rows, :] = zc_ref[:, sl]
        xs, xw = zs_ref[:, sl], zw_ref[:, sl]
        if c < N_KV_HEADS:
            xs, xw = _head_norm(xs, kg_ref[1:2, :]), _head_norm(xw, kg_ref[2:3, :])
        kvs_ref[rows, :] = xs
        kvw_ref[rows, :] = xw
        ks_ref[0, c] = xs.astype(BF16)
        kw_ref[0, c] = xw.astype(BF16)


def postproj_prompt(z, n, t, q_norm_g, k_norm_g, tm=256):
    kv = 2 * KV_DIM
    nt = t // tm
    q0 = 3 * CONV_DIM // ATTN_DIM
    c0 = (3 * CONV_DIM + ATTN_DIM) // kv
    rows_spec = pl.BlockSpec((tm * N_SLOT, HEAD_DIM), lambda b, i: (b * nt + i, 0))
    slot_spec = pl.BlockSpec((1, N_SLOT, tm, HEAD_DIM), lambda b, i: (b, 0, i, 0))
    rows_sds = jax.ShapeDtypeStruct((n * t * N_SLOT, HEAD_DIM), F32)
    slot_sds = jax.ShapeDtypeStruct((n, N_SLOT, t, HEAD_DIM), BF16)
    return pl.pallas_call(
        _postproj_prompt_body, grid=(n, nt),
        in_specs=[pl.BlockSpec((tm, ATTN_DIM), lambda b, i: (b * nt + i, q0)),
                  pl.BlockSpec((tm, kv), lambda b, i: (b * nt + i, c0)),
                  pl.BlockSpec((tm, kv), lambda b, i: (b * nt + i, c0 + 1)),
                  pl.BlockSpec((tm, kv), lambda b, i: (b * nt + i, c0 + 2)),
                  pl.BlockSpec((1, HEAD_DIM), lambda b, i: (0, 0)),
                  pl.BlockSpec((3, HEAD_DIM), lambda b, i: (0, 0))],
        out_specs=[pl.BlockSpec((1, N_Q_HEADS, tm, HEAD_DIM), lambda b, i: (b, 0, i, 0)),
                   slot_spec, slot_spec, rows_spec, rows_spec, rows_spec],
        out_shape=[jax.ShapeDtypeStruct((n, N_Q_HEADS, t, HEAD_DIM), BF16), slot_sds, slot_sds,
                   rows_sds, rows_sds, rows_sds],
        compiler_params=_cparams("arbitrary", "arbitrary"), name="postproj_prompt",
    )(z, z, z, z, q_norm_g.reshape(1, HEAD_DIM), k_norm_g)


def _conv_finish(b, y, g):
    c = b * y
    inv = lax.rsqrt(jnp.mean(c * c, axis=-1, keepdims=True) + RMS_EPS)
    return ((c * inv) * g).astype(BF16)


def _conv_prompt_body(gb_ref, gc_ref, hc_ref, w_ref, g_ref, o_ref, st_ref, carry_ref):
    tt = gb_ref.shape[0]

    @pl.when(pl.program_id(1) == 0)
    def _():
        carry_ref[...] = jnp.zeros_like(carry_ref)

    u = gc_ref[...] * hc_ref[...]
    prev = carry_ref[...]
    p1, p2 = prev[7:8, :], prev[6:7, :]
    row = lax.broadcasted_iota(jnp.int32, u.shape, 0)
    u1 = jnp.where(row == 0, p1, pltpu.roll(u, 1, axis=0))
    u2 = jnp.where(row == 0, p2, jnp.where(row == 1, p1, pltpu.roll(u, 2, axis=0)))
    y = u2 * w_ref[0:1, :] + u1 * w_ref[1:2, :] + u * w_ref[2:3, :]
    o_ref[...] = _conv_finish(gb_ref[...], y, g_ref[...])
    last = u[tt - 8:tt, :]
    carry_ref[...] = last
    st_ref[0] = last


def conv_prompt(z, n, t, conv_w, gain, tt=256):
    nt = t // tt
    row = lambda b, i: (b * nt + i, 0)
    return pl.pallas_call(
        _conv_prompt_body, grid=(n, nt),
        in_specs=[pl.BlockSpec((tt, CONV_DIM), lambda b, i: (b * nt + i, 0)),
                  pl.BlockSpec((tt, CONV_DIM), lambda b, i: (b * nt + i, 1)),
                  pl.BlockSpec((tt, CONV_DIM), lambda b, i: (b * nt + i, 2)),
                  pl.BlockSpec((CONV_WIDTH, CONV_DIM), lambda b, i: (0, 0)),
                  pl.BlockSpec((1, CONV_DIM), lambda b, i: (0, 0))],
        out_specs=[pl.BlockSpec((tt, CONV_DIM), row),
                   pl.BlockSpec((1, 8, CONV_DIM), lambda b, i: (b, 0, 0))],
        out_shape=[jax.ShapeDtypeStruct((n * t, CONV_DIM), BF16), jax.ShapeDtypeStruct((n, 8, CONV_DIM), F32)],
        scratch_shapes=[pltpu.VMEM((8, CONV_DIM), F32)],
        compiler_params=_cparams("arbitrary", "arbitrary"), name="conv_prompt",
    )(z, z, z, conv_w, gain.reshape(1, CONV_DIM))


def _conv_sample_body(z_ref, pre_ref, w_ref, g_ref, o_ref, st_ref, *, t):
    up = [pre_ref[k] for k in range(CONV_WIDTH - 1)] + [z_ref[1, k] * z_ref[2, k] for k in range(t)]
    for k in range(t):
        y = up[k] * w_ref[0:1, :] + up[k + 1] * w_ref[1:2, :] + up[k + 2] * w_ref[2:3, :]
        o_ref[k] = _conv_finish(z_ref[0, k], y, g_ref[...])
    for k in range(CONV_WIDTH - 1):
        st_ref[k] = up[t + k]


def conv_sample(z, n, t, state, conv_w, gain):
    zt = z[:, :3 * CONV_DIM].reshape(n, t, 3, CONV_DIM).transpose(2, 1, 0, 3)
    whole = lambda shape: pl.BlockSpec(shape, lambda i: (0,) * len(shape))
    out, st = pl.pallas_call(
        functools.partial(_conv_sample_body, t=t), grid=(1,),
        in_specs=[whole((3, t, n, CONV_DIM)), whole((CONV_WIDTH - 1, n, CONV_DIM)),
                  whole((CONV_WIDTH, CONV_DIM)), whole((1, CONV_DIM))],
        out_specs=[whole((t, n, CONV_DIM)), whole((CONV_WIDTH - 1, n, CONV_DIM))],
        out_shape=[jax.ShapeDtypeStruct((t, n, CONV_DIM), BF16),
                   jax.ShapeDtypeStruct((CONV_WIDTH - 1, n, CONV_DIM), F32)],
        compiler_params=_cparams("arbitrary"), name="conv_sample",
    )(zt, state.transpose(1, 0, 2), conv_w, gain.reshape(1, CONV_DIM))
    return out.transpose(1, 0, 2).reshape(n * t, CONV_DIM), st.transpose(1, 0, 2)


def _compress_body(pt_ref, *refs):
    del pt_ref
    pages = refs[:N_PAGES]
    wkv_ref, pe_ref, w2_ref, kg_ref, out_ref = refs[N_PAGES:]
    cpp = PAGE_SIZE // CMP_STRIDE
    x4 = [pages[p][...].reshape(cpp, CMP_STRIDE, N_SLOT, HEAD_DIM) for p in range(N_PAGES)]
    n_rows = N_PAGES * cpp * N_SLOT
    is_key = (lax.broadcasted_iota(jnp.int32, (n_rows, HEAD_DIM), 0) & (N_SLOT - 1)) < N_KV_HEADS

    def split(x):
        return [jnp.where(is_key, x, 0.0).astype(BF16), jnp.where(is_key, 0.0, x).astype(BF16)]

    pieces = []
    for s in range(CMP_STRIDE):
        pieces += split(jnp.concatenate([x4[p][:, s].reshape(cpp * N_SLOT, HEAD_DIM) for p in range(N_PAGES)], axis=0))
    wkv = wkv_ref[...].astype(BF16)
    r = _dot(jnp.concatenate(pieces, axis=1), wkv)
    hpre = r[:, :HEAD_DIM] + pltpu.roll(r[:, HEAD_DIM:], n_rows - N_SLOT, axis=0)
    bias = []
    for v in range(2):
        b = jnp.zeros((8, HEAD_DIM), F32)
        for rr in range(R_CMP):
            pe = jnp.broadcast_to(pe_ref[v, rr:rr + 1, :], (8, wkv.shape[0])).astype(BF16)
            b = b + _dot(pe, wkv[:, rr * HEAD_DIM:(rr + 1) * HEAD_DIM])
        bias.append(b[0:1, :])
    hid = jax.nn.gelu(hpre + jnp.where(is_key, bias[0], bias[1]))
    out = _dot(jnp.concatenate(split(hid), axis=1), w2_ref[...].astype(BF16))
    out_ref[0] = jnp.where(is_key, _head_norm(out, kg_ref[0:1, :]), out)


def compress(pool, page_table, wkv, pe_kv, w2cat, k_norm_g):
    n = page_table.shape[0]
    page_rows = PAGE_SIZE * N_SLOT
    page_spec = lambda p: pl.BlockSpec((page_rows, HEAD_DIM), lambda i, pt, p=p: (pt[i, p], 0))
    const = lambda shape: pl.BlockSpec(shape, lambda i, pt: (0,) * len(shape))
    return pl.pallas_call(
        _compress_body,
        grid_spec=pltpu.PrefetchScalarGridSpec(
            num_scalar_prefetch=1, grid=(n,),
            in_specs=[page_spec(p) for p in range(N_PAGES)]
            + [const(wkv.shape), const(pe_kv.shape), const(w2cat.shape), const(k_norm_g.shape)],
            out_specs=pl.BlockSpec((1, N_CMP_BLK * N_SLOT, HEAD_DIM), lambda i, pt: (i, 0, 0))),
        out_shape=jax.ShapeDtypeStruct((n, N_CMP_BLK * N_SLOT, HEAD_DIM), F32),
        compiler_params=_cparams("arbitrary"), name="compress",
    )(page_table, *([pool] * N_PAGES), wkv, pe_kv, w2cat, k_norm_g)


def _cmp_select_body(q_ref, kvc_ref, cov_ref, o_ref, sel_ref, *, bn, tq, pos_base):
    qt = pl.program_id(1)
    grp = GQA * tq
    pairs = [(b, j) for b in range(bn) for j in range(N_KV_HEADS)]

    def slot(b, c):
        return kvc_ref[b, pl.ds(c, N_CMP_BLK, stride=N_SLOT), :].astype(BF16)

    s = jnp.concatenate(
        [_dot_nt(q_ref[b, j * GQA:(j + 1) * GQA].reshape(grp, HEAD_DIM).astype(BF16), slot(b, j)) for b, j in pairs],
        axis=0) * SCALE
    rows = len(pairs) * grp
    pos = pos_base + qt * tq + (lax.broadcasted_iota(jnp.int32, (rows, N_CMP_BLK), 0) & (tq - 1))
    blk = lax.broadcasted_iota(jnp.int32, (rows, N_CMP_BLK), 1)
    valid = (blk < N_CMP_BLK - 1) & (blk * CMP_STRIDE + (L_CMP - 1) <= pos)
    p = _masked_softmax(s, valid)
    p_bf = p.astype(BF16)
    for i, (b, j) in enumerate(pairs):
        o_ref[b, j * GQA:(j + 1) * GQA] = _dot(p_bf[i * grp:(i + 1) * grp],
                                               slot(b, N_KV_HEADS + j)).reshape(GQA, tq, HEAD_DIM)
    psum = jnp.sum(p.reshape(len(pairs), GQA, tq, N_CMP_BLK), axis=1).reshape(len(pairs) * tq, N_CMP_BLK)
    p_hi = psum.astype(BF16)
    p_lo = (psum - p_hi.astype(F32)).astype(BF16)
    imp = _dot(p_hi, cov_ref[...]) + _dot(p_lo, cov_ref[...])
    srows = len(pairs) * tq
    blk = lax.broadcasted_iota(jnp.int32, (srows, LANE), 1)
    cur = (pos_base + qt * tq + (lax.broadcasted_iota(jnp.int32, (srows, LANE), 0) & (tq - 1))) >> 6
    forced = (blk == 0) | (blk == cur) | (blk == cur - 1)
    score = jnp.where(blk <= cur, imp + jnp.where(forced, FORCE_BONUS, 0.0), NEG_INF)
    sel = jnp.zeros((srows, LANE), F32)
    for _ in range(N_SEL):
        m = jnp.max(score, axis=-1, keepdims=True)
        first = jnp.min(jnp.where(score == m, blk, LANE), axis=-1, keepdims=True)
        hit = blk == first
        sel = jnp.where(hit, 1.0, sel)
        score = jnp.where(hit, PICKED, score)
    for i, (b, j) in enumerate(pairs):
        sel_ref[b, j] = sel[i * tq:(i + 1) * tq]


def cmp_select(q, kvc, cover, bn, tq, pos_base):
    n, _, t, _ = q.shape
    assert L_SEL == 64 and tq & (tq - 1) == 0
    return pl.pallas_call(
        functools.partial(_cmp_select_body, bn=bn, tq=tq, pos_base=pos_base), grid=(n // bn, t // tq),
        in_specs=[pl.BlockSpec((bn, N_Q_HEADS, tq, HEAD_DIM), lambda b, i: (b, 0, i, 0)),
                  pl.BlockSpec((bn, N_CMP_BLK * N_SLOT, HEAD_DIM), lambda b, i: (b, 0, 0)),
                  pl.BlockSpec((N_CMP_BLK, LANE), lambda b, i: (0, 0))],
        out_specs=[pl.BlockSpec((bn, N_Q_HEADS, tq, HEAD_DIM), lambda b, i: (b, 0, i, 0)),
                   pl.BlockSpec((bn, N_KV_HEADS, tq, LANE), lambda b, i: (b, 0, i, 0))],
        out_shape=[jax.ShapeDtypeStruct((n, N_Q_HEADS, t, HEAD_DIM), F32),
                   jax.ShapeDtypeStruct((n, N_KV_HEADS, t, LANE), F32)],
        compiler_params=_cparams("arbitrary", "arbitrary"), name="cmp_select",
    )(q, kvc, cover)


SOFTMAX_ROWS = 16


def _attn_prompt_body(*refs, tq, t, selected):
    if selected:
        q_ref, k_ref, v_ref, sel_ref, e_ref, o_ref, s_ref, bias_ref, p_ref, inv_ref = refs
    else:
        q_ref, k_ref, v_ref, o_ref, s_ref, bias_ref, p_ref, inv_ref = refs
    qt = pl.program_id(2)
    rows = GQA * tq
    q = q_ref[0].reshape(rows, HEAD_DIM)

    def attend(start, nk):
        k = k_ref[0, 0, pl.ds(start, nk), :]
        v = v_ref[0, 0, pl.ds(start, nk), :]
        s_ref[:, 0:nk] = _dot_nt(q, k)
        qpos = qt * tq + lax.broadcasted_iota(jnp.int32, (tq, nk), 0)
        kpos = start + lax.broadcasted_iota(jnp.int32, (tq, nk), 1)
        ok = kpos <= qpos
        if selected:
            ok = ok & (_dot(sel_ref[0, 0].astype(BF16), e_ref[:, 0:nk]) > 0.5)
        else:
            ok = ok & (kpos > qpos - WINDOW)
        bias_ref[:, 0:nk] = jnp.where(ok, 0.0, NEG_INF)

        for r0 in range(0, rows, SOFTMAX_ROWS):
            b0 = r0 % tq
            sb = s_ref[r0:r0 + SOFTMAX_ROWS, 0:nk] * (SCALE * LOG2_E) + bias_ref[b0:b0 + SOFTMAX_ROWS, 0:nk]
            m = jnp.max(sb, axis=-1, keepdims=True)
            e = jnp.exp2(sb - m)
            p_ref[r0:r0 + SOFTMAX_ROWS, 0:nk] = e.astype(BF16)
            total = jnp.maximum(jnp.sum(e, axis=-1, keepdims=True), TINY)
            inv_ref[r0:r0 + SOFTMAX_ROWS, :] = jnp.broadcast_to(jnp.where(m > 0.5 * NEG_INF, 1.0 / total, 0.0),
                                                                (SOFTMAX_ROWS, HEAD_DIM))
        o_ref[0] = (_dot(p_ref[:, 0:nk], v) * inv_ref[...]).reshape(GQA, tq, HEAD_DIM)

    if selected:
        n_bucket = t // WINDOW
        per = WINDOW // tq
        for b in range(n_bucket):
            @pl.when(qt // per == b)
            def _(b=b):
                attend(0, (b + 1) * WINDOW)
    else:
        attend(pl.multiple_of(jnp.maximum(qt * tq - WINDOW, 0), tq), WINDOW + tq)


def attn_prompt(q, kv, sel, expand, tq=128):
    n, _, t, _ = q.shape
    selected = sel is not None
    in_specs = [pl.BlockSpec((1, GQA, tq, HEAD_DIM), lambda b, j, i: (b, j, i, 0)),
                pl.BlockSpec((1, 1, t, HEAD_DIM), lambda b, j, i: (b, j, 0, 0)),
                pl.BlockSpec((1, 1, t, HEAD_DIM), lambda b, j, i: (b, N_KV_HEADS + j, 0, 0))]
    args = [q, kv, kv]
    if selected:
        in_specs += [pl.BlockSpec((1, 1, tq, LANE), lambda b, j, i: (b, j, i, 0)),
                     pl.BlockSpec(expand.shape, lambda b, j, i: (0, 0))]
        args += [sel, expand]
    nk_max = t if selected else WINDOW + tq
    return pl.pallas_call(
        functools.partial(_attn_prompt_body, tq=tq, t=t, selected=selected), grid=(n, N_KV_HEADS, t // tq),
        in_specs=in_specs,
        out_specs=pl.BlockSpec((1, GQA, tq, HEAD_DIM), lambda b, j, i: (b, j, i, 0)),
        out_shape=jax.ShapeDtypeStruct((n, N_Q_HEADS, t, HEAD_DIM), F32),
        scratch_shapes=[pltpu.VMEM((GQA * tq, nk_max), F32), pltpu.VMEM((tq, nk_max), F32),
                        pltpu.VMEM((GQA * tq, nk_max), BF16), pltpu.VMEM((GQA * tq, HEAD_DIM), F32)],
        compiler_params=_cparams("arbitrary", "arbitrary", "arbitrary"),
        name="attn_prompt_sel" if selected else "attn_prompt_win",
    )(*args)


T_PAD = 8


def _pad_keys(x):
    return jnp.concatenate([x, jnp.zeros((PAGE_SIZE - x.shape[0], x.shape[1]), x.dtype)], axis=0)


def _attn_sample_sel_body(pt_ref, *refs, t_real):
    del pt_ref
    pages = refs[:N_PAGES]
    q_ref, sel_ref, new_ref, e_ref, o_ref = refs[N_PAGES:]
    rows = GQA * T_PAD
    tok = lax.broadcasted_iota(jnp.int32, (rows, PAGE_SIZE), 0) & (T_PAD - 1)
    col = lax.broadcasted_iota(jnp.int32, (rows, PAGE_SIZE), 1)
    for j in range(N_KV_HEADS):
        kc0, vc0 = j * HEAD_DIM, KV_DIM + j * HEAD_DIM
        q = q_ref[0, j * GQA:(j + 1) * GQA].reshape(rows, HEAD_DIM).astype(BF16)
        k_rows = pl.ds(j, PAGE_SIZE, stride=N_SLOT)
        v_rows = pl.ds(N_KV_HEADS + j, PAGE_SIZE, stride=N_SLOT)
        parts = [_dot_nt(q, pages[p][k_rows, :].astype(BF16)) for p in range(N_PAGES)]
        parts.append(_dot_nt(q, _pad_keys(new_ref[0, :, kc0:kc0 + HEAD_DIM]).astype(BF16)))
        s = jnp.concatenate(parts, axis=1) * SCALE
        sel = sel_ref[0, j]
        sel4 = jnp.concatenate([sel] * GQA, axis=0)
        picked = _dot(sel4.astype(BF16), e_ref[...])
        new_blk = PAST_LEN // L_SEL
        new_ok = (col < t_real) & (col <= tok)
        new_picked = jnp.where(new_ok, _lane_pick(sel4, new_blk), 0.0)
        mask = jnp.concatenate([picked, new_picked], axis=1) > 0.5
        p = _masked_softmax(s, mask).astype(BF16)
        o = _dot(p[:, PAST_LEN:], _pad_keys(new_ref[0, :, vc0:vc0 + HEAD_DIM]).astype(BF16))
        for pg in range(N_PAGES):
            o = o + _dot(p[:, pg * PAGE_SIZE:(pg + 1) * PAGE_SIZE], pages[pg][v_rows, :].astype(BF16))
        o_ref[0, j * GQA:(j + 1) * GQA] = o.reshape(GQA, T_PAD, HEAD_DIM)


def attn_sample_sel(pool, page_table, q, sel, new_rows, expand, t_real):
    n = page_table.shape[0]
    cols = new_rows.shape[-1]
    page_spec = lambda p: pl.BlockSpec((PAGE_SIZE * N_SLOT, HEAD_DIM), lambda i, pt, p=p: (pt[i, p], 0))
    return pl.pallas_call(
        functools.partial(_attn_sample_sel_body, t_real=t_real),
        grid_spec=pltpu.PrefetchScalarGridSpec(
            num_scalar_prefetch=1, grid=(n,),
            in_specs=[page_spec(p) for p in range(N_PAGES)]
            + [pl.BlockSpec((1, N_Q_HEADS, T_PAD, HEAD_DIM), lambda i, pt: (i, 0, 0, 0)),
               pl.BlockSpec((1, N_KV_HEADS, T_PAD, LANE), lambda i, pt: (i, 0, 0, 0)),
               pl.BlockSpec((1, T_PAD, cols), lambda i, pt: (i, 0, 0)),
               pl.BlockSpec(expand.shape, lambda i, pt: (0, 0))],
            out_specs=pl.BlockSpec((1, N_Q_HEADS, T_PAD, HEAD_DIM), lambda i, pt: (i, 0, 0, 0))),
        out_shape=jax.ShapeDtypeStruct((n, N_Q_HEADS, T_PAD, HEAD_DIM), F32),
        compiler_params=_cparams("arbitrary"), name="attn_sample_sel",
    )(page_table, *([pool] * N_PAGES), q, sel, new_rows, expand)


def _attn_sample_win_body(win_ref, q_ref, new_ref, newrows_ref, o_ref, wout_ref, *, t_real):
    rows = GQA * T_PAD
    w_buf = win_ref.shape[0] // N_SLOT
    tok_o = lax.broadcasted_iota(jnp.int32, (rows, w_buf), 0) & (T_PAD - 1)
    col_o = lax.broadcasted_iota(jnp.int32, (rows, w_buf), 1)
    tok_n = lax.broadcasted_iota(jnp.int32, (rows, PAGE_SIZE), 0) & (T_PAD - 1)
    col_n = lax.broadcasted_iota(jnp.int32, (rows, PAGE_SIZE), 1)
    old_ok = jnp.where(col_o + (WINDOW - w_buf) > tok_o, 1.0, 0.0)
    new_ok = jnp.where((col_n < t_real) & (col_n <= tok_n), 1.0, 0.0)
    mask = jnp.concatenate([old_ok, new_ok], axis=1) > 0.5
    for j in range(N_KV_HEADS):
        kc0, vc0 = j * HEAD_DIM, KV_DIM + j * HEAD_DIM
        q = q_ref[0, j * GQA:(j + 1) * GQA].reshape(rows, HEAD_DIM).astype(BF16)
        k_new = _pad_keys(new_ref[0, :, kc0:kc0 + HEAD_DIM]).astype(BF16)
        v_new = _pad_keys(new_ref[0, :, vc0:vc0 + HEAD_DIM]).astype(BF16)
        k_old = win_ref[pl.ds(j, w_buf, stride=N_SLOT), :].astype(BF16)
        v_old = win_ref[pl.ds(N_KV_HEADS + j, w_buf, stride=N_SLOT), :].astype(BF16)
        s = jnp.concatenate([_dot_nt(q, k_old), _dot_nt(q, k_new)], axis=1) * SCALE
        p = _masked_softmax(s, mask).astype(BF16)
        o = _dot(p[:, :w_buf], v_old) + _dot(p[:, w_buf:], v_new)
        o_ref[0, j * GQA:(j + 1) * GQA] = o.reshape(GQA, T_PAD, HEAD_DIM)
    keep = (w_buf - t_real) * N_SLOT
    wout_ref[0:keep, :] = win_ref[t_real * N_SLOT:, :]
    wout_ref[keep:, :] = newrows_ref[...]


def attn_sample_win(win, q, new_rows, new_cache_rows, t_real):
    n = q.shape[0]
    cols = new_rows.shape[-1]
    buf_rows = win.shape[0] // n
    return pl.pallas_call(
        functools.partial(_attn_sample_win_body, t_real=t_real), grid=(n,),
        in_specs=[pl.BlockSpec((buf_rows, HEAD_DIM), lambda i: (i, 0)),
                  pl.BlockSpec((1, N_Q_HEADS, T_PAD, HEAD_DIM), lambda i: (i, 0, 0, 0)),
                  pl.BlockSpec((1, T_PAD, cols), lambda i: (i, 0, 0)),
                  pl.BlockSpec((t_real * N_SLOT, HEAD_DIM), lambda i: (i, 0))],
        out_specs=[pl.BlockSpec((1, N_Q_HEADS, T_PAD, HEAD_DIM), lambda i: (i, 0, 0, 0)),
                   pl.BlockSpec((buf_rows, HEAD_DIM), lambda i: (i, 0))],
        out_shape=[jax.ShapeDtypeStruct((n, N_Q_HEADS, T_PAD, HEAD_DIM), F32),
                   jax.ShapeDtypeStruct(win.shape, F32)],
        compiler_params=_cparams("arbitrary"), name="attn_sample_win",
    )(win, q, new_rows, new_cache_rows)


def _combine_body(oc_ref, os_ref, ow_ref, gl_ref, g_ref, out_ref, *, rows):
    gates = jax.nn.sigmoid(gl_ref[...])
    outs = []
    sq = jnp.zeros((rows, 1), F32)
    for h in range(N_Q_HEADS):
        a = (_lane_pick(gates, h) * oc_ref[:, h].reshape(rows, HEAD_DIM)
             + _lane_pick(gates, N_Q_HEADS + h) * os_ref[:, h].reshape(rows, HEAD_DIM)
             + _lane_pick(gates, 2 * N_Q_HEADS + h) * ow_ref[:, h].reshape(rows, HEAD_DIM))
        outs.append(a)
        sq = sq + jnp.sum(a * a, axis=-1, keepdims=True)
    inv = lax.rsqrt(sq * (1.0 / ATTN_DIM) + RMS_EPS)
    for h in range(N_Q_HEADS):
        sl = slice(h * HEAD_DIM, (h + 1) * HEAD_DIM)
        out_ref[:, sl] = ((outs[h] * inv) * g_ref[:, sl]).astype(out_ref.dtype)


def combine(o_cmp, o_sel, o_win, gate_logits, gain, bn, tt):
    n, _, t, _ = o_cmp.shape
    nt = t // tt
    rows = bn * tt
    o_spec = pl.BlockSpec((bn, N_Q_HEADS, tt, HEAD_DIM), lambda b, i: (b, 0, i, 0))
    return pl.pallas_call(
        functools.partial(_combine_body, rows=rows), grid=(n // bn, nt),
        in_specs=[o_spec, o_spec, o_spec,
                  pl.BlockSpec((rows, LANE), lambda b, i: (b * nt + i, 0)),
                  pl.BlockSpec((1, ATTN_DIM), lambda b, i: (0, 0))],
        out_specs=pl.BlockSpec((rows, ATTN_DIM), lambda b, i: (b * nt + i, 0)),
        out_shape=jax.ShapeDtypeStruct((n * t, ATTN_DIM), BF16),
        compiler_params=_cparams("arbitrary", "arbitrary"), name="combine",
    )(o_cmp, o_sel, o_win, gate_logits, gain.reshape(1, ATTN_DIM))


TOKEN_CHUNKS = D_MODEL // LANE
TOKEN_PITCH = 40


def _router_body(h_ref, g_ref, w_ref, b_ref, xrows_ref, ids_ref, wts_ref):
    tm = h_ref.shape[0]
    h = h_ref[...]
    x = (h * lax.rsqrt(jnp.mean(h * h, axis=-1, keepdims=True) + RMS_EPS)) * g_ref[...]
    for c in range(TOKEN_PITCH):
        piece = x[:, c * LANE:(c + 1) * LANE] if c < TOKEN_CHUNKS else jnp.zeros((tm, LANE), F32)
        xrows_ref[pl.ds(c, tm, stride=TOKEN_PITCH), :] = piece
    logits = _dot(x.astype(BF16), w_ref[...].astype(BF16)) + b_ref[...]
    lane = lax.broadcasted_iota(jnp.int32, logits.shape, 1)
    is_grp = (lane >= N_EXPERTS) & (lane < N_EXPERTS + N_GROUPS)
    gl = jnp.where(is_grp, logits, NEG_INF)
    ge = jnp.where(is_grp, jnp.exp(gl - jnp.max(gl, axis=-1, keepdims=True)), 0.0)
    p_grp = ge / jnp.sum(ge, axis=-1, keepdims=True)
    g_val = jnp.max(p_grp, axis=-1, keepdims=True)
    g_idx = jnp.min(jnp.where(is_grp & (p_grp == g_val), lane, 2 * LANE), axis=-1, keepdims=True) - N_EXPERTS
    lo = g_idx * EXPERTS_PER_GROUP
    in_grp = (lane >= lo) & (lane < lo + EXPERTS_PER_GROUP)
    el = jnp.where(in_grp, logits, NEG_INF)
    ee = jnp.where(in_grp, jnp.exp(el - jnp.max(el, axis=-1, keepdims=True)), 0.0)
    p_e = ee / jnp.sum(ee, axis=-1, keepdims=True)
    cand = jnp.where(in_grp, p_e, -1.0)
    e1 = jnp.max(cand, axis=-1, keepdims=True)
    i1 = jnp.min(jnp.where(cand == e1, lane, 2 * LANE), axis=-1, keepdims=True)
    cand = jnp.where(lane == i1, -1.0, cand)
    e2 = jnp.max(cand, axis=-1, keepdims=True)
    i2 = jnp.min(jnp.where(cand == e2, lane, 2 * LANE), axis=-1, keepdims=True)
    tot = e1 + e2
    ids_ref[...] = jnp.where(lane == 0, i1, jnp.where(lane == 1, i2, 0))
    wts_ref[...] = jnp.where(lane == 0, g_val * e1 / tot, jnp.where(lane == 1, g_val * e2 / tot, 0.0))


def router(h, gain, w_router, b_router, tm=256):
    m, d = h.shape
    out_spec = pl.BlockSpec((tm, LANE), lambda i: (i, 0))
    return pl.pallas_call(
        _router_body, grid=(m // tm,),
        in_specs=[pl.BlockSpec((tm, d), lambda i: (i, 0)),
                  pl.BlockSpec((1, d), lambda i: (0, 0)),
                  pl.BlockSpec((d, LANE), lambda i: (0, 0)),
                  pl.BlockSpec((1, LANE), lambda i: (0, 0))],
        out_specs=[pl.BlockSpec((tm * TOKEN_PITCH, LANE), lambda i: (i, 0)), out_spec, out_spec],
        out_shape=[jax.ShapeDtypeStruct((m * TOKEN_PITCH, LANE), F32),
                   jax.ShapeDtypeStruct((m, LANE), jnp.int32), jax.ShapeDtypeStruct((m, LANE), F32)],
        compiler_params=_cparams("arbitrary"), name="router",
    )(h, gain.reshape(1, d), w_router, b_router)


def _gather_rows(idx_ref, n_rows, src_hbm, dst, sem, wait):
    def copy(r, src):
        return pltpu.make_async_copy(src_hbm.at[pl.ds(src, 1), :], dst.at[pl.ds(r, 1), :], sem)

    if wait:
        def body(r, c):
            copy(r, 0).wait()
            return c
        lax.fori_loop(0, n_rows, body, 0, unroll=8)
    else:
        for r in range(n_rows):
            copy(r, idx_ref[0, 0, r]).start()


def _gather_tokens(idx_ref, n_tok, src_hbm, dst, base, sem, wait):
    def copy(r, tok):
        return pltpu.make_async_copy(src_hbm.at[pl.ds(pl.multiple_of(tok * TOKEN_PITCH, 8), TOKEN_CHUNKS), :],
                                     dst.at[pl.ds(pl.multiple_of(base + r * TOKEN_PITCH, 8), TOKEN_CHUNKS), :], sem)

    if wait:
        def body(r, c):
            copy(r, 0).wait()
            return c
        lax.fori_loop(0, n_tok, body, 0, unroll=8)
    else:
        for r in range(n_tok):
            copy(r, idx_ref[0, 0, r]).start()


def _moe_ffn_body(te_ref, cur_ref, nxt_ref, x_hbm, rw_ref, wg_ref, wu_ref, wd_ref, ys_ref, xbuf, sem, *, tm):
    del te_ref
    i = pl.program_id(0)
    n = pl.num_programs(0)
    slot = lax.rem(i, 2)
    half = tm * TOKEN_PITCH
    here, other = slot * half, (1 - slot) * half

    @pl.when(i == 0)
    def _():
        _gather_tokens(cur_ref, tm, x_hbm, xbuf, 0, sem.at[0], wait=False)

    _gather_tokens(cur_ref, tm, x_hbm, xbuf, here, sem.at[slot], wait=True)
    _gather_tokens(nxt_ref, tm, x_hbm, xbuf, other, sem.at[1 - slot], wait=False)
    x = jnp.concatenate([xbuf[pl.ds(here + c, tm, stride=TOKEN_PITCH), :].astype(BF16) for c in range(TOKEN_CHUNKS)],
                        axis=1)
    hid = jax.nn.silu(_dot(x, wg_ref[0])) * _dot(x, wu_ref[0])
    gate = jnp.concatenate([rw_ref[...]] * (hid.shape[1] // LANE), axis=1)
    ys_ref[...] = _dot((hid * gate).astype(BF16), wd_ref[0])

    @pl.when(i == n - 1)
    def _():
        _gather_tokens(cur_ref, tm, x_hbm, xbuf, other, sem.at[1 - slot], wait=True)


def moe_ffn(x_rows, tile_expert, row_token, row_weight, w_gate, w_up, w_down, tm):
    n_tiles = tile_expert.shape[0]
    d = TOKEN_CHUNKS * LANE
    f = w_gate.shape[-1]
    smem_rows = lambda fn: pl.BlockSpec((1, 1, tm), fn, memory_space=pltpu.SMEM)
    return pl.pallas_call(
        functools.partial(_moe_ffn_body, tm=tm),
        grid_spec=pltpu.PrefetchScalarGridSpec(
            num_scalar_prefetch=1, grid=(n_tiles,),
            in_specs=[smem_rows(lambda i, te: (i, 0, 0)),
                      smem_rows(lambda i, te: (jnp.minimum(i + 1, n_tiles - 1), 0, 0)),
                      pl.BlockSpec(memory_space=pl.ANY),
                      pl.BlockSpec((tm, LANE), lambda i, te: (i, 0)),
                      pl.BlockSpec((1, d, f), lambda i, te: (te[i], 0, 0)),
                      pl.BlockSpec((1, d, f), lambda i, te: (te[i], 0, 0)),
                      pl.BlockSpec((1, f, d), lambda i, te: (te[i], 0, 0))],
            out_specs=pl.BlockSpec((tm, d), lambda i, te: (i, 0)),
            scratch_shapes=[pltpu.VMEM((2 * tm * TOKEN_PITCH, LANE), F32), pltpu.SemaphoreType.DMA((2,))]),
        out_shape=jax.ShapeDtypeStruct((n_tiles * tm, d), F32),
        compiler_params=_cparams("arbitrary"), name="moe_ffn",
    )(tile_expert, row_token, row_token, x_rows, row_weight, w_gate, w_up, w_down)


def _moe_combine_body(cur_ref, nxt_ref, ys_hbm, h_ref, ya_ref, yb_ref, buf, sem, *, tiles_a):
    i = pl.program_id(0)
    n = pl.num_programs(0)
    rows = buf.shape[1]
    tm = h_ref.shape[0]
    slot = lax.rem(i, 2)

    @pl.when(i == 0)
    def _():
        _gather_rows(cur_ref, rows, ys_hbm, buf.at[0], sem.at[0], wait=False)

    _gather_rows(cur_ref, rows, ys_hbm, buf.at[slot], sem.at[slot], wait=True)
    _gather_rows(nxt_ref, rows, ys_hbm, buf.at[1 - slot], sem.at[1 - slot], wait=False)
    y = h_ref[...] + buf[slot, 0:tm, :] + buf[slot, tm:rows, :]

    @pl.when(i == n - 1)
    def _():
        _gather_rows(cur_ref, rows, ys_hbm, buf.at[1 - slot], sem.at[1 - slot], wait=True)

    @pl.when(i < tiles_a)
    def _():
        ya_ref[...] = y

    @pl.when(i >= tiles_a)
    def _():
        yb_ref[...] = y


def moe_combine(ys, pair_row, h, rows_a, tm=128):
    m, d = h.shape
    n_tiles = m // tm
    tiles_a = rows_a // tm
    assert rows_a % tm == 0 and 0 < tiles_a < n_tiles
    smem_rows = lambda fn: pl.BlockSpec((1, 1, 2 * tm), fn, memory_space=pltpu.SMEM)
    return pl.pallas_call(
        functools.partial(_moe_combine_body, tiles_a=tiles_a), grid=(n_tiles,),
        in_specs=[smem_rows(lambda i: (i, 0, 0)),
                  smem_rows(lambda i: (jnp.minimum(i + 1, n_tiles - 1), 0, 0)),
                  pl.BlockSpec(memory_space=pl.ANY),
                  pl.BlockSpec((tm, d), lambda i: (i, 0))],
        out_specs=[pl.BlockSpec((tm, d), lambda i: (jnp.minimum(i, tiles_a - 1), 0)),
                   pl.BlockSpec((tm, d), lambda i: (jnp.maximum(i - tiles_a, 0), 0))],
        out_shape=[jax.ShapeDtypeStruct((rows_a, d), F32), jax.ShapeDtypeStruct((m - rows_a, d), F32)],
        scratch_shapes=[pltpu.VMEM((2, 2 * tm, d), F32), pltpu.SemaphoreType.DMA((2,))],
        compiler_params=_cparams("arbitrary"), name="moe_combine",
    )(pair_row, pair_row, ys, h)


def moe_routed(xt, ids, wts, h, w_gate, w_up, w_down, tm, rows_a):
    m = h.shape[0]
    n_pair = 2 * m
    n_tiles = (n_pair + N_EXPERTS * (tm - 1)) // tm + 1
    flat_e = ids[:, :2].reshape(n_pair)
    flat_w = wts[:, :2].reshape(n_pair)
    order = jnp.argsort(flat_e, stable=True).astype(jnp.int32)
    rank = jnp.argsort(order).astype(jnp.int32)
    counts = jnp.sum(flat_e[:, None] == jnp.arange(N_EXPERTS, dtype=jnp.int32)[None, :], axis=0, dtype=jnp.int32)
    padded = ((counts + tm - 1) // tm) * tm
    pad_end = jnp.cumsum(padded)
    pad_start = pad_end - padded
    start = jnp.cumsum(counts) - counts
    tile_start = jnp.arange(n_tiles, dtype=jnp.int32) * tm
    tile_expert = jnp.minimum(jnp.sum(tile_start[:, None] >= pad_end[None, :], axis=1), N_EXPERTS - 1).astype(jnp.int32)
    row_e = jnp.repeat(tile_expert, tm)
    offs = jnp.arange(n_tiles * tm, dtype=jnp.int32) - pad_start[row_e]
    used = offs < counts[row_e]
    src = order[jnp.clip(start[row_e] + offs, 0, n_pair - 1)]
    row_token = jnp.where(used, src // 2, 0)
    row_weight = jnp.where(used, flat_w[src], 0.0)
    pair_row = (pad_start[flat_e] + rank - start[flat_e]).reshape(m, 2)
    ys = moe_ffn(xt, tile_expert, row_token.reshape(n_tiles, 1, tm),
                 jnp.broadcast_to(row_weight[:, None], (n_tiles * tm, LANE)), w_gate, w_up, w_down, tm)
    tc = 128
    pair_tiles = pair_row.reshape(m // tc, tc, 2).transpose(0, 2, 1).reshape(m // tc, 1, 2 * tc)
    return moe_combine(ys, pair_tiles, h, rows_a, tm=tc)


def _cover_matrix(nsb):
    i = np.arange(N_CMP_BLK)[:, None]
    j = np.arange(LANE)[None, :]
    m = np.zeros((N_CMP_BLK, LANE), np.float32)
    for a in range(L_SEL // CMP_STRIDE):
        for c in range(R_CMP):
            m += (i == (L_SEL // CMP_STRIDE) * j + a - c)
    m[N_CMP_BLK - 1:, :] = 0.0
    m[:, nsb:] = 0.0
    return jnp.asarray(m, BF16)


def _expand_matrix(n_keys):
    b = np.arange(LANE)[:, None]
    k = np.arange(n_keys)[None, :]
    return jnp.asarray((k // L_SEL == b).astype(np.float32), BF16)


def _to_heads(q, n, t):
    return q.reshape(n, t, N_Q_HEADS, HEAD_DIM).transpose(0, 2, 1, 3)


def kernel(x_prompt, x_sample, cache_cmp_kv, cache_sel_kv, state_win_kv, state_conv, page_table, norm_mix_g, w_in,
           conv_w, q_norm_g, k_norm_g, phi_pe, phi_w1, phi_w2, out_norm_g, w_out, norm_ffn_g, w_group_router,
           b_group_router, w_expert_router, b_expert_router, w_gate, w_up, w_down):
    n_p, t_p, d = x_prompt.shape
    n_s, t_s, _ = x_sample.shape
    assert w_in.shape[0] == 1 and t_s < CMP_STRIDE and t_s <= T_PAD and t_p % WINDOW == 0
    kv_cols = 2 * KV_DIM
    w_in_t = jnp.transpose(w_in[0])
    w_in_gate_t = jnp.pad(w_in_t[Z_MAIN:], ((0, LANE - 3 * N_Q_HEADS), (0, 0)))
    w_o = w_out
    wkv = phi_w1[0].reshape(2, R_CMP, CMP_STRIDE, HEAD_DIM, HEAD_DIM).transpose(2, 0, 3, 1, 4)
    wkv = wkv.reshape(CMP_STRIDE * 2 * HEAD_DIM, R_CMP * HEAD_DIM)
    pe5 = phi_pe[0].reshape(2, R_CMP, CMP_STRIDE, 1, HEAD_DIM)
    pe_kv = (pe5 * jnp.eye(2, dtype=F32).reshape(2, 1, 1, 2, 1)).reshape(2, R_CMP, CMP_STRIDE * 2 * HEAD_DIM)
    w2cat = phi_w2[0].reshape(2 * HEAD_DIM, HEAD_DIM)
    w_router = jnp.pad(jnp.concatenate([w_expert_router[0], w_group_router[0]], axis=1),
                       ((0, 0), (0, LANE - N_EXPERTS - N_GROUPS)))
    b_router = jnp.pad(jnp.concatenate([b_expert_router[0], b_group_router[0]]),
                       (0, LANE - N_EXPERTS - N_GROUPS)).reshape(1, LANE)
    wg_bf, wu_bf, wd_bf = w_gate[0].astype(BF16), w_up[0].astype(BF16), w_down[0].astype(BF16)

    def project(x2d):
        xn = rmsnorm_cast(x2d, norm_mix_g[0])
        z = matmul([([xn], None)], [(w_in_t, 0)], tm=1024, tn=512, w_transposed=True)
        gate_logits = matmul([([xn], None)], [(w_in_gate_t, 0)], tn=LANE, w_transposed=True)
        return z, gate_logits

    m_p = n_p * t_p

    xp = x_prompt.reshape(n_p * t_p, d)
    z, glog = project(xp)
    qh, ks_h, kw_h, kvc, kvs, kvw = postproj_prompt(z, n_p, t_p, q_norm_g[0], k_norm_g[0])
    conv_out, conv_last = conv_prompt(z, n_p, t_p, conv_w[0], out_norm_g[0][:CONV_DIM])
    ident = jnp.arange(n_p * (t_p // PAGE_SIZE), dtype=jnp.int32).reshape(n_p, t_p // PAGE_SIZE)
    kv_cmp = compress(kvc, ident, wkv, pe_kv, w2cat, k_norm_g[0])
    o_cmp, sel = cmp_select(qh, kv_cmp, _cover_matrix(t_p // L_SEL), bn=1, tq=256, pos_base=0)
    o_sel = attn_prompt(qh, ks_h, sel, _expand_matrix(t_p))
    o_win = attn_prompt(qh, kw_h, None, None)
    attn_out = combine(o_cmp, o_sel, o_win, glog, out_norm_g[0][CONV_DIM:], bn=1, tt=256)
    mixed_p = ([conv_out, attn_out], xp)
    kv_shape = (1, n_p, t_p, 2, N_KV_HEADS, HEAD_DIM)
    w_keep = min(WINDOW, t_p)
    prompt_win = kvw.reshape(kv_shape)[:, :, t_p - w_keep:]
    prompt_conv = conv_last[:, 8 - (CONV_WIDTH - 1):, :][None]

    xs = x_sample.reshape(n_s * t_s, d)
    z, glog = project(xs)
    q, kvc_s, kvs_s, kvw_s = postproj(z, q_norm_g[0], k_norm_g[0])
    conv_out, conv_state = conv_sample(z, n_s, t_s, state_conv[0], conv_w[0], out_norm_g[0][:CONV_DIM])
    pad_t = lambda a: jnp.pad(a, ((0, 0), (0, T_PAD - t_s), (0, 0)))
    qh = jnp.pad(_to_heads(q, n_s, t_s), ((0, 0), (0, 0), (0, T_PAD - t_s), (0, 0)))
    pool_cmp = cache_cmp_kv[0].reshape(-1, HEAD_DIM)
    pool_sel = cache_sel_kv[0].reshape(-1, HEAD_DIM)
    kv_cmp = compress(pool_cmp, page_table, wkv, pe_kv, w2cat, k_norm_g[0])
    o_cmp, sel = cmp_select(qh, kv_cmp, _cover_matrix(PAST_LEN // L_SEL + 1), bn=16, tq=T_PAD, pos_base=PAST_LEN)
    o_sel = attn_sample_sel(pool_sel, page_table, qh, sel, pad_t(kvs_s.reshape(n_s, t_s, kv_cols)),
                            _expand_matrix(PAST_LEN), t_s)
    win = state_win_kv[0].reshape(-1, HEAD_DIM)
    o_win, win_new = attn_sample_win(win, qh, pad_t(kvw_s.reshape(n_s, t_s, kv_cols)),
                                     kvw_s.reshape(-1, HEAD_DIM), t_s)
    glog_pad = pad_t(glog.reshape(n_s, t_s, LANE)).reshape(n_s * T_PAD, LANE)
    attn_out = combine(o_cmp, o_sel, o_win, glog_pad, out_norm_g[0][CONV_DIM:], bn=32, tt=T_PAD)
    attn_out = attn_out.reshape(n_s, T_PAD, ATTN_DIM)[:, :t_s].reshape(n_s * t_s, ATTN_DIM)
    s_shape = (1, n_s, t_s, 2, N_KV_HEADS, HEAD_DIM)

    h_all = matmul([mixed_p, ([conv_out, attn_out], xs)], [(w_o, 0), (w_o, 1)])
    x_rows, ids, wts = router(h_all, norm_ffn_g[0], w_router, b_router)
    y_prompt, y_sample = moe_routed(x_rows, ids, wts, h_all, wg_bf, wu_bf, wd_bf, 256, m_p)

    return (y_prompt.reshape(n_p, t_p, d), y_sample.reshape(n_s, t_s, d), kvc.reshape(kv_shape), kvs.reshape(kv_shape), prompt_win, prompt_conv,
            kvc_s.reshape(s_shape), kvs_s.reshape(s_shape),
            win_new.reshape(1, n_s, -1, 2, N_KV_HEADS, HEAD_DIM), conv_state[None])
```

```python
import functools

import numpy as np
import jax
import jax.numpy as jnp
from jax import lax
from jax.experimental import pallas as pl
from jax.experimental.pallas import tpu as pltpu

F32 = jnp.float32
BF16 = jnp.bfloat16

D_MODEL = 4096
PAST_LEN = 2048
PAGE_SIZE = 128
HEAD_DIM = 128
CONV_DIM = 2048
N_Q_HEADS = 16
N_KV_HEADS = 4
GQA = 4
ATTN_DIM = 2048
KV_DIM = 512
CONV_WIDTH = 3
L_CMP = 32
CMP_STRIDE = 16
R_CMP = 2
L_SEL = 64
N_SEL = 8
WINDOW = 512
FORCE_BONUS = 1e3
SCALE = HEAD_DIM ** -0.5
N_GROUPS = 4
EXPERTS_PER_GROUP = 4
N_EXPERTS = 16
D_FF_EXPERT = 512
RMS_EPS = 1e-6
NEG_INF = -1e30
TINY = 1e-30
PICKED = -3e38
LOG2_E = 1.4426950408889634

N_PAGES = PAST_LEN // PAGE_SIZE
N_CMP_BLK = 128
Z_MAIN = 3 * CONV_DIM + ATTN_DIM + 6 * KV_DIM
LANE = 128
VMEM_LIMIT = 56 * 1024 * 1024


def _cparams(*sem):
    return pltpu.CompilerParams(dimension_semantics=sem, vmem_limit_bytes=VMEM_LIMIT)


def _masked_softmax(s, mask):
    s = jnp.where(mask, s, NEG_INF)
    m = jnp.max(s, axis=-1, keepdims=True)
    e = jnp.where(mask, jnp.exp(s - m), 0.0)
    return e / jnp.maximum(jnp.sum(e, axis=-1, keepdims=True), TINY)


def _dot_nt(a, b):
    return lax.dot_general(a, b, (((1,), (1,)), ((), ())), preferred_element_type=F32)


def _dot(a, b):
    return jnp.dot(a, b, preferred_element_type=F32)


def _lane_pick(x, c):
    lane = lax.broadcasted_iota(jnp.int32, x.shape, 1)
    return jnp.sum(jnp.where(lane == c, x, 0.0), axis=-1, keepdims=True)


def _rmsnorm_body(x_ref, g_ref, o_ref):
    x = x_ref[...]
    inv = lax.rsqrt(jnp.mean(x * x, axis=-1, keepdims=True) + RMS_EPS)
    o_ref[...] = ((x * inv) * g_ref[...]).astype(o_ref.dtype)


def rmsnorm_cast(x, g, dtype=BF16, tm=256):
    m, d = x.shape
    return pl.pallas_call(
        _rmsnorm_body, grid=(m // tm,),
        in_specs=[pl.BlockSpec((tm, d), lambda i: (i, 0)), pl.BlockSpec((1, d), lambda i: (0, 0))],
        out_specs=pl.BlockSpec((tm, d), lambda i: (i, 0)),
        out_shape=jax.ShapeDtypeStruct((m, d), dtype),
        compiler_params=_cparams("arbitrary"), name="rmsnorm_cast",
    )(x, g.reshape(1, d))


def _matmul_body(*refs, n_w, has_res, tiles, w_transposed):
    per = n_w + (1 if has_res else 0)
    n_groups = len(tiles) - 1
    o_ref = refs[n_w + n_groups * per]
    wbf = refs[n_w + n_groups * per + 1:]
    i = pl.program_id(1)
    dot = _dot_nt if w_transposed else _dot

    @pl.when(i == 0)
    def _():
        for p in range(n_w):
            wbf[p][...] = refs[p][...].astype(BF16)

    for g in range(n_groups):
        grp = refs[n_w + g * per:n_w + (g + 1) * per]

        def compute(grp=grp):
            acc = dot(grp[0][...], wbf[0][...])
            for p in range(1, n_w):
                acc = acc + dot(grp[p][...], wbf[p][...])
            if has_res:
                acc = acc + grp[n_w][...]
            o_ref[...] = acc

        if n_groups == 1:
            compute()
        else:
            pl.when((i >= tiles[g]) & (i < tiles[g + 1]))(compute)


def matmul(groups, weights, tm=512, tn=512, w_transposed=False):
    ms = [g[0][0].shape[0] for g in groups]
    tm = min([tm] + ms)
    assert all(m % tm == 0 for m in ms)
    n = (weights[0][0].shape[0 if w_transposed else -1] // tn) * tn
    has_res = groups[0][1] is not None
    tiles = [0]
    for m in ms:
        tiles.append(tiles[-1] + m // tm)
    in_specs, args, scratch = [], [], []
    ks = [a.shape[1] for a in groups[0][0]]
    for (w, kb), k in zip(weights, ks):
        if w_transposed:
            in_specs.append(pl.BlockSpec((tn, k), lambda j, i, kb=kb: (j, kb)))
        elif w.ndim == 3:
            in_specs.append(pl.BlockSpec((None, k, tn), lambda j, i, kb=kb: (0, kb, j)))
        else:
            in_specs.append(pl.BlockSpec((k, tn), lambda j, i, kb=kb: (kb, j)))
        args.append(w)
        scratch.append(pltpu.VMEM((tn, k) if w_transposed else (k, tn), BF16))
    for g, (a_list, res) in enumerate(groups):
        lo, hi = tiles[g], tiles[g + 1]
        row = lambda i, lo=lo, hi=hi: jnp.clip(i, lo, hi - 1) - lo
        for a, k in zip(a_list, ks):
            in_specs.append(pl.BlockSpec((tm, k), lambda j, i, row=row: (row(i), 0)))
            args.append(a)
        if has_res:
            in_specs.append(pl.BlockSpec((tm, tn), lambda j, i, row=row: (row(i), j)))
            args.append(res)
    return pl.pallas_call(
        functools.partial(_matmul_body, n_w=len(weights), has_res=has_res, tiles=tuple(tiles),
                          w_transposed=w_transposed),
        grid=(n // tn, tiles[-1]), in_specs=in_specs,
        out_specs=pl.BlockSpec((tm, tn), lambda j, i: (i, j)),
        out_shape=jax.ShapeDtypeStruct((sum(ms), n), F32), scratch_shapes=scratch,
        compiler_params=_cparams("arbitrary", "arbitrary"), name="matmul",
    )(*args)


def _head_norm(x, g):
    inv = lax.rsqrt(jnp.mean(x * x, axis=-1, keepdims=True) + RMS_EPS)
    return (x * inv) * g


def _postproj_body(zq_ref, zc_ref, zs_ref, zw_ref, qg_ref, kg_ref, q_ref, kvc_ref, kvs_ref, kvw_ref):
    for h in range(N_Q_HEADS):
        sl = slice(h * HEAD_DIM, (h + 1) * HEAD_DIM)
        q_ref[:, sl] = _head_norm(zq_ref[:, sl], qg_ref[...]).astype(q_ref.dtype)
    kvc_ref[...] = zc_ref[...]
    for h in range(N_KV_HEADS):
        sl = slice(h * HEAD_DIM, (h + 1) * HEAD_DIM)
        kvs_ref[:, sl] = _head_norm(zs_ref[:, sl], kg_ref[1:2, :])
        kvw_ref[:, sl] = _head_norm(zw_ref[:, sl], kg_ref[2:3, :])
    kvs_ref[:, KV_DIM:] = zs_ref[:, KV_DIM:]
    kvw_ref[:, KV_DIM:] = zw_ref[:, KV_DIM:]


def postproj(z, q_norm_g, k_norm_g, tm=256):
    m = z.shape[0]
    kv = 2 * KV_DIM
    q0 = 3 * CONV_DIM // ATTN_DIM
    c0 = (3 * CONV_DIM + ATTN_DIM) // kv
    return pl.pallas_call(
        _postproj_body, grid=(m // tm,),
        in_specs=[pl.BlockSpec((tm, ATTN_DIM), lambda i: (i, q0)),
                  pl.BlockSpec((tm, kv), lambda i: (i, c0)),
                  pl.BlockSpec((tm, kv), lambda i: (i, c0 + 1)),
                  pl.BlockSpec((tm, kv), lambda i: (i, c0 + 2)),
                  pl.BlockSpec((1, HEAD_DIM), lambda i: (0, 0)),
                  pl.BlockSpec((3, HEAD_DIM), lambda i: (0, 0))],
        out_specs=[pl.BlockSpec((tm, ATTN_DIM), lambda i: (i, 0)),
                   pl.BlockSpec((tm, kv), lambda i: (i, 0)),
                   pl.BlockSpec((tm, kv), lambda i: (i, 0)),
                   pl.BlockSpec((tm, kv), lambda i: (i, 0))],
        out_shape=[jax.ShapeDtypeStruct((m, ATTN_DIM), F32)] + [jax.ShapeDtypeStruct((m, kv), F32)] * 3,
        compiler_params=_cparams("arbitrary"), name="postproj",
    )(z, z, z, z, q_norm_g.reshape(1, HEAD_DIM), k_norm_g)


N_SLOT = 2 * N_KV_HEADS


def _postproj_prompt_body(zq_ref, zc_ref, zs_ref, zw_ref, qg_ref, kg_ref,
                          q_ref, ks_ref, kw_ref, kvc_ref, kvs_ref, kvw_ref):
    tm = zq_ref.shape[0]
    for h in range(N_Q_HEADS):
        sl = slice(h * HEAD_DIM, (h + 1) * HEAD_DIM)
        q_ref[0, h] = _head_norm(zq_ref[:, sl], qg_ref[...]).astype(BF16)
    for c in range(N_SLOT):
        sl = slice(c * HEAD_DIM, (c + 1) * HEAD_DIM)
        rows = pl.ds(c, tm, stride=N_SLOT)
        kvc_ref[rows, :] = zc_ref[:, sl]
        xs, xw = zs_ref[:, sl], zw_ref[:, sl]
        if c < N_KV_HEADS:
            xs, xw = _head_norm(xs, kg_ref[1:2, :]), _head_norm(xw, kg_ref[2:3, :])
        kvs_ref[rows, :] = xs
        kvw_ref[rows, :] = xw
        ks_ref[0, c] = xs.astype(BF16)
        kw_ref[0, c] = xw.astype(BF16)


def postproj_prompt(z, n, t, q_norm_g, k_norm_g, tm=256):
    kv = 2 * KV_DIM
    nt = t // tm
    q0 = 3 * CONV_DIM // ATTN_DIM
    c0 = (3 * CONV_DIM + ATTN_DIM) // kv
    rows_spec = pl.BlockSpec((tm * N_SLOT, HEAD_DIM), lambda b, i: (b * nt + i, 0))
    slot_spec = pl.BlockSpec((1, N_SLOT, tm, HEAD_DIM), lambda b, i: (b, 0, i, 0))
    rows_sds = jax.ShapeDtypeStruct((n * t * N_SLOT, HEAD_DIM), F32)
    slot_sds = jax.ShapeDtypeStruct((n, N_SLOT, t, HEAD_DIM), BF16)
    return pl.pallas_call(
        _postproj_prompt_body, grid=(n, nt),
        in_specs=[pl.BlockSpec((tm, ATTN_DIM), lambda b, i: (b * nt + i, q0)),
                  pl.BlockSpec((tm, kv), lambda b, i: (b * nt + i, c0)),
                  pl.BlockSpec((tm, kv), lambda b, i: (b * nt + i, c0 + 1)),
                  pl.BlockSpec((tm, kv), lambda b, i: (b * nt + i, c0 + 2)),
                  pl.BlockSpec((1, HEAD_DIM), lambda b, i: (0, 0)),
                  pl.BlockSpec((3, HEAD_DIM), lambda b, i: (0, 0))],
        out_specs=[pl.BlockSpec((1, N_Q_HEADS, tm, HEAD_DIM), lambda b, i: (b, 0, i, 0)),
                   slot_spec, slot_spec, rows_spec, rows_spec, rows_spec],
        out_shape=[jax.ShapeDtypeStruct((n, N_Q_HEADS, t, HEAD_DIM), BF16), slot_sds, slot_sds,
                   rows_sds, rows_sds, rows_sds],
        compiler_params=_cparams("arbitrary", "arbitrary"), name="postproj_prompt",
    )(z, z, z, z, q_norm_g.reshape(1, HEAD_DIM), k_norm_g)


def _conv_finish(b, y, g):
    c = b * y
    inv = lax.rsqrt(jnp.mean(c * c, axis=-1, keepdims=True) + RMS_EPS)
    return ((c * inv) * g).astype(BF16)


def _conv_prompt_body(gb_ref, gc_ref, hc_ref, w_ref, g_ref, o_ref, st_ref, carry_ref):
    tt = gb_ref.shape[0]

    @pl.when(pl.program_id(1) == 0)
    def _():
        carry_ref[...] = jnp.zeros_like(carry_ref)

    u = gc_ref[...] * hc_ref[...]
    prev = carry_ref[...]
    p1, p2 = prev[7:8, :], prev[6:7, :]
    row = lax.broadcasted_iota(jnp.int32, u.shape, 0)
    u1 = jnp.where(row == 0, p1, pltpu.roll(u, 1, axis=0))
    u2 = jnp.where(row == 0, p2, jnp.where(row == 1, p1, pltpu.roll(u, 2, axis=0)))
    y = u2 * w_ref[0:1, :] + u1 * w_ref[1:2, :] + u * w_ref[2:3, :]
    o_ref[...] = _conv_finish(gb_ref[...], y, g_ref[...])
    last = u[tt - 8:tt, :]
    carry_ref[...] = last
    st_ref[0] = last


def conv_prompt(z, n, t, conv_w, gain, tt=256):
    nt = t // tt
    row = lambda b, i: (b * nt + i, 0)
    return pl.pallas_call(
        _conv_prompt_body, grid=(n, nt),
        in_specs=[pl.BlockSpec((tt, CONV_DIM), lambda b, i: (b * nt + i, 0)),
                  pl.BlockSpec((tt, CONV_DIM), lambda b, i: (b * nt + i, 1)),
                  pl.BlockSpec((tt, CONV_DIM), lambda b, i: (b * nt + i, 2)),
                  pl.BlockSpec((CONV_WIDTH, CONV_DIM), lambda b, i: (0, 0)),
                  pl.BlockSpec((1, CONV_DIM), lambda b, i: (0, 0))],
        out_specs=[pl.BlockSpec((tt, CONV_DIM), row),
                   pl.BlockSpec((1, 8, CONV_DIM), lambda b, i: (b, 0, 0))],
        out_shape=[jax.ShapeDtypeStruct((n * t, CONV_DIM), BF16), jax.ShapeDtypeStruct((n, 8, CONV_DIM), F32)],
        scratch_shapes=[pltpu.VMEM((8, CONV_DIM), F32)],
        compiler_params=_cparams("arbitrary", "arbitrary"), name="conv_prompt",
    )(z, z, z, conv_w, gain.reshape(1, CONV_DIM))


def _conv_sample_body(z_ref, pre_ref, w_ref, g_ref, o_ref, st_ref, *, t):
    up = [pre_ref[k] for k in range(CONV_WIDTH - 1)] + [z_ref[1, k] * z_ref[2, k] for k in range(t)]
    for k in range(t):
        y = up[k] * w_ref[0:1, :] + up[k + 1] * w_ref[1:2, :] + up[k + 2] * w_ref[2:3, :]
        o_ref[k] = _conv_finish(z_ref[0, k], y, g_ref[...])
    for k in range(CONV_WIDTH - 1):
        st_ref[k] = up[t + k]


def conv_sample(z, n, t, state, conv_w, gain):
    zt = z[:, :3 * CONV_DIM].reshape(n, t, 3, CONV_DIM).transpose(2, 1, 0, 3)
    whole = lambda shape: pl.BlockSpec(shape, lambda i: (0,) * len(shape))
    out, st = pl.pallas_call(
        functools.partial(_conv_sample_body, t=t), grid=(1,),
        in_specs=[whole((3, t, n, CONV_DIM)), whole((CONV_WIDTH - 1, n, CONV_DIM)),
                  whole((CONV_WIDTH, CONV_DIM)), whole((1, CONV_DIM))],
        out_specs=[whole((t, n, CONV_DIM)), whole((CONV_WIDTH - 1, n, CONV_DIM))],
        out_shape=[jax.ShapeDtypeStruct((t, n, CONV_DIM), BF16),
                   jax.ShapeDtypeStruct((CONV_WIDTH - 1, n, CONV_DIM), F32)],
        compiler_params=_cparams("arbitrary"), name="conv_sample",
    )(zt, state.transpose(1, 0, 2), conv_w, gain.reshape(1, CONV_DIM))
    return out.transpose(1, 0, 2).reshape(n * t, CONV_DIM), st.transpose(1, 0, 2)


def _compress_body(pt_ref, *refs):
    del pt_ref
    pages = refs[:N_PAGES]
    wkv_ref, pe_ref, w2_ref, kg_ref, out_ref = refs[N_PAGES:]
    cpp = PAGE_SIZE // CMP_STRIDE
    x4 = [pages[p][...].reshape(cpp, CMP_STRIDE, N_SLOT, HEAD_DIM) for p in range(N_PAGES)]
    n_rows = N_PAGES * cpp * N_SLOT
    is_key = (lax.broadcasted_iota(jnp.int32, (n_rows, HEAD_DIM), 0) & (N_SLOT - 1)) < N_KV_HEADS

    def split(x):
        return [jnp.where(is_key, x, 0.0).astype(BF16), jnp.where(is_key, 0.0, x).astype(BF16)]

    pieces = []
    for s in range(CMP_STRIDE):
        pieces += split(jnp.concatenate([x4[p][:, s].reshape(cpp * N_SLOT, HEAD_DIM) for p in range(N_PAGES)], axis=0))
    wkv = wkv_ref[...].astype(BF16)
    r = _dot(jnp.concatenate(pieces, axis=1), wkv)
    hpre = r[:, :HEAD_DIM] + pltpu.roll(r[:, HEAD_DIM:], n_rows - N_SLOT, axis=0)
    bias = []
    for v in range(2):
        b = jnp.zeros((8, HEAD_DIM), F32)
        for rr in range(R_CMP):
            pe = jnp.broadcast_to(pe_ref[v, rr:rr + 1, :], (8, wkv.shape[0])).astype(BF16)
            b = b + _dot(pe, wkv[:, rr * HEAD_DIM:(rr + 1) * HEAD_DIM])
        bias.append(b[0:1, :])
    hid = jax.nn.gelu(hpre + jnp.where(is_key, bias[0], bias[1]))
    out = _dot(jnp.concatenate(split(hid), axis=1), w2_ref[...].astype(BF16))
    out_ref[0] = jnp.where(is_key, _head_norm(out, kg_ref[0:1, :]), out)


def compress(pool, page_table, wkv, pe_kv, w2cat, k_norm_g):
    n = page_table.shape[0]
    page_rows = PAGE_SIZE * N_SLOT
    page_spec = lambda p: pl.BlockSpec((page_rows, HEAD_DIM), lambda i, pt, p=p: (pt[i, p], 0))
    const = lambda shape: pl.BlockSpec(shape, lambda i, pt: (0,) * len(shape))
    return pl.pallas_call(
        _compress_body,
        grid_spec=pltpu.PrefetchScalarGridSpec(
            num_scalar_prefetch=1, grid=(n,),
            in_specs=[page_spec(p) for p in range(N_PAGES)]
            + [const(wkv.shape), const(pe_kv.shape), const(w2cat.shape), const(k_norm_g.shape)],
            out_specs=pl.BlockSpec((1, N_CMP_BLK * N_SLOT, HEAD_DIM), lambda i, pt: (i, 0, 0))),
        out_shape=jax.ShapeDtypeStruct((n, N_CMP_BLK * N_SLOT, HEAD_DIM), F32),
        compiler_params=_cparams("arbitrary"), name="compress",
    )(page_table, *([pool] * N_PAGES), wkv, pe_kv, w2cat, k_norm_g)


SEL_ROWS = 40


def _cmp_select_body(q_ref, kvc_ref, cov_ref, o_ref, sel_ref, *, bn, tq, pos_base):
    qt = pl.program_id(1)
    grp = GQA * tq
    pairs = [(b, j) for b in range(bn) for j in range(N_KV_HEADS)]

    def slot(b, c):
        return kvc_ref[b, pl.ds(c, N_CMP_BLK, stride=N_SLOT), :].astype(BF16)

    s = jnp.concatenate(
        [_dot_nt(q_ref[b, j * GQA:(j + 1) * GQA].reshape(grp, HEAD_DIM).astype(BF16), slot(b, j)) for b, j in pairs],
        axis=0) * SCALE
    rows = len(pairs) * grp
    pos = pos_base + qt * tq + (lax.broadcasted_iota(jnp.int32, (rows, N_CMP_BLK), 0) & (tq - 1))
    blk = lax.broadcasted_iota(jnp.int32, (rows, N_CMP_BLK), 1)
    valid = (blk < N_CMP_BLK - 1) & (blk * CMP_STRIDE + (L_CMP - 1) <= pos)
    p = _masked_softmax(s, valid)
    p_bf = p.astype(BF16)
    for i, (b, j) in enumerate(pairs):
        o_ref[b, j * GQA:(j + 1) * GQA] = _dot(p_bf[i * grp:(i + 1) * grp],
                                               slot(b, N_KV_HEADS + j)).reshape(GQA, tq, HEAD_DIM)
    psum = jnp.sum(p.reshape(len(pairs), GQA, tq, N_CMP_BLK), axis=1).reshape(len(pairs) * tq, N_CMP_BLK)
    p_hi = psum.astype(BF16)
    p_lo = (psum - p_hi.astype(F32)).astype(BF16)
    imp_t = _dot_nt(cov_ref[...], p_hi) + _dot_nt(cov_ref[...], p_lo)
    srows = len(pairs) * tq
    imp_t = imp_t[0:SEL_ROWS, :]
    blk = lax.broadcasted_iota(jnp.int32, (SEL_ROWS, srows), 0)
    cur = (pos_base + qt * tq + (lax.broadcasted_iota(jnp.int32, (SEL_ROWS, srows), 1) & (tq - 1))) >> 6
    forced = (blk == 0) | (blk == cur) | (blk == cur - 1)
    score = jnp.where(blk <= cur, imp_t + jnp.where(forced, FORCE_BONUS, 0.0), NEG_INF)
    sel_t = jnp.zeros((SEL_ROWS, srows), F32)
    for _ in range(N_SEL):
        m = jnp.max(score, axis=0, keepdims=True)
        first = jnp.min(jnp.where(score == m, blk, LANE), axis=0, keepdims=True)
        hit = blk == first
        sel_t = jnp.where(hit, 1.0, sel_t)
        score = jnp.where(hit, PICKED, score)
    sel = jnp.transpose(jnp.concatenate([sel_t, jnp.zeros((LANE - SEL_ROWS, srows), F32)], axis=0))
    for i, (b, j) in enumerate(pairs):
        sel_ref[b, j] = sel[i * tq:(i + 1) * tq]


def cmp_select(q, kvc, cover, bn, tq, pos_base):
    n, _, t, _ = q.shape
    assert L_SEL == 64 and tq & (tq - 1) == 0
    return pl.pallas_call(
        functools.partial(_cmp_select_body, bn=bn, tq=tq, pos_base=pos_base), grid=(n // bn, t // tq),
        in_specs=[pl.BlockSpec((bn, N_Q_HEADS, tq, HEAD_DIM), lambda b, i: (b, 0, i, 0)),
                  pl.BlockSpec((bn, N_CMP_BLK * N_SLOT, HEAD_DIM), lambda b, i: (b, 0, 0)),
                  pl.BlockSpec((N_CMP_BLK, LANE), lambda b, i: (0, 0))],
        out_specs=[pl.BlockSpec((bn, N_Q_HEADS, tq, HEAD_DIM), lambda b, i: (b, 0, i, 0)),
                   pl.BlockSpec((bn, N_KV_HEADS, tq, LANE), lambda b, i: (b, 0, i, 0))],
        out_shape=[jax.ShapeDtypeStruct((n, N_Q_HEADS, t, HEAD_DIM), F32),
                   jax.ShapeDtypeStruct((n, N_KV_HEADS, t, LANE), F32)],
        compiler_params=_cparams("arbitrary", "arbitrary"), name="cmp_select",
    )(q, kvc, cover)


SOFTMAX_ROWS = 16


def _attn_prompt_body(*refs, tq, t, selected):
    if selected:
        q_ref, k_ref, v_ref, sel_ref, e_ref, o_ref, s_ref, bias_ref, p_ref, inv_ref = refs
    else:
        q_ref, k_ref, v_ref, o_ref, s_ref, bias_ref, p_ref, inv_ref = refs
    qt = pl.program_id(2)
    rows = GQA * tq
    q = q_ref[0].reshape(rows, HEAD_DIM)

    def attend(start, nk):
        k = k_ref[0, 0, pl.ds(start, nk), :]
        v = v_ref[0, 0, pl.ds(start, nk), :]
        s_ref[:, 0:nk] = _dot_nt(q, k)
        qpos = qt * tq + lax.broadcasted_iota(jnp.int32, (tq, nk), 0)
        kpos = start + lax.broadcasted_iota(jnp.int32, (tq, nk), 1)
        ok = kpos <= qpos
        if selected:
            ok = ok & (_dot(sel_ref[0, 0].astype(BF16), e_ref[:, 0:nk]) > 0.5)
        else:
            ok = ok & (kpos > qpos - WINDOW)
        bias_ref[:, 0:nk] = jnp.where(ok, 0.0, NEG_INF)

        for r0 in range(0, rows, SOFTMAX_ROWS):
            b0 = r0 % tq
            sb = s_ref[r0:r0 + SOFTMAX_ROWS, 0:nk] * (SCALE * LOG2_E) + bias_ref[b0:b0 + SOFTMAX_ROWS, 0:nk]
            m = jnp.max(sb, axis=-1, keepdims=True)
            e = jnp.exp2(sb - m)
            p_ref[r0:r0 + SOFTMAX_ROWS, 0:nk] = e.astype(BF16)
            total = jnp.maximum(jnp.sum(e, axis=-1, keepdims=True), TINY)
            inv_ref[r0:r0 + SOFTMAX_ROWS, :] = jnp.broadcast_to(jnp.where(m > 0.5 * NEG_INF, 1.0 / total, 0.0),
                                                                (SOFTMAX_ROWS, HEAD_DIM))
        o_ref[0] = (_dot(p_ref[:, 0:nk], v) * inv_ref[...]).reshape(GQA, tq, HEAD_DIM)

    if selected:
        n_bucket = t // WINDOW
        per = WINDOW // tq
        for b in range(n_bucket):
            @pl.when(qt // per == b)
            def _(b=b):
                attend(0, (b + 1) * WINDOW)
    else:
        attend(pl.multiple_of(jnp.maximum(qt * tq - WINDOW, 0), tq), WINDOW + tq)


def attn_prompt(q, kv, sel, expand, tq=128):
    n, _, t, _ = q.shape
    selected = sel is not None
    in_specs = [pl.BlockSpec((1, GQA, tq, HEAD_DIM), lambda b, j, i: (b, j, i, 0)),
                pl.BlockSpec((1, 1, t, HEAD_DIM), lambda b, j, i: (b, j, 0, 0)),
                pl.BlockSpec((1, 1, t, HEAD_DIM), lambda b, j, i: (b, N_KV_HEADS + j, 0, 0))]
    args = [q, kv, kv]
    if selected:
        in_specs += [pl.BlockSpec((1, 1, tq, LANE), lambda b, j, i: (b, j, i, 0)),
                     pl.BlockSpec(expand.shape, lambda b, j, i: (0, 0))]
        args += [sel, expand]
    nk_max = t if selected else WINDOW + tq
    return pl.pallas_call(
        functools.partial(_attn_prompt_body, tq=tq, t=t, selected=selected), grid=(n, N_KV_HEADS, t // tq),
        in_specs=in_specs,
        out_specs=pl.BlockSpec((1, GQA, tq, HEAD_DIM), lambda b, j, i: (b, j, i, 0)),
        out_shape=jax.ShapeDtypeStruct((n, N_Q_HEADS, t, HEAD_DIM), F32),
        scratch_shapes=[pltpu.VMEM((GQA * tq, nk_max), F32), pltpu.VMEM((tq, nk_max), F32),
                        pltpu.VMEM((GQA * tq, nk_max), BF16), pltpu.VMEM((GQA * tq, HEAD_DIM), F32)],
        compiler_params=_cparams("arbitrary", "arbitrary", "arbitrary"),
        name="attn_prompt_sel" if selected else "attn_prompt_win",
    )(*args)


T_PAD = 8


def _pad_keys(x):
    return jnp.concatenate([x, jnp.zeros((PAGE_SIZE - x.shape[0], x.shape[1]), x.dtype)], axis=0)


def _attn_sample_sel_body(pt_ref, *refs, t_real):
    del pt_ref
    pages = refs[:N_PAGES]
    q_ref, sel_ref, new_ref, e_ref, o_ref = refs[N_PAGES:]
    rows = GQA * T_PAD
    tok = lax.broadcasted_iota(jnp.int32, (rows, PAGE_SIZE), 0) & (T_PAD - 1)
    col = lax.broadcasted_iota(jnp.int32, (rows, PAGE_SIZE), 1)
    for j in range(N_KV_HEADS):
        kc0, vc0 = j * HEAD_DIM, KV_DIM + j * HEAD_DIM
        q = q_ref[0, j * GQA:(j + 1) * GQA].reshape(rows, HEAD_DIM).astype(BF16)
        k_rows = pl.ds(j, PAGE_SIZE, stride=N_SLOT)
        v_rows = pl.ds(N_KV_HEADS + j, PAGE_SIZE, stride=N_SLOT)
        parts = [_dot_nt(q, pages[p][k_rows, :].astype(BF16)) for p in range(N_PAGES)]
        parts.append(_dot_nt(q, _pad_keys(new_ref[0, :, kc0:kc0 + HEAD_DIM]).astype(BF16)))
        s = jnp.concatenate(parts, axis=1) * SCALE
        sel = sel_ref[0, j]
        sel4 = jnp.concatenate([sel] * GQA, axis=0)
        picked = _dot(sel4.astype(BF16), e_ref[...])
        new_blk = PAST_LEN // L_SEL
        new_ok = (col < t_real) & (col <= tok)
        new_picked = jnp.where(new_ok, _lane_pick(sel4, new_blk), 0.0)
        mask = jnp.concatenate([picked, new_picked], axis=1) > 0.5
        p = _masked_softmax(s, mask).astype(BF16)
        o = _dot(p[:, PAST_LEN:], _pad_keys(new_ref[0, :, vc0:vc0 + HEAD_DIM]).astype(BF16))
        for pg in range(N_PAGES):
            o = o + _dot(p[:, pg * PAGE_SIZE:(pg + 1) * PAGE_SIZE], pages[pg][v_rows, :].astype(BF16))
        o_ref[0, j * GQA:(j + 1) * GQA] = o.reshape(GQA, T_PAD, HEAD_DIM)


def attn_sample_sel(pool, page_table, q, sel, new_rows, expand, t_real):
    n = page_table.shape[0]
    cols = new_rows.shape[-1]
    page_spec = lambda p: pl.BlockSpec((PAGE_SIZE * N_SLOT, HEAD_DIM), lambda i, pt, p=p: (pt[i, p], 0))
    return pl.pallas_call(
        functools.partial(_attn_sample_sel_body, t_real=t_real),
        grid_spec=pltpu.PrefetchScalarGridSpec(
            num_scalar_prefetch=1, grid=(n,),
            in_specs=[page_spec(p) for p in range(N_PAGES)]
            + [pl.BlockSpec((1, N_Q_HEADS, T_PAD, HEAD_DIM), lambda i, pt: (i, 0, 0, 0)),
               pl.BlockSpec((1, N_KV_HEADS, T_PAD, LANE), lambda i, pt: (i, 0, 0, 0)),
               pl.BlockSpec((1, T_PAD, cols), lambda i, pt: (i, 0, 0)),
               pl.BlockSpec(expand.shape, lambda i, pt: (0, 0))],
            out_specs=pl.BlockSpec((1, N_Q_HEADS, T_PAD, HEAD_DIM), lambda i, pt: (i, 0, 0, 0))),
        out_shape=jax.ShapeDtypeStruct((n, N_Q_HEADS, T_PAD, HEAD_DIM), F32),
        compiler_params=_cparams("arbitrary"), name="attn_sample_sel",
    )(page_table, *([pool] * N_PAGES), q, sel, new_rows, expand)


def _attn_sample_win_body(win_ref, q_ref, new_ref, newrows_ref, o_ref, wout_ref, *, t_real):
    rows = GQA * T_PAD
    w_buf = win_ref.shape[0] // N_SLOT
    tok_o = lax.broadcasted_iota(jnp.int32, (rows, w_buf), 0) & (T_PAD - 1)
    col_o = lax.broadcasted_iota(jnp.int32, (rows, w_buf), 1)
    tok_n = lax.broadcasted_iota(jnp.int32, (rows, PAGE_SIZE), 0) & (T_PAD - 1)
    col_n = lax.broadcasted_iota(jnp.int32, (rows, PAGE_SIZE), 1)
    old_ok = jnp.where(col_o + (WINDOW - w_buf) > tok_o, 1.0, 0.0)
    new_ok = jnp.where((col_n < t_real) & (col_n <= tok_n), 1.0, 0.0)
    mask = jnp.concatenate([old_ok, new_ok], axis=1) > 0.5
    for j in range(N_KV_HEADS):
        kc0, vc0 = j * HEAD_DIM, KV_DIM + j * HEAD_DIM
        q = q_ref[0, j * GQA:(j + 1) * GQA].reshape(rows, HEAD_DIM).astype(BF16)
        k_new = _pad_keys(new_ref[0, :, kc0:kc0 + HEAD_DIM]).astype(BF16)
        v_new = _pad_keys(new_ref[0, :, vc0:vc0 + HEAD_DIM]).astype(BF16)
        k_old = win_ref[pl.ds(j, w_buf, stride=N_SLOT), :].astype(BF16)
        v_old = win_ref[pl.ds(N_KV_HEADS + j, w_buf, stride=N_SLOT), :].astype(BF16)
        s = jnp.concatenate([_dot_nt(q, k_old), _dot_nt(q, k_new)], axis=1) * SCALE
        p = _masked_softmax(s, mask).astype(BF16)
        o = _dot(p[:, :w_buf], v_old) + _dot(p[:, w_buf:], v_new)
        o_ref[0, j * GQA:(j + 1) * GQA] = o.reshape(GQA, T_PAD, HEAD_DIM)
    keep = (w_buf - t_real) * N_SLOT
    wout_ref[0:keep, :] = win_ref[t_real * N_SLOT:, :]
    wout_ref[keep:, :] = newrows_ref[...]


def attn_sample_win(win, q, new_rows, new_cache_rows, t_real):
    n = q.shape[0]
    cols = new_rows.shape[-1]
    buf_rows = win.shape[0] // n
    return pl.pallas_call(
        functools.partial(_attn_sample_win_body, t_real=t_real), grid=(n,),
        in_specs=[pl.BlockSpec((buf_rows, HEAD_DIM), lambda i: (i, 0)),
                  pl.BlockSpec((1, N_Q_HEADS, T_PAD, HEAD_DIM), lambda i: (i, 0, 0, 0)),
                  pl.BlockSpec((1, T_PAD, cols), lambda i: (i, 0, 0)),
                  pl.BlockSpec((t_real * N_SLOT, HEAD_DIM), lambda i: (i, 0))],
        out_specs=[pl.BlockSpec((1, N_Q_HEADS, T_PAD, HEAD_DIM), lambda i: (i, 0, 0, 0)),
                   pl.BlockSpec((buf_rows, HEAD_DIM), lambda i: (i, 0))],
        out_shape=[jax.ShapeDtypeStruct((n, N_Q_HEADS, T_PAD, HEAD_DIM), F32),
                   jax.ShapeDtypeStruct(win.shape, F32)],
        compiler_params=_cparams("arbitrary"), name="attn_sample_win",
    )(win, q, new_rows, new_cache_rows)


def _combine_body(oc_ref, os_ref, ow_ref, gl_ref, g_ref, out_ref, *, rows):
    gates = jax.nn.sigmoid(gl_ref[...])
    outs = []
    sq = jnp.zeros((rows, 1), F32)
    for h in range(N_Q_HEADS):
        a = (_lane_pick(gates, h) * oc_ref[:, h].reshape(rows, HEAD_DIM)
             + _lane_pick(gates, N_Q_HEADS + h) * os_ref[:, h].reshape(rows, HEAD_DIM)
             + _lane_pick(gates, 2 * N_Q_HEADS + h) * ow_ref[:, h].reshape(rows, HEAD_DIM))
        outs.append(a)
        sq = sq + jnp.sum(a * a, axis=-1, keepdims=True)
    inv = lax.rsqrt(sq * (1.0 / ATTN_DIM) + RMS_EPS)
    for h in range(N_Q_HEADS):
        sl = slice(h * HEAD_DIM, (h + 1) * HEAD_DIM)
        out_ref[:, sl] = ((outs[h] * inv) * g_ref[:, sl]).astype(out_ref.dtype)


def combine(o_cmp, o_sel, o_win, gate_logits, gain, bn, tt):
    n, _, t, _ = o_cmp.shape
    nt = t // tt
    rows = bn * tt
    o_spec = pl.BlockSpec((bn, N_Q_HEADS, tt, HEAD_DIM), lambda b, i: (b, 0, i, 0))
    return pl.pallas_call(
        functools.partial(_combine_body, rows=rows), grid=(n // bn, nt),
        in_specs=[o_spec, o_spec, o_spec,
                  pl.BlockSpec((rows, LANE), lambda b, i: (b * nt + i, 0)),
                  pl.BlockSpec((1, ATTN_DIM), lambda b, i: (0, 0))],
        out_specs=pl.BlockSpec((rows, ATTN_DIM), lambda b, i: (b * nt + i, 0)),
        out_shape=jax.ShapeDtypeStruct((n * t, ATTN_DIM), BF16),
        compiler_params=_cparams("arbitrary", "arbitrary"), name="combine",
    )(o_cmp, o_sel, o_win, gate_logits, gain.reshape(1, ATTN_DIM))


TOKEN_CHUNKS = D_MODEL // LANE
TOKEN_PITCH = 40


def _router_body(h_ref, g_ref, w_ref, b_ref, xrows_ref, ids_ref, wts_ref):
    tm = h_ref.shape[0]
    h = h_ref[...]
    x = (h * lax.rsqrt(jnp.mean(h * h, axis=-1, keepdims=True) + RMS_EPS)) * g_ref[...]
    for c in range(TOKEN_PITCH):
        piece = x[:, c * LANE:(c + 1) * LANE] if c < TOKEN_CHUNKS else jnp.zeros((tm, LANE), F32)
        xrows_ref[pl.ds(c, tm, stride=TOKEN_PITCH), :] = piece
    logits = _dot(x.astype(BF16), w_ref[...].astype(BF16)) + b_ref[...]
    lane = lax.broadcasted_iota(jnp.int32, logits.shape, 1)
    is_grp = (lane >= N_EXPERTS) & (lane < N_EXPERTS + N_GROUPS)
    gl = jnp.where(is_grp, logits, NEG_INF)
    ge = jnp.where(is_grp, jnp.exp(gl - jnp.max(gl, axis=-1, keepdims=True)), 0.0)
    p_grp = ge / jnp.sum(ge, axis=-1, keepdims=True)
    g_val = jnp.max(p_grp, axis=-1, keepdims=True)
    g_idx = jnp.min(jnp.where(is_grp & (p_grp == g_val), lane, 2 * LANE), axis=-1, keepdims=True) - N_EXPERTS
    lo = g_idx * EXPERTS_PER_GROUP
    in_grp = (lane >= lo) & (lane < lo + EXPERTS_PER_GROUP)
    el = jnp.where(in_grp, logits, NEG_INF)
    ee = jnp.where(in_grp, jnp.exp(el - jnp.max(el, axis=-1, keepdims=True)), 0.0)
    p_e = ee / jnp.sum(ee, axis=-1, keepdims=True)
    cand = jnp.where(in_grp, p_e, -1.0)
    e1 = jnp.max(cand, axis=-1, keepdims=True)
    i1 = jnp.min(jnp.where(cand == e1, lane, 2 * LANE), axis=-1, keepdims=True)
    cand = jnp.where(lane == i1, -1.0, cand)
    e2 = jnp.max(cand, axis=-1, keepdims=True)
    i2 = jnp.min(jnp.where(cand == e2, lane, 2 * LANE), axis=-1, keepdims=True)
    tot = e1 + e2
    ids_ref[...] = jnp.where(lane == 0, i1, jnp.where(lane == 1, i2, 0))
    wts_ref[...] = jnp.where(lane == 0, g_val * e1 / tot, jnp.where(lane == 1, g_val * e2 / tot, 0.0))


def router(h, gain, w_router, b_router, tm=256):
    m, d = h.shape
    out_spec = pl.BlockSpec((tm, LANE), lambda i: (i, 0))
    return pl.pallas_call(
        _router_body, grid=(m // tm,),
        in_specs=[pl.BlockSpec((tm, d), lambda i: (i, 0)),
                  pl.BlockSpec((1, d), lambda i: (0, 0)),
                  pl.BlockSpec((d, LANE), lambda i: (0, 0)),
                  pl.BlockSpec((1, LANE), lambda i: (0, 0))],
        out_specs=[pl.BlockSpec((tm * TOKEN_PITCH, LANE), lambda i: (i, 0)), out_spec, out_spec],
        out_shape=[jax.ShapeDtypeStruct((m * TOKEN_PITCH, LANE), F32),
                   jax.ShapeDtypeStruct((m, LANE), jnp.int32), jax.ShapeDtypeStruct((m, LANE), F32)],
        compiler_params=_cparams("arbitrary"), name="router",
    )(h, gain.reshape(1, d), w_router, b_router)


def _gather_rows(idx_ref, n_rows, src_hbm, dst, sem, wait):
    def copy(r, src):
        return pltpu.make_async_copy(src_hbm.at[pl.ds(src, 1), :], dst.at[pl.ds(r, 1), :], sem)

    if wait:
        def body(r, c):
            copy(r, 0).wait()
            return c
        lax.fori_loop(0, n_rows, body, 0, unroll=8)
    else:
        for r in range(n_rows):
            copy(r, idx_ref[0, 0, r]).start()


def _gather_tokens(idx_ref, tokens, src_hbm, dst, base, sem, wait):
    def copy(r, tok):
        return pltpu.make_async_copy(src_hbm.at[pl.ds(pl.multiple_of(tok * TOKEN_PITCH, 8), TOKEN_CHUNKS), :],
                                     dst.at[pl.ds(pl.multiple_of(base + r * TOKEN_PITCH, 8), TOKEN_CHUNKS), :], sem)

    if wait:
        def body(r, c):
            copy(r, 0).wait()
            return c
        lax.fori_loop(tokens.start, tokens.stop, body, 0, unroll=8)
    else:
        for r in tokens:
            copy(r, idx_ref[0, 0, r]).start()


def _moe_ffn_body(te_ref, cur_ref, nxt_ref, x_hbm, rw_ref, wg_ref, wu_ref, wd_ref, ys_ref, xbuf, sem, *, tm):
    del te_ref
    i = pl.program_id(0)
    n = pl.num_programs(0)
    slot = lax.rem(i, 2)
    half = tm * TOKEN_PITCH
    here, other = slot * half, (1 - slot) * half

    @pl.when(i == 0)
    def _():
        _gather_tokens(cur_ref, range(tm), x_hbm, xbuf, 0, sem.at[0], wait=False)

    _gather_tokens(cur_ref, range(tm), x_hbm, xbuf, here, sem.at[slot], wait=True)

    def prefetch(part):
        q = tm // 4
        _gather_tokens(nxt_ref, range(part * q, (part + 1) * q), x_hbm, xbuf, other, sem.at[1 - slot], wait=False)

    prefetch(0)
    x = jnp.concatenate([xbuf[pl.ds(here + c, tm, stride=TOKEN_PITCH), :].astype(BF16) for c in range(TOKEN_CHUNKS)],
                        axis=1)
    prefetch(1)
    g = _dot(x, wg_ref[0])
    prefetch(2)
    hid = jax.nn.silu(g) * _dot(x, wu_ref[0])
    gate = jnp.concatenate([rw_ref[...]] * (hid.shape[1] // LANE), axis=1)
    prefetch(3)
    ys_ref[...] = _dot((hid * gate).astype(BF16), wd_ref[0])

    @pl.when(i == n - 1)
    def _():
        _gather_tokens(cur_ref, range(tm), x_hbm, xbuf, other, sem.at[1 - slot], wait=True)


def moe_ffn(x_rows, tile_expert, row_token, row_weight, w_gate, w_up, w_down, tm):
    n_tiles = tile_expert.shape[0]
    d = TOKEN_CHUNKS * LANE
    f = w_gate.shape[-1]
    smem_rows = lambda fn: pl.BlockSpec((1, 1, tm), fn, memory_space=pltpu.SMEM)
    return pl.pallas_call(
        functools.partial(_moe_ffn_body, tm=tm),
        grid_spec=pltpu.PrefetchScalarGridSpec(
            num_scalar_prefetch=1, grid=(n_tiles,),
            in_specs=[smem_rows(lambda i, te: (i, 0, 0)),
                      smem_rows(lambda i, te: (jnp.minimum(i + 1, n_tiles - 1), 0, 0)),
                      pl.BlockSpec(memory_space=pl.ANY),
                      pl.BlockSpec((tm, LANE), lambda i, te: (i, 0)),
                      pl.BlockSpec((1, d, f), lambda i, te: (te[i], 0, 0)),
                      pl.BlockSpec((1, d, f), lambda i, te: (te[i], 0, 0)),
                      pl.BlockSpec((1, f, d), lambda i, te: (te[i], 0, 0))],
            out_specs=pl.BlockSpec((tm, d), lambda i, te: (i, 0)),
            scratch_shapes=[pltpu.VMEM((2 * tm * TOKEN_PITCH, LANE), F32), pltpu.SemaphoreType.DMA((2,))]),
        out_shape=jax.ShapeDtypeStruct((n_tiles * tm, d), F32),
        compiler_params=_cparams("arbitrary"), name="moe_ffn",
    )(tile_expert, row_token, row_token, x_rows, row_weight, w_gate, w_up, w_down)


def _moe_combine_body(cur_ref, nxt_ref, ys_hbm, h_ref, ya_ref, yb_ref, buf, sem, *, tiles_a):
    i = pl.program_id(0)
    n = pl.num_programs(0)
    rows = buf.shape[1]
    tm = h_ref.shape[0]
    slot = lax.rem(i, 2)

    @pl.when(i == 0)
    def _():
        _gather_rows(cur_ref, rows, ys_hbm, buf.at[0], sem.at[0], wait=False)

    _gather_rows(cur_ref, rows, ys_hbm, buf.at[slot], sem.at[slot], wait=True)
    _gather_rows(nxt_ref, rows, ys_hbm, buf.at[1 - slot], sem.at[1 - slot], wait=False)
    y = h_ref[...] + buf[slot, 0:tm, :] + buf[slot, tm:rows, :]

    @pl.when(i == n - 1)
    def _():
        _gather_rows(cur_ref, rows, ys_hbm, buf.at[1 - slot], sem.at[1 - slot], wait=True)

    @pl.when(i < tiles_a)
    def _():
        ya_ref[...] = y

    @pl.when(i >= tiles_a)
    def _():
        yb_ref[...] = y


def moe_combine(ys, pair_row, h, rows_a, tm=128):
    m, d = h.shape
    n_tiles = m // tm
    tiles_a = rows_a // tm
    assert rows_a % tm == 0 and 0 < tiles_a < n_tiles
    smem_rows = lambda fn: pl.BlockSpec((1, 1, 2 * tm), fn, memory_space=pltpu.SMEM)
    return pl.pallas_call(
        functools.partial(_moe_combine_body, tiles_a=tiles_a), grid=(n_tiles,),
        in_specs=[smem_rows(lambda i: (i, 0, 0)),
                  smem_rows(lambda i: (jnp.minimum(i + 1, n_tiles - 1), 0, 0)),
                  pl.BlockSpec(memory_space=pl.ANY),
                  pl.BlockSpec((tm, d), lambda i: (i, 0))],
        out_specs=[pl.BlockSpec((tm, d), lambda i: (jnp.minimum(i, tiles_a - 1), 0)),
                   pl.BlockSpec((tm, d), lambda i: (jnp.maximum(i - tiles_a, 0), 0))],
        out_shape=[jax.ShapeDtypeStruct((rows_a, d), F32), jax.ShapeDtypeStruct((m - rows_a, d), F32)],
        scratch_shapes=[pltpu.VMEM((2, 2 * tm, d), F32), pltpu.SemaphoreType.DMA((2,))],
        compiler_params=_cparams("arbitrary"), name="moe_combine",
    )(pair_row, pair_row, ys, h)


def moe_routed(xt, ids, wts, h, w_gate, w_up, w_down, tm, rows_a):
    m = h.shape[0]
    n_pair = 2 * m
    n_tiles = (n_pair + N_EXPERTS * (tm - 1)) // tm + 1
    flat_e = ids[:, :2].reshape(n_pair)
    flat_w = wts[:, :2].reshape(n_pair)
    order = jnp.argsort(flat_e, stable=True).astype(jnp.int32)
    rank = jnp.argsort(order).astype(jnp.int32)
    counts = jnp.sum(flat_e[:, None] == jnp.arange(N_EXPERTS, dtype=jnp.int32)[None, :], axis=0, dtype=jnp.int32)
    padded = ((counts + tm - 1) // tm) * tm
    pad_end = jnp.cumsum(padded)
    pad_start = pad_end - padded
    start = jnp.cumsum(counts) - counts
    tile_start = jnp.arange(n_tiles, dtype=jnp.int32) * tm
    tile_expert = jnp.minimum(jnp.sum(tile_start[:, None] >= pad_end[None, :], axis=1), N_EXPERTS - 1).astype(jnp.int32)
    row_e = jnp.repeat(tile_expert, tm)
    offs = jnp.arange(n_tiles * tm, dtype=jnp.int32) - pad_start[row_e]
    used = offs < counts[row_e]
    src = order[jnp.clip(start[row_e] + offs, 0, n_pair - 1)]
    row_token = jnp.where(used, src // 2, 0)
    row_weight = jnp.where(used, flat_w[src], 0.0)
    pair_row = (pad_start[flat_e] + rank - start[flat_e]).reshape(m, 2)
    ys = moe_ffn(xt, tile_expert, row_token.reshape(n_tiles, 1, tm),
                 jnp.broadcast_to(row_weight[:, None], (n_tiles * tm, LANE)), w_gate, w_up, w_down, tm)
    tc = 128
    pair_tiles = pair_row.reshape(m // tc, tc, 2).transpose(0, 2, 1).reshape(m // tc, 1, 2 * tc)
    return moe_combine(ys, pair_tiles, h, rows_a, tm=tc)


def _cover_matrix(nsb):
    i = np.arange(N_CMP_BLK)[:, None]
    j = np.arange(LANE)[None, :]
    m = np.zeros((N_CMP_BLK, LANE), np.float32)
    for a in range(L_SEL // CMP_STRIDE):
        for c in range(R_CMP):
            m += (i == (L_SEL // CMP_STRIDE) * j + a - c)
    m[N_CMP_BLK - 1:, :] = 0.0
    m[:, nsb:] = 0.0
    assert nsb <= SEL_ROWS
    return jnp.asarray(m.T, BF16)


def _expand_matrix(n_keys):
    b = np.arange(LANE)[:, None]
    k = np.arange(n_keys)[None, :]
    return jnp.asarray((k // L_SEL == b).astype(np.float32), BF16)


def _to_heads(q, n, t):
    return q.reshape(n, t, N_Q_HEADS, HEAD_DIM).transpose(0, 2, 1, 3)


def kernel(x_prompt, x_sample, cache_cmp_kv, cache_sel_kv, state_win_kv, state_conv, page_table, norm_mix_g, w_in,
           conv_w, q_norm_g, k_norm_g, phi_pe, phi_w1, phi_w2, out_norm_g, w_out, norm_ffn_g, w_group_router,
           b_group_router, w_expert_router, b_expert_router, w_gate, w_up, w_down):
    n_p, t_p, d = x_prompt.shape
    n_s, t_s, _ = x_sample.shape
    assert w_in.shape[0] == 1 and t_s < CMP_STRIDE and t_s <= T_PAD and t_p % WINDOW == 0
    kv_cols = 2 * KV_DIM
    w_in_t = jnp.transpose(w_in[0])
    w_in_gate_t = jnp.pad(w_in_t[Z_MAIN:], ((0, LANE - 3 * N_Q_HEADS), (0, 0)))
    w_o = w_out
    wkv = phi_w1[0].reshape(2, R_CMP, CMP_STRIDE, HEAD_DIM, HEAD_DIM).transpose(2, 0, 3, 1, 4)
    wkv = wkv.reshape(CMP_STRIDE * 2 * HEAD_DIM, R_CMP * HEAD_DIM)
    pe5 = phi_pe[0].reshape(2, R_CMP, CMP_STRIDE, 1, HEAD_DIM)
    pe_kv = (pe5 * jnp.eye(2, dtype=F32).reshape(2, 1, 1, 2, 1)).reshape(2, R_CMP, CMP_STRIDE * 2 * HEAD_DIM)
    w2cat = phi_w2[0].reshape(2 * HEAD_DIM, HEAD_DIM)
    w_router = jnp.pad(jnp.concatenate([w_expert_router[0], w_group_router[0]], axis=1),
                       ((0, 0), (0, LANE - N_EXPERTS - N_GROUPS)))
    b_router = jnp.pad(jnp.concatenate([b_expert_router[0], b_group_router[0]]),
                       (0, LANE - N_EXPERTS - N_GROUPS)).reshape(1, LANE)
    wg_bf, wu_bf, wd_bf = w_gate[0].astype(BF16), w_up[0].astype(BF16), w_down[0].astype(BF16)

    def project(x2d):
        xn = rmsnorm_cast(x2d, norm_mix_g[0])
        z = matmul([([xn], None)], [(w_in_t, 0)], tm=1024, tn=512, w_transposed=True)
        gate_logits = matmul([([xn], None)], [(w_in_gate_t, 0)], tn=LANE, w_transposed=True)
        return z, gate_logits

    m_p = n_p * t_p

    xp = x_prompt.reshape(n_p * t_p, d)
    z, glog = project(xp)
    qh, ks_h, kw_h, kvc, kvs, kvw = postproj_prompt(z, n_p, t_p, q_norm_g[0], k_norm_g[0])
    conv_out, conv_last = conv_prompt(z, n_p, t_p, conv_w[0], out_norm_g[0][:CONV_DIM])
    ident = jnp.arange(n_p * (t_p // PAGE_SIZE), dtype=jnp.int32).reshape(n_p, t_p // PAGE_SIZE)
    kv_cmp = compress(kvc, ident, wkv, pe_kv, w2cat, k_norm_g[0])
    o_cmp, sel = cmp_select(qh, kv_cmp, _cover_matrix(t_p // L_SEL), bn=1, tq=256, pos_base=0)
    o_sel = attn_prompt(qh, ks_h, sel, _expand_matrix(t_p))
    o_win = attn_prompt(qh, kw_h, None, None)
    attn_out = combine(o_cmp, o_sel, o_win, glog, out_norm_g[0][CONV_DIM:], bn=1, tt=256)
    mixed_p = ([conv_out, attn_out], xp)
    kv_shape = (1, n_p, t_p, 2, N_KV_HEADS, HEAD_DIM)
    w_keep = min(WINDOW, t_p)
    prompt_win = kvw.reshape(kv_shape)[:, :, t_p - w_keep:]
    prompt_conv = conv_last[:, 8 - (CONV_WIDTH - 1):, :][None]

    xs = x_sample.reshape(n_s * t_s, d)
    z, glog = project(xs)
    q, kvc_s, kvs_s, kvw_s = postproj(z, q_norm_g[0], k_norm_g[0])
    conv_out, conv_state = conv_sample(z, n_s, t_s, state_conv[0], conv_w[0], out_norm_g[0][:CONV_DIM])
    pad_t = lambda a: jnp.pad(a, ((0, 0), (0, T_PAD - t_s), (0, 0)))
    qh = jnp.pad(_to_heads(q, n_s, t_s), ((0, 0), (0, 0), (0, T_PAD - t_s), (0, 0)))
    pool_cmp = cache_cmp_kv[0].reshape(-1, HEAD_DIM)
    pool_sel = cache_sel_kv[0].reshape(-1, HEAD_DIM)
    kv_cmp = compress(pool_cmp, page_table, wkv, pe_kv, w2cat, k_norm_g[0])
    o_cmp, sel = cmp_select(qh, kv_cmp, _cover_matrix(PAST_LEN // L_SEL + 1), bn=16, tq=T_PAD, pos_base=PAST_LEN)
    o_sel = attn_sample_sel(pool_sel, page_table, qh, sel, pad_t(kvs_s.reshape(n_s, t_s, kv_cols)),
                            _expand_matrix(PAST_LEN), t_s)
    win = state_win_kv[0].reshape(-1, HEAD_DIM)
    o_win, win_new = attn_sample_win(win, qh, pad_t(kvw_s.reshape(n_s, t_s, kv_cols)),
                                     kvw_s.reshape(-1, HEAD_DIM), t_s)
    glog_pad = pad_t(glog.reshape(n_s, t_s, LANE)).reshape(n_s * T_PAD, LANE)
    attn_out = combine(o_cmp, o_sel, o_win, glog_pad, out_norm_g[0][CONV_DIM:], bn=32, tt=T_PAD)
    attn_out = attn_out.reshape(n_s, T_PAD, ATTN_DIM)[:, :t_s].reshape(n_s * t_s, ATTN_DIM)
    s_shape = (1, n_s, t_s, 2, N_KV_HEADS, HEAD_DIM)

    h_all = matmul([mixed_p, ([conv_out, attn_out], xs)], [(w_o, 0), (w_o, 1)])
    x_rows, ids, wts = router(h_all, norm_ffn_g[0], w_router, b_router)
    y_prompt, y_sample = moe_routed(x_rows, ids, wts, h_all, wg_bf, wu_bf, wd_bf, 256, m_p)

    return (y_prompt.reshape(n_p, t_p, d), y_sample.reshape(n_s, t_s, d), kvc.reshape(kv_shape), kvs.reshape(kv_shape), prompt_win, prompt_conv,
            kvc_s.reshape(s_shape), kvs_s.reshape(s_shape),
            win_new.reshape(1, n_s, -1, 2, N_KV_HEADS, HEAD_DIM), conv_state[None])
```

```python
import functools

import numpy as np
import jax
import jax.numpy as jnp
from jax import lax
from jax.experimental import pallas as pl
from jax.experimental.pallas import tpu as pltpu

F32 = jnp.float32
BF16 = jnp.bfloat16

D_MODEL = 4096
PAST_LEN = 2048
PAGE_SIZE = 128
HEAD_DIM = 128
CONV_DIM = 2048
N_Q_HEADS = 16
N_KV_HEADS = 4
GQA = 4
ATTN_DIM = 2048
KV_DIM = 512
CONV_WIDTH = 3
L_CMP = 32
CMP_STRIDE = 16
R_CMP = 2
L_SEL = 64
N_SEL = 8
WINDOW = 512
FORCE_BONUS = 1e3
SCALE = HEAD_DIM ** -0.5
N_GROUPS = 4
EXPERTS_PER_GROUP = 4
N_EXPERTS = 16
D_FF_EXPERT = 512
RMS_EPS = 1e-6
NEG_INF = -1e30
TINY = 1e-30
PICKED = -3e38
LOG2_E = 1.4426950408889634

N_PAGES = PAST_LEN // PAGE_SIZE
N_CMP_BLK = 128
Z_MAIN = 3 * CONV_DIM + ATTN_DIM + 6 * KV_DIM
LANE = 128
VMEM_LIMIT = 56 * 1024 * 1024


def _cparams(*sem):
    return pltpu.CompilerParams(dimension_semantics=sem, vmem_limit_bytes=VMEM_LIMIT)


def _masked_softmax(s, mask):
    s = jnp.where(mask, s, NEG_INF)
    m = jnp.max(s, axis=-1, keepdims=True)
    e = jnp.where(mask, jnp.exp(s - m), 0.0)
    return e / jnp.maximum(jnp.sum(e, axis=-1, keepdims=True), TINY)


def _dot_nt(a, b):
    return lax.dot_general(a, b, (((1,), (1,)), ((), ())), preferred_element_type=F32)


def _dot(a, b):
    return jnp.dot(a, b, preferred_element_type=F32)


def _lane_pick(x, c):
    lane = lax.broadcasted_iota(jnp.int32, x.shape, 1)
    return jnp.sum(jnp.where(lane == c, x, 0.0), axis=-1, keepdims=True)


def _rmsnorm_body(x_ref, g_ref, o_ref):
    x = x_ref[...]
    inv = lax.rsqrt(jnp.mean(x * x, axis=-1, keepdims=True) + RMS_EPS)
    o_ref[...] = ((x * inv) * g_ref[...]).astype(o_ref.dtype)


def rmsnorm_cast(x, g, dtype=BF16, tm=256):
    m, d = x.shape
    return pl.pallas_call(
        _rmsnorm_body, grid=(m // tm,),
        in_specs=[pl.BlockSpec((tm, d), lambda i: (i, 0)), pl.BlockSpec((1, d), lambda i: (0, 0))],
        out_specs=pl.BlockSpec((tm, d), lambda i: (i, 0)),
        out_shape=jax.ShapeDtypeStruct((m, d), dtype),
        compiler_params=_cparams("arbitrary"), name="rmsnorm_cast",
    )(x, g.reshape(1, d))


def _matmul_body(*refs, n_w, has_res, tiles, w_transposed):
    per = n_w + (1 if has_res else 0)
    n_groups = len(tiles) - 1
    o_ref = refs[n_w + n_groups * per]
    wbf = refs[n_w + n_groups * per + 1:]
    i = pl.program_id(1)
    dot = _dot_nt if w_transposed else _dot

    @pl.when(i == 0)
    def _():
        for p in range(n_w):
            wbf[p][...] = refs[p][...].astype(BF16)

    for g in range(n_groups):
        grp = refs[n_w + g * per:n_w + (g + 1) * per]

        def compute(grp=grp):
            acc = dot(grp[0][...], wbf[0][...])
            for p in range(1, n_w):
                acc = acc + dot(grp[p][...], wbf[p][...])
            if has_res:
                acc = acc + grp[n_w][...]
            o_ref[...] = acc

        if n_groups == 1:
            compute()
        else:
            pl.when((i >= tiles[g]) & (i < tiles[g + 1]))(compute)


def matmul(groups, weights, tm=512, tn=512, w_transposed=False):
    ms = [g[0][0].shape[0] for g in groups]
    tm = min([tm] + ms)
    assert all(m % tm == 0 for m in ms)
    n = (weights[0][0].shape[0 if w_transposed else -1] // tn) * tn
    has_res = groups[0][1] is not None
    tiles = [0]
    for m in ms:
        tiles.append(tiles[-1] + m // tm)
    in_specs, args, scratch = [], [], []
    ks = [a.shape[1] for a in groups[0][0]]
    for (w, kb), k in zip(weights, ks):
        if w_transposed:
            in_specs.append(pl.BlockSpec((tn, k), lambda j, i, kb=kb: (j, kb)))
        elif w.ndim == 3:
            in_specs.append(pl.BlockSpec((None, k, tn), lambda j, i, kb=kb: (0, kb, j)))
        else:
            in_specs.append(pl.BlockSpec((k, tn), lambda j, i, kb=kb: (kb, j)))
        args.append(w)
        scratch.append(pltpu.VMEM((tn, k) if w_transposed else (k, tn), BF16))
    for g, (a_list, res) in enumerate(groups):
        lo, hi = tiles[g], tiles[g + 1]
        row = lambda i, lo=lo, hi=hi: jnp.clip(i, lo, hi - 1) - lo
        for a, k in zip(a_list, ks):
            in_specs.append(pl.BlockSpec((tm, k), lambda j, i, row=row: (row(i), 0)))
            args.append(a)
        if has_res:
            in_specs.append(pl.BlockSpec((tm, tn), lambda j, i, row=row: (row(i), j)))
            args.append(res)
    return pl.pallas_call(
        functools.partial(_matmul_body, n_w=len(weights), has_res=has_res, tiles=tuple(tiles),
                          w_transposed=w_transposed),
        grid=(n // tn, tiles[-1]), in_specs=in_specs,
        out_specs=pl.BlockSpec((tm, tn), lambda j, i: (i, j)),
        out_shape=jax.ShapeDtypeStruct((sum(ms), n), F32), scratch_shapes=scratch,
        compiler_params=_cparams("arbitrary", "arbitrary"), name="matmul",
    )(*args)


def _head_norm(x, g):
    inv = lax.rsqrt(jnp.mean(x * x, axis=-1, keepdims=True) + RMS_EPS)
    return (x * inv) * g


def _postproj_body(zq_ref, zc_ref, zs_ref, zw_ref, qg_ref, kg_ref, q_ref, kvc_ref, kvs_ref, kvw_ref):
    for h in range(N_Q_HEADS):
        sl = slice(h * HEAD_DIM, (h + 1) * HEAD_DIM)
        q_ref[:, sl] = _head_norm(zq_ref[:, sl], qg_ref[...]).astype(q_ref.dtype)
    kvc_ref[...] = zc_ref[...]
    for h in range(N_KV_HEADS):
        sl = slice(h * HEAD_DIM, (h + 1) * HEAD_DIM)
        kvs_ref[:, sl] = _head_norm(zs_ref[:, sl], kg_ref[1:2, :])
        kvw_ref[:, sl] = _head_norm(zw_ref[:, sl], kg_ref[2:3, :])
    kvs_ref[:, KV_DIM:] = zs_ref[:, KV_DIM:]
    kvw_ref[:, KV_DIM:] = zw_ref[:, KV_DIM:]


def postproj(z, q_norm_g, k_norm_g, tm=256):
    m = z.shape[0]
    kv = 2 * KV_DIM
    q0 = 3 * CONV_DIM // ATTN_DIM
    c0 = (3 * CONV_DIM + ATTN_DIM) // kv
    return pl.pallas_call(
        _postproj_body, grid=(m // tm,),
        in_specs=[pl.BlockSpec((tm, ATTN_DIM), lambda i: (i, q0)),
                  pl.BlockSpec((tm, kv), lambda i: (i, c0)),
                  pl.BlockSpec((tm, kv), lambda i: (i, c0 + 1)),
                  pl.BlockSpec((tm, kv), lambda i: (i, c0 + 2)),
                  pl.BlockSpec((1, HEAD_DIM), lambda i: (0, 0)),
                  pl.BlockSpec((3, HEAD_DIM), lambda i: (0, 0))],
        out_specs=[pl.BlockSpec((tm, ATTN_DIM), lambda i: (i, 0)),
                   pl.BlockSpec((tm, kv), lambda i: (i, 0)),
                   pl.BlockSpec((tm, kv), lambda i: (i, 0)),
                   pl.BlockSpec((tm, kv), lambda i: (i, 0))],
        out_shape=[jax.ShapeDtypeStruct((m, ATTN_DIM), F32)] + [jax.ShapeDtypeStruct((m, kv), F32)] * 3,
        compiler_params=_cparams("arbitrary"), name="postproj",
    )(z, z, z, z, q_norm_g.reshape(1, HEAD_DIM), k_norm_g)


N_SLOT = 2 * N_KV_HEADS


def _postproj_prompt_body(zq_ref, zc_ref, zs_ref, zw_ref, qg_ref, kg_ref,
                          q_ref, ks_ref, kw_ref, kvc_ref, kvs_ref, kvw_ref):
    tm = zq_ref.shape[0]
    for h in range(N_Q_HEADS):
        sl = slice(h * HEAD_DIM, (h + 1) * HEAD_DIM)
        q_ref[0, h] = _head_norm(zq_ref[:, sl], qg_ref[...]).astype(BF16)
    for c in range(N_SLOT):
        sl = slice(c * HEAD_DIM, (c + 1) * HEAD_DIM)
        rows = pl.ds(c, tm, stride=N_SLOT)
        kvc_ref[rows, :] = zc_ref[:, sl]
        xs, xw = zs_ref[:, sl], zw_ref[:, sl]
        if c < N_KV_HEADS:
            xs, xw = _head_norm(xs, kg_ref[1:2, :]), _head_norm(xw, kg_ref[2:3, :])
        kvs_ref[rows, :] = xs
        kvw_ref[rows, :] = xw
        ks_ref[0, c] = xs.astype(BF16)
        kw_ref[0, c] = xw.astype(BF16)


def postproj_prompt(z, n, t, q_norm_g, k_norm_g, tm=256):
    kv = 2 * KV_DIM
    nt = t // tm
    q0 = 3 * CONV_DIM // ATTN_DIM
    c0 = (3 * CONV_DIM + ATTN_DIM) // kv
    rows_spec = pl.BlockSpec((tm * N_SLOT, HEAD_DIM), lambda b, i: (b * nt + i, 0))
    slot_spec = pl.BlockSpec((1, N_SLOT, tm, HEAD_DIM), lambda b, i: (b, 0, i, 0))
    rows_sds = jax.ShapeDtypeStruct((n * t * N_SLOT, HEAD_DIM), F32)
    slot_sds = jax.ShapeDtypeStruct((n, N_SLOT, t, HEAD_DIM), BF16)
    return pl.pallas_call(
        _postproj_prompt_body, grid=(n, nt),
        in_specs=[pl.BlockSpec((tm, ATTN_DIM), lambda b, i: (b * nt + i, q0)),
                  pl.BlockSpec((tm, kv), lambda b, i: (b * nt + i, c0)),
                  pl.BlockSpec((tm, kv), lambda b, i: (b * nt + i, c0 + 1)),
                  pl.BlockSpec((tm, kv), lambda b, i: (b * nt + i, c0 + 2)),
                  pl.BlockSpec((1, HEAD_DIM), lambda b, i: (0, 0)),
                  pl.BlockSpec((3, HEAD_DIM), lambda b, i: (0, 0))],
        out_specs=[pl.BlockSpec((1, N_Q_HEADS, tm, HEAD_DIM), lambda b, i: (b, 0, i, 0)),
                   slot_spec, slot_spec, rows_spec, rows_spec, rows_spec],
        out_shape=[jax.ShapeDtypeStruct((n, N_Q_HEADS, t, HEAD_DIM), BF16), slot_sds, slot_sds,
                   rows_sds, rows_sds, rows_sds],
        compiler_params=_cparams("arbitrary", "arbitrary"), name="postproj_prompt",
    )(z, z, z, z, q_norm_g.reshape(1, HEAD_DIM), k_norm_g)


def _conv_finish(b, y, g):
    c = b * y
    inv = lax.rsqrt(jnp.mean(c * c, axis=-1, keepdims=True) + RMS_EPS)
    return ((c * inv) * g).astype(BF16)


def _conv_prompt_body(gb_ref, gc_ref, hc_ref, w_ref, g_ref, o_ref, st_ref, carry_ref):
    tt = gb_ref.shape[0]

    @pl.when(pl.program_id(1) == 0)
    def _():
        carry_ref[...] = jnp.zeros_like(carry_ref)

    u = gc_ref[...] * hc_ref[...]
    prev = carry_ref[...]
    p1, p2 = prev[7:8, :], prev[6:7, :]
    row = lax.broadcasted_iota(jnp.int32, u.shape, 0)
    u1 = jnp.where(row == 0, p1, pltpu.roll(u, 1, axis=0))
    u2 = jnp.where(row == 0, p2, jnp.where(row == 1, p1, pltpu.roll(u, 2, axis=0)))
    y = u2 * w_ref[0:1, :] + u1 * w_ref[1:2, :] + u * w_ref[2:3, :]
    o_ref[...] = _conv_finish(gb_ref[...], y, g_ref[...])
    last = u[tt - 8:tt, :]
    carry_ref[...] = last
    st_ref[0] = last


def conv_prompt(z, n, t, conv_w, gain, tt=256):
    nt = t // tt
    row = lambda b, i: (b * nt + i, 0)
    return pl.pallas_call(
        _conv_prompt_body, grid=(n, nt),
        in_specs=[pl.BlockSpec((tt, CONV_DIM), lambda b, i: (b * nt + i, 0)),
                  pl.BlockSpec((tt, CONV_DIM), lambda b, i: (b * nt + i, 1)),
                  pl.BlockSpec((tt, CONV_DIM), lambda b, i: (b * nt + i, 2)),
                  pl.BlockSpec((CONV_WIDTH, CONV_DIM), lambda b, i: (0, 0)),
                  pl.BlockSpec((1, CONV_DIM), lambda b, i: (0, 0))],
        out_specs=[pl.BlockSpec((tt, CONV_DIM), row),
                   pl.BlockSpec((1, 8, CONV_DIM), lambda b, i: (b, 0, 0))],
        out_shape=[jax.ShapeDtypeStruct((n * t, CONV_DIM), BF16), jax.ShapeDtypeStruct((n, 8, CONV_DIM), F32)],
        scratch_shapes=[pltpu.VMEM((8, CONV_DIM), F32)],
        compiler_params=_cparams("arbitrary", "arbitrary"), name="conv_prompt",
    )(z, z, z, conv_w, gain.reshape(1, CONV_DIM))


def _conv_sample_body(z_ref, pre_ref, w_ref, g_ref, o_ref, st_ref, *, t):
    up = [pre_ref[k] for k in range(CONV_WIDTH - 1)] + [z_ref[1, k] * z_ref[2, k] for k in range(t)]
    for k in range(t):
        y = up[k] * w_ref[0:1, :] + up[k + 1] * w_ref[1:2, :] + up[k + 2] * w_ref[2:3, :]
        o_ref[k] = _conv_finish(z_ref[0, k], y, g_ref[...])
    for k in range(CONV_WIDTH - 1):
        st_ref[k] = up[t + k]


def conv_sample(z, n, t, state, conv_w, gain):
    zt = z[:, :3 * CONV_DIM].reshape(n, t, 3, CONV_DIM).transpose(2, 1, 0, 3)
    whole = lambda shape: pl.BlockSpec(shape, lambda i: (0,) * len(shape))
    out, st = pl.pallas_call(
        functools.partial(_conv_sample_body, t=t), grid=(1,),
        in_specs=[whole((3, t, n, CONV_DIM)), whole((CONV_WIDTH - 1, n, CONV_DIM)),
                  whole((CONV_WIDTH, CONV_DIM)), whole((1, CONV_DIM))],
        out_specs=[whole((t, n, CONV_DIM)), whole((CONV_WIDTH - 1, n, CONV_DIM))],
        out_shape=[jax.ShapeDtypeStruct((t, n, CONV_DIM), BF16),
                   jax.ShapeDtypeStruct((CONV_WIDTH - 1, n, CONV_DIM), F32)],
        compiler_params=_cparams("arbitrary"), name="conv_sample",
    )(zt, state.transpose(1, 0, 2), conv_w, gain.reshape(1, CONV_DIM))
    return out.transpose(1, 0, 2).reshape(n * t, CONV_DIM), st.transpose(1, 0, 2)


def _compress_body(pt_ref, *refs):
    del pt_ref
    pages = refs[:N_PAGES]
    wkv_ref, pe_ref, w2_ref, kg_ref, out_ref = refs[N_PAGES:]
    cpp = PAGE_SIZE // CMP_STRIDE
    x4 = [pages[p][...].reshape(cpp, CMP_STRIDE, N_SLOT, HEAD_DIM) for p in range(N_PAGES)]
    n_rows = N_PAGES * cpp * N_SLOT
    is_key = (lax.broadcasted_iota(jnp.int32, (n_rows, HEAD_DIM), 0) & (N_SLOT - 1)) < N_KV_HEADS

    def split(x):
        return [jnp.where(is_key, x, 0.0).astype(BF16), jnp.where(is_key, 0.0, x).astype(BF16)]

    pieces = []
    for s in range(CMP_STRIDE):
        pieces += split(jnp.concatenate([x4[p][:, s].reshape(cpp * N_SLOT, HEAD_DIM) for p in range(N_PAGES)], axis=0))
    wkv = wkv_ref[...].astype(BF16)
    r = _dot(jnp.concatenate(pieces, axis=1), wkv)
    hpre = r[:, :HEAD_DIM] + pltpu.roll(r[:, HEAD_DIM:], n_rows - N_SLOT, axis=0)
    bias = []
    for v in range(2):
        b = jnp.zeros((8, HEAD_DIM), F32)
        for rr in range(R_CMP):
            pe = jnp.broadcast_to(pe_ref[v, rr:rr + 1, :], (8, wkv.shape[0])).astype(BF16)
            b = b + _dot(pe, wkv[:, rr * HEAD_DIM:(rr + 1) * HEAD_DIM])
        bias.append(b[0:1, :])
    hid = jax.nn.gelu(hpre + jnp.where(is_key, bias[0], bias[1]))
    out = _dot(jnp.concatenate(split(hid), axis=1), w2_ref[...].astype(BF16))
    out_ref[0] = jnp.where(is_key, _head_norm(out, kg_ref[0:1, :]), out)


def compress(pool, page_table, wkv, pe_kv, w2cat, k_norm_g):
    n = page_table.shape[0]
    page_rows = PAGE_SIZE * N_SLOT
    page_spec = lambda p: pl.BlockSpec((page_rows, HEAD_DIM), lambda i, pt, p=p: (pt[i, p], 0))
    const = lambda shape: pl.BlockSpec(shape, lambda i, pt: (0,) * len(shape))
    return pl.pallas_call(
        _compress_body,
        grid_spec=pltpu.PrefetchScalarGridSpec(
            num_scalar_prefetch=1, grid=(n,),
            in_specs=[page_spec(p) for p in range(N_PAGES)]
            + [const(wkv.shape), const(pe_kv.shape), const(w2cat.shape), const(k_norm_g.shape)],
            out_specs=pl.BlockSpec((1, N_CMP_BLK * N_SLOT, HEAD_DIM), lambda i, pt: (i, 0, 0))),
        out_shape=jax.ShapeDtypeStruct((n, N_CMP_BLK * N_SLOT, HEAD_DIM), F32),
        compiler_params=_cparams("arbitrary"), name="compress",
    )(page_table, *([pool] * N_PAGES), wkv, pe_kv, w2cat, k_norm_g)


SEL_ROWS = 40


def _cmp_select_body(q_ref, kvc_ref, cov_ref, o_ref, sel_ref, *, bn, tq, pos_base):
    qt = pl.program_id(1)
    grp = GQA * tq
    pairs = [(b, j) for b in range(bn) for j in range(N_KV_HEADS)]

    def slot(b, c):
        return kvc_ref[b, pl.ds(c, N_CMP_BLK, stride=N_SLOT), :].astype(BF16)

    s = jnp.concatenate(
        [_dot_nt(q_ref[b, j * GQA:(j + 1) * GQA].reshape(grp, HEAD_DIM).astype(BF16), slot(b, j)) for b, j in pairs],
        axis=0) * SCALE
    rows = len(pairs) * grp
    pos = pos_base + qt * tq + (lax.broadcasted_iota(jnp.int32, (rows, N_CMP_BLK), 0) & (tq - 1))
    blk = lax.broadcasted_iota(jnp.int32, (rows, N_CMP_BLK), 1)
    valid = (blk < N_CMP_BLK - 1) & (blk * CMP_STRIDE + (L_CMP - 1) <= pos)
    p = _masked_softmax(s, valid)
    p_bf = p.astype(BF16)
    for i, (b, j) in enumerate(pairs):
        o_ref[b, j * GQA:(j + 1) * GQA] = _dot(p_bf[i * grp:(i + 1) * grp],
                                               slot(b, N_KV_HEADS + j)).reshape(GQA, tq, HEAD_DIM)
    psum = jnp.sum(p.reshape(len(pairs), GQA, tq, N_CMP_BLK), axis=1).reshape(len(pairs) * tq, N_CMP_BLK)
    p_hi = psum.astype(BF16)
    p_lo = (psum - p_hi.astype(F32)).astype(BF16)
    imp_t = _dot_nt(cov_ref[...], p_hi) + _dot_nt(cov_ref[...], p_lo)
    srows = len(pairs) * tq
    imp_t = imp_t[0:SEL_ROWS, :]
    blk = lax.broadcasted_iota(jnp.int32, (SEL_ROWS, srows), 0)
    cur = (pos_base + qt * tq + (lax.broadcasted_iota(jnp.int32, (SEL_ROWS, srows), 1) & (tq - 1))) >> 6
    forced = (blk == 0) | (blk == cur) | (blk == cur - 1)
    score = jnp.where(blk <= cur, imp_t + jnp.where(forced, FORCE_BONUS, 0.0), NEG_INF)
    sel_t = jnp.zeros((SEL_ROWS, srows), F32)
    for _ in range(N_SEL):
        m = jnp.max(score, axis=0, keepdims=True)
        first = jnp.min(jnp.where(score == m, blk, LANE), axis=0, keepdims=True)
        hit = blk == first
        sel_t = jnp.where(hit, 1.0, sel_t)
        score = jnp.where(hit, PICKED, score)
    sel = jnp.transpose(jnp.concatenate([sel_t, jnp.zeros((LANE - SEL_ROWS, srows), F32)], axis=0))
    for i, (b, j) in enumerate(pairs):
        sel_ref[b, j] = sel[i * tq:(i + 1) * tq]


def cmp_select(q, kvc, cover, bn, tq, pos_base):
    n, _, t, _ = q.shape
    assert L_SEL == 64 and tq & (tq - 1) == 0
    return pl.pallas_call(
        functools.partial(_cmp_select_body, bn=bn, tq=tq, pos_base=pos_base), grid=(n // bn, t // tq),
        in_specs=[pl.BlockSpec((bn, N_Q_HEADS, tq, HEAD_DIM), lambda b, i: (b, 0, i, 0)),
                  pl.BlockSpec((bn, N_CMP_BLK * N_SLOT, HEAD_DIM), lambda b, i: (b, 0, 0)),
                  pl.BlockSpec((N_CMP_BLK, LANE), lambda b, i: (0, 0))],
        out_specs=[pl.BlockSpec((bn, N_Q_HEADS, tq, HEAD_DIM), lambda b, i: (b, 0, i, 0)),
                   pl.BlockSpec((bn, N_KV_HEADS, tq, LANE), lambda b, i: (b, 0, i, 0))],
        out_shape=[jax.ShapeDtypeStruct((n, N_Q_HEADS, t, HEAD_DIM), F32),
                   jax.ShapeDtypeStruct((n, N_KV_HEADS, t, LANE), F32)],
        compiler_params=_cparams("arbitrary", "arbitrary"), name="cmp_select",
    )(q, kvc, cover)


SOFTMAX_ROWS = 16


def _attn_prompt_body(*refs, tq, t, selected):
    if selected:
        q_ref, k_ref, v_ref, sel_ref, e_ref, o_ref, s_ref, bias_ref, p_ref, inv_ref = refs
    else:
        q_ref, k_ref, v_ref, o_ref, s_ref, bias_ref, p_ref, inv_ref = refs
    qt = pl.program_id(2)
    rows = GQA * tq
    q = q_ref[0].reshape(rows, HEAD_DIM)

    def attend(start, nk):
        k = k_ref[0, 0, pl.ds(start, nk), :]
        v = v_ref[0, 0, pl.ds(start, nk), :]
        s_ref[:, 0:nk] = _dot_nt(q, k)
        qpos = qt * tq + lax.broadcasted_iota(jnp.int32, (tq, nk), 0)
        kpos = start + lax.broadcasted_iota(jnp.int32, (tq, nk), 1)
        ok = kpos <= qpos
        if selected:
            ok = ok & (_dot(sel_ref[0, 0].astype(BF16), e_ref[:, 0:nk]) > 0.5)
        else:
            ok = ok & (kpos > qpos - WINDOW)
        bias_ref[:, 0:nk] = jnp.where(ok, 0.0, NEG_INF)

        for r0 in range(0, rows, SOFTMAX_ROWS):
            b0 = r0 % tq
            sb = s_ref[r0:r0 + SOFTMAX_ROWS, 0:nk] * (SCALE * LOG2_E) + bias_ref[b0:b0 + SOFTMAX_ROWS, 0:nk]
            m = jnp.max(sb, axis=-1, keepdims=True)
            e = jnp.exp2(sb - m)
            p_ref[r0:r0 + SOFTMAX_ROWS, 0:nk] = e.astype(BF16)
            total = jnp.maximum(jnp.sum(e, axis=-1, keepdims=True), TINY)
            inv_ref[r0:r0 + SOFTMAX_ROWS, :] = jnp.broadcast_to(jnp.where(m > 0.5 * NEG_INF, 1.0 / total, 0.0),
                                                                (SOFTMAX_ROWS, HEAD_DIM))
        o_ref[0] = (_dot(p_ref[:, 0:nk], v) * inv_ref[...]).reshape(GQA, tq, HEAD_DIM)

    if selected:
        n_bucket = t // WINDOW
        per = WINDOW // tq
        for b in range(n_bucket):
            @pl.when(qt // per == b)
            def _(b=b):
                attend(0, (b + 1) * WINDOW)
    else:
        attend(pl.multiple_of(jnp.maximum(qt * tq - WINDOW, 0), tq), WINDOW + tq)


def attn_prompt(q, kv, sel, expand, tq=128):
    n, _, t, _ = q.shape
    selected = sel is not None
    in_specs = [pl.BlockSpec((1, GQA, tq, HEAD_DIM), lambda b, j, i: (b, j, i, 0)),
                pl.BlockSpec((1, 1, t, HEAD_DIM), lambda b, j, i: (b, j, 0, 0)),
                pl.BlockSpec((1, 1, t, HEAD_DIM), lambda b, j, i: (b, N_KV_HEADS + j, 0, 0))]
    args = [q, kv, kv]
    if selected:
        in_specs += [pl.BlockSpec((1, 1, tq, LANE), lambda b, j, i: (b, j, i, 0)),
                     pl.BlockSpec(expand.shape, lambda b, j, i: (0, 0))]
        args += [sel, expand]
    nk_max = t if selected else WINDOW + tq
    return pl.pallas_call(
        functools.partial(_attn_prompt_body, tq=tq, t=t, selected=selected), grid=(n, N_KV_HEADS, t // tq),
        in_specs=in_specs,
        out_specs=pl.BlockSpec((1, GQA, tq, HEAD_DIM), lambda b, j, i: (b, j, i, 0)),
        out_shape=jax.ShapeDtypeStruct((n, N_Q_HEADS, t, HEAD_DIM), F32),
        scratch_shapes=[pltpu.VMEM((GQA * tq, nk_max), F32), pltpu.VMEM((tq, nk_max), F32),
                        pltpu.VMEM((GQA * tq, nk_max), BF16), pltpu.VMEM((GQA * tq, HEAD_DIM), F32)],
        compiler_params=_cparams("arbitrary", "arbitrary", "arbitrary"),
        name="attn_prompt_sel" if selected else "attn_prompt_win",
    )(*args)


T_PAD = 8


def _pad_keys(x):
    return jnp.concatenate([x, jnp.zeros((PAGE_SIZE - x.shape[0], x.shape[1]), x.dtype)], axis=0)


SEQS_PER_STEP = 2


def _attn_sample_sel_body(pt_ref, *refs, t_real):
    del pt_ref
    n_page_refs = SEQS_PER_STEP * N_PAGES
    q_ref, sel_ref, new_ref, e_ref, o_ref = refs[n_page_refs:]
    rows = GQA * T_PAD
    tok = lax.broadcasted_iota(jnp.int32, (rows, PAGE_SIZE), 0) & (T_PAD - 1)
    col = lax.broadcasted_iota(jnp.int32, (rows, PAGE_SIZE), 1)
    for b, j in [(b, j) for b in range(SEQS_PER_STEP) for j in range(N_KV_HEADS)]:
        pages = refs[b * N_PAGES:(b + 1) * N_PAGES]
        kc0, vc0 = j * HEAD_DIM, KV_DIM + j * HEAD_DIM
        q = q_ref[b, j * GQA:(j + 1) * GQA].reshape(rows, HEAD_DIM).astype(BF16)
        k_rows = pl.ds(j, PAGE_SIZE, stride=N_SLOT)
        v_rows = pl.ds(N_KV_HEADS + j, PAGE_SIZE, stride=N_SLOT)
        parts = [_dot_nt(q, pages[p][k_rows, :].astype(BF16)) for p in range(N_PAGES)]
        parts.append(_dot_nt(q, _pad_keys(new_ref[b, :, kc0:kc0 + HEAD_DIM]).astype(BF16)))
        s = jnp.concatenate(parts, axis=1) * SCALE
        sel = sel_ref[b, j]
        sel4 = jnp.concatenate([sel] * GQA, axis=0)
        picked = _dot(sel4.astype(BF16), e_ref[...])
        new_blk = PAST_LEN // L_SEL
        new_ok = (col < t_real) & (col <= tok)
        new_picked = jnp.where(new_ok, _lane_pick(sel4, new_blk), 0.0)
        mask = jnp.concatenate([picked, new_picked], axis=1) > 0.5
        p = _masked_softmax(s, mask).astype(BF16)
        o = _dot(p[:, PAST_LEN:], _pad_keys(new_ref[b, :, vc0:vc0 + HEAD_DIM]).astype(BF16))
        for pg in range(N_PAGES):
            o = o + _dot(p[:, pg * PAGE_SIZE:(pg + 1) * PAGE_SIZE], pages[pg][v_rows, :].astype(BF16))
        o_ref[b, j * GQA:(j + 1) * GQA] = o.reshape(GQA, T_PAD, HEAD_DIM)


def attn_sample_sel(pool, page_table, q, sel, new_rows, expand, t_real):
    n = page_table.shape[0]
    cols = new_rows.shape[-1]
    bs = SEQS_PER_STEP
    assert n % bs == 0
    page_spec = lambda b, p: pl.BlockSpec((PAGE_SIZE * N_SLOT, HEAD_DIM), lambda i, pt, b=b, p=p: (pt[bs * i + b, p], 0))
    return pl.pallas_call(
        functools.partial(_attn_sample_sel_body, t_real=t_real),
        grid_spec=pltpu.PrefetchScalarGridSpec(
            num_scalar_prefetch=1, grid=(n // bs,),
            in_specs=[page_spec(b, p) for b in range(bs) for p in range(N_PAGES)]
            + [pl.BlockSpec((bs, N_Q_HEADS, T_PAD, HEAD_DIM), lambda i, pt: (i, 0, 0, 0)),
               pl.BlockSpec((bs, N_KV_HEADS, T_PAD, LANE), lambda i, pt: (i, 0, 0, 0)),
               pl.BlockSpec((bs, T_PAD, cols), lambda i, pt: (i, 0, 0)),
               pl.BlockSpec(expand.shape, lambda i, pt: (0, 0))],
            out_specs=pl.BlockSpec((bs, N_Q_HEADS, T_PAD, HEAD_DIM), lambda i, pt: (i, 0, 0, 0))),
        out_shape=jax.ShapeDtypeStruct((n, N_Q_HEADS, T_PAD, HEAD_DIM), F32),
        compiler_params=_cparams("arbitrary"), name="attn_sample_sel",
    )(page_table, *([pool] * (bs * N_PAGES)), q, sel, new_rows, expand)


def _attn_sample_win_body(win_ref, q_ref, new_ref, newrows_ref, o_ref, wout_ref, *, t_real):
    rows = GQA * T_PAD
    buf_rows = win_ref.shape[0] // SEQS_PER_STEP
    w_buf = buf_rows // N_SLOT
    tok_o = lax.broadcasted_iota(jnp.int32, (rows, w_buf), 0) & (T_PAD - 1)
    col_o = lax.broadcasted_iota(jnp.int32, (rows, w_buf), 1)
    tok_n = lax.broadcasted_iota(jnp.int32, (rows, PAGE_SIZE), 0) & (T_PAD - 1)
    col_n = lax.broadcasted_iota(jnp.int32, (rows, PAGE_SIZE), 1)
    old_ok = jnp.where(col_o + (WINDOW - w_buf) > tok_o, 1.0, 0.0)
    new_ok = jnp.where((col_n < t_real) & (col_n <= tok_n), 1.0, 0.0)
    mask = jnp.concatenate([old_ok, new_ok], axis=1) > 0.5
    for b, j in [(b, j) for b in range(SEQS_PER_STEP) for j in range(N_KV_HEADS)]:
        kc0, vc0 = j * HEAD_DIM, KV_DIM + j * HEAD_DIM
        base = b * buf_rows
        q = q_ref[b, j * GQA:(j + 1) * GQA].reshape(rows, HEAD_DIM).astype(BF16)
        k_new = _pad_keys(new_ref[b, :, kc0:kc0 + HEAD_DIM]).astype(BF16)
        v_new = _pad_keys(new_ref[b, :, vc0:vc0 + HEAD_DIM]).astype(BF16)
        k_old = win_ref[pl.ds(base + j, w_buf, stride=N_SLOT), :].astype(BF16)
        v_old = win_ref[pl.ds(base + N_KV_HEADS + j, w_buf, stride=N_SLOT), :].astype(BF16)
        s = jnp.concatenate([_dot_nt(q, k_old), _dot_nt(q, k_new)], axis=1) * SCALE
        p = _masked_softmax(s, mask).astype(BF16)
        o = _dot(p[:, :w_buf], v_old) + _dot(p[:, w_buf:], v_new)
        o_ref[b, j * GQA:(j + 1) * GQA] = o.reshape(GQA, T_PAD, HEAD_DIM)
    keep = (w_buf - t_real) * N_SLOT
    new_n = t_real * N_SLOT
    for b in range(SEQS_PER_STEP):
        base = b * buf_rows
        wout_ref[base:base + keep, :] = win_ref[base + new_n:base + buf_rows, :]
        wout_ref[base + keep:base + buf_rows, :] = newrows_ref[b * new_n:(b + 1) * new_n, :]


def attn_sample_win(win, q, new_rows, new_cache_rows, t_real):
    n = q.shape[0]
    cols = new_rows.shape[-1]
    bs = SEQS_PER_STEP
    assert n % bs == 0
    buf_rows = bs * (win.shape[0] // n)
    return pl.pallas_call(
        functools.partial(_attn_sample_win_body, t_real=t_real), grid=(n // bs,),
        in_specs=[pl.BlockSpec((buf_rows, HEAD_DIM), lambda i: (i, 0)),
                  pl.BlockSpec((bs, N_Q_HEADS, T_PAD, HEAD_DIM), lambda i: (i, 0, 0, 0)),
                  pl.BlockSpec((bs, T_PAD, cols), lambda i: (i, 0, 0)),
                  pl.BlockSpec((bs * t_real * N_SLOT, HEAD_DIM), lambda i: (i, 0))],
        out_specs=[pl.BlockSpec((bs, N_Q_HEADS, T_PAD, HEAD_DIM), lambda i: (i, 0, 0, 0)),
                   pl.BlockSpec((buf_rows, HEAD_DIM), lambda i: (i, 0))],
        out_shape=[jax.ShapeDtypeStruct((n, N_Q_HEADS, T_PAD, HEAD_DIM), F32),
                   jax.ShapeDtypeStruct(win.shape, F32)],
        compiler_params=_cparams("arbitrary"), name="attn_sample_win",
    )(win, q, new_rows, new_cache_rows)


def _combine_body(oc_ref, os_ref, ow_ref, gl_ref, g_ref, out_ref, *, rows):
    gates = jax.nn.sigmoid(gl_ref[...])
    outs = []
    sq = jnp.zeros((rows, 1), F32)
    for h in range(N_Q_HEADS):
        a = (_lane_pick(gates, h) * oc_ref[:, h].reshape(rows, HEAD_DIM)
             + _lane_pick(gates, N_Q_HEADS + h) * os_ref[:, h].reshape(rows, HEAD_DIM)
             + _lane_pick(gates, 2 * N_Q_HEADS + h) * ow_ref[:, h].reshape(rows, HEAD_DIM))
        outs.append(a)
        sq = sq + jnp.sum(a * a, axis=-1, keepdims=True)
    inv = lax.rsqrt(sq * (1.0 / ATTN_DIM) + RMS_EPS)
    for h in range(N_Q_HEADS):
        sl = slice(h * HEAD_DIM, (h + 1) * HEAD_DIM)
        out_ref[:, sl] = ((outs[h] * inv) * g_ref[:, sl]).astype(out_ref.dtype)


def combine(o_cmp, o_sel, o_win, gate_logits, gain, bn, tt):
    n, _, t, _ = o_cmp.shape
    nt = t // tt
    rows = bn * tt
    o_spec = pl.BlockSpec((bn, N_Q_HEADS, tt, HEAD_DIM), lambda b, i: (b, 0, i, 0))
    return pl.pallas_call(
        functools.partial(_combine_body, rows=rows), grid=(n // bn, nt),
        in_specs=[o_spec, o_spec, o_spec,
                  pl.BlockSpec((rows, LANE), lambda b, i: (b * nt + i, 0)),
                  pl.BlockSpec((1, ATTN_DIM), lambda b, i: (0, 0))],
        out_specs=pl.BlockSpec((rows, ATTN_DIM), lambda b, i: (b * nt + i, 0)),
        out_shape=jax.ShapeDtypeStruct((n * t, ATTN_DIM), BF16),
        compiler_params=_cparams("arbitrary", "arbitrary"), name="combine",
    )(o_cmp, o_sel, o_win, gate_logits, gain.reshape(1, ATTN_DIM))


TOKEN_CHUNKS = D_MODEL // LANE
TOKEN_PITCH = 40


def _router_body(h_ref, g_ref, w_ref, b_ref, xrows_ref, ids_ref, wts_ref):
    tm = h_ref.shape[0]
    h = h_ref[...]
    x = (h * lax.rsqrt(jnp.mean(h * h, axis=-1, keepdims=True) + RMS_EPS)) * g_ref[...]
    for c in range(TOKEN_PITCH):
        piece = x[:, c * LANE:(c + 1) * LANE] if c < TOKEN_CHUNKS else jnp.zeros((tm, LANE), F32)
        xrows_ref[pl.ds(c, tm, stride=TOKEN_PITCH), :] = piece
    logits = _dot(x.astype(BF16), w_ref[...].astype(BF16)) + b_ref[...]
    lane = lax.broadcasted_iota(jnp.int32, logits.shape, 1)
    is_grp = (lane >= N_EXPERTS) & (lane < N_EXPERTS + N_GROUPS)
    gl = jnp.where(is_grp, logits, NEG_INF)
    ge = jnp.where(is_grp, jnp.exp(gl - jnp.max(gl, axis=-1, keepdims=True)), 0.0)
    p_grp = ge / jnp.sum(ge, axis=-1, keepdims=True)
    g_val = jnp.max(p_grp, axis=-1, keepdims=True)
    g_idx = jnp.min(jnp.where(is_grp & (p_grp == g_val), lane, 2 * LANE), axis=-1, keepdims=True) - N_EXPERTS
    lo = g_idx * EXPERTS_PER_GROUP
    in_grp = (lane >= lo) & (lane < lo + EXPERTS_PER_GROUP)
    el = jnp.where(in_grp, logits, NEG_INF)
    ee = jnp.where(in_grp, jnp.exp(el - jnp.max(el, axis=-1, keepdims=True)), 0.0)
    p_e = ee / jnp.sum(ee, axis=-1, keepdims=True)
    cand = jnp.where(in_grp, p_e, -1.0)
    e1 = jnp.max(cand, axis=-1, keepdims=True)
    i1 = jnp.min(jnp.where(cand == e1, lane, 2 * LANE), axis=-1, keepdims=True)
    cand = jnp.where(lane == i1, -1.0, cand)
    e2 = jnp.max(cand, axis=-1, keepdims=True)
    i2 = jnp.min(jnp.where(cand == e2, lane, 2 * LANE), axis=-1, keepdims=True)
    tot = e1 + e2
    ids_ref[...] = jnp.where(lane == 0, i1, jnp.where(lane == 1, i2, 0))
    wts_ref[...] = jnp.where(lane == 0, g_val * e1 / tot, jnp.where(lane == 1, g_val * e2 / tot, 0.0))


def router(h, gain, w_router, b_router, tm=256):
    m, d = h.shape
    out_spec = pl.BlockSpec((tm, LANE), lambda i: (i, 0))
    return pl.pallas_call(
        _router_body, grid=(m // tm,),
        in_specs=[pl.BlockSpec((tm, d), lambda i: (i, 0)),
                  pl.BlockSpec((1, d), lambda i: (0, 0)),
                  pl.BlockSpec((d, LANE), lambda i: (0, 0)),
                  pl.BlockSpec((1, LANE), lambda i: (0, 0))],
        out_specs=[pl.BlockSpec((tm * TOKEN_PITCH, LANE), lambda i: (i, 0)), out_spec, out_spec],
        out_shape=[jax.ShapeDtypeStruct((m * TOKEN_PITCH, LANE), F32),
                   jax.ShapeDtypeStruct((m, LANE), jnp.int32), jax.ShapeDtypeStruct((m, LANE), F32)],
        compiler_params=_cparams("arbitrary"), name="router",
    )(h, gain.reshape(1, d), w_router, b_router)


def _gather_rows(idx_ref, n_rows, src_hbm, dst, sem, wait):
    def copy(r, src):
        return pltpu.make_async_copy(src_hbm.at[pl.ds(src, 1), :], dst.at[pl.ds(r, 1), :], sem)

    if wait:
        def body(r, c):
            copy(r, 0).wait()
            return c
        lax.fori_loop(0, n_rows, body, 0, unroll=8)
    else:
        for r in range(n_rows):
            copy(r, idx_ref[0, 0, r]).start(priority=r % 2)


def _gather_tokens(idx_ref, tokens, src_hbm, dst, base, sem, wait):
    def copy(r, tok):
        return pltpu.make_async_copy(src_hbm.at[pl.ds(pl.multiple_of(tok * TOKEN_PITCH, 8), TOKEN_CHUNKS), :],
                                     dst.at[pl.ds(pl.multiple_of(base + r * TOKEN_PITCH, 8), TOKEN_CHUNKS), :], sem)

    if wait:
        def body(r, c):
            copy(r, 0).wait()
            return c
        lax.fori_loop(tokens.start, tokens.stop, body, 0, unroll=8)
    else:
        for r in tokens:
            copy(r, idx_ref[0, 0, r]).start(priority=r % 2)


def _moe_ffn_body(te_ref, cur_ref, nxt_ref, x_hbm, rw_ref, wg_ref, wu_ref, wd_ref, ys_ref, xbuf, sem, *, tm):
    del te_ref
    i = pl.program_id(0)
    n = pl.num_programs(0)
    slot = lax.rem(i, 2)
    half = tm * TOKEN_PITCH
    here, other = slot * half, (1 - slot) * half

    @pl.when(i == 0)
    def _():
        _gather_tokens(cur_ref, range(tm), x_hbm, xbuf, 0, sem.at[0], wait=False)

    _gather_tokens(cur_ref, range(tm), x_hbm, xbuf, here, sem.at[slot], wait=True)

    _gather_tokens(nxt_ref, range(tm), x_hbm, xbuf, other, sem.at[1 - slot], wait=False)
    x = jnp.concatenate([xbuf[pl.ds(here + c, tm, stride=TOKEN_PITCH), :].astype(BF16) for c in range(TOKEN_CHUNKS)],
                        axis=1)
    hid = jax.nn.silu(_dot(x, wg_ref[0])) * _dot(x, wu_ref[0])
    gate = jnp.concatenate([rw_ref[...]] * (hid.shape[1] // LANE), axis=1)
    ys_ref[...] = _dot((hid * gate).astype(BF16), wd_ref[0])

    @pl.when(i == n - 1)
    def _():
        _gather_tokens(cur_ref, range(tm), x_hbm, xbuf, other, sem.at[1 - slot], wait=True)


def moe_ffn(x_rows, tile_expert, row_token, row_weight, w_gate, w_up, w_down, tm):
    n_tiles = tile_expert.shape[0]
    d = TOKEN_CHUNKS * LANE
    f = w_gate.shape[-1]
    smem_rows = lambda fn: pl.BlockSpec((1, 1, tm), fn, memory_space=pltpu.SMEM)
    return pl.pallas_call(
        functools.partial(_moe_ffn_body, tm=tm),
        grid_spec=pltpu.PrefetchScalarGridSpec(
            num_scalar_prefetch=1, grid=(n_tiles,),
            in_specs=[smem_rows(lambda i, te: (i, 0, 0)),
                      smem_rows(lambda i, te: (jnp.minimum(i + 1, n_tiles - 1), 0, 0)),
                      pl.BlockSpec(memory_space=pl.ANY),
                      pl.BlockSpec((tm, LANE), lambda i, te: (i, 0)),
                      pl.BlockSpec((1, d, f), lambda i, te: (te[i], 0, 0)),
                      pl.BlockSpec((1, d, f), lambda i, te: (te[i], 0, 0)),
                      pl.BlockSpec((1, f, d), lambda i, te: (te[i], 0, 0))],
            out_specs=pl.BlockSpec((tm, d), lambda i, te: (i, 0)),
            scratch_shapes=[pltpu.VMEM((2 * tm * TOKEN_PITCH, LANE), F32), pltpu.SemaphoreType.DMA((2,))]),
        out_shape=jax.ShapeDtypeStruct((n_tiles * tm, d), F32),
        compiler_params=_cparams("arbitrary"), name="moe_ffn",
    )(tile_expert, row_token, row_token, x_rows, row_weight, w_gate, w_up, w_down)


def _moe_combine_body(cur_ref, nxt_ref, ys_hbm, h_ref, ya_ref, yb_ref, buf, sem, *, tiles_a):
    i = pl.program_id(0)
    n = pl.num_programs(0)
    rows = buf.shape[1]
    tm = h_ref.shape[0]
    slot = lax.rem(i, 2)

    @pl.when(i == 0)
    def _():
        _gather_rows(cur_ref, rows, ys_hbm, buf.at[0], sem.at[0], wait=False)

    _gather_rows(cur_ref, rows, ys_hbm, buf.at[slot], sem.at[slot], wait=True)
    _gather_rows(nxt_ref, rows, ys_hbm, buf.at[1 - slot], sem.at[1 - slot], wait=False)
    y = h_ref[...] + buf[slot, 0:tm, :] + buf[slot, tm:rows, :]

    @pl.when(i == n - 1)
    def _():
        _gather_rows(cur_ref, rows, ys_hbm, buf.at[1 - slot], sem.at[1 - slot], wait=True)

    @pl.when(i < tiles_a)
    def _():
        ya_ref[...] = y

    @pl.when(i >= tiles_a)
    def _():
        yb_ref[...] = y


def moe_combine(ys, pair_row, h, rows_a, tm=128):
    m, d = h.shape
    n_tiles = m // tm
    tiles_a = rows_a // tm
    assert rows_a % tm == 0 and 0 < tiles_a < n_tiles
    smem_rows = lambda fn: pl.BlockSpec((1, 1, 2 * tm), fn, memory_space=pltpu.SMEM)
    return pl.pallas_call(
        functools.partial(_moe_combine_body, tiles_a=tiles_a), grid=(n_tiles,),
        in_specs=[smem_rows(lambda i: (i, 0, 0)),
                  smem_rows(lambda i: (jnp.minimum(i + 1, n_tiles - 1), 0, 0)),
                  pl.BlockSpec(memory_space=pl.ANY),
                  pl.BlockSpec((tm, d), lambda i: (i, 0))],
        out_specs=[pl.BlockSpec((tm, d), lambda i: (jnp.minimum(i, tiles_a - 1), 0)),
                   pl.BlockSpec((tm, d), lambda i: (jnp.maximum(i - tiles_a, 0), 0))],
        out_shape=[jax.ShapeDtypeStruct((rows_a, d), F32), jax.ShapeDtypeStruct((m - rows_a, d), F32)],
        scratch_shapes=[pltpu.VMEM((2, 2 * tm, d), F32), pltpu.SemaphoreType.DMA((2,))],
        compiler_params=_cparams("arbitrary"), name="moe_combine",
    )(pair_row, pair_row, ys, h)


def moe_routed(xt, ids, wts, h, w_gate, w_up, w_down, tm, rows_a):
    m = h.shape[0]
    n_pair = 2 * m
    n_tiles = (n_pair + N_EXPERTS * (tm - 1)) // tm + 1
    flat_e = ids[:, :2].reshape(n_pair)
    flat_w = wts[:, :2].reshape(n_pair)
    order = jnp.argsort(flat_e, stable=True).astype(jnp.int32)
    rank = jnp.argsort(order).astype(jnp.int32)
    counts = jnp.sum(flat_e[:, None] == jnp.arange(N_EXPERTS, dtype=jnp.int32)[None, :], axis=0, dtype=jnp.int32)
    padded = ((counts + tm - 1) // tm) * tm
    pad_end = jnp.cumsum(padded)
    pad_start = pad_end - padded
    start = jnp.cumsum(counts) - counts
    tile_start = jnp.arange(n_tiles, dtype=jnp.int32) * tm
    tile_expert = jnp.minimum(jnp.sum(tile_start[:, None] >= pad_end[None, :], axis=1), N_EXPERTS - 1).astype(jnp.int32)
    row_e = jnp.repeat(tile_expert, tm)
    offs = jnp.arange(n_tiles * tm, dtype=jnp.int32) - pad_start[row_e]
    used = offs < counts[row_e]
    src = order[jnp.clip(start[row_e] + offs, 0, n_pair - 1)]
    row_token = jnp.where(used, src // 2, 0)
    row_weight = jnp.where(used, flat_w[src], 0.0)
    pair_row = (pad_start[flat_e] + rank - start[flat_e]).reshape(m, 2)
    ys = moe_ffn(xt, tile_expert, row_token.reshape(n_tiles, 1, tm),
                 jnp.broadcast_to(row_weight[:, None], (n_tiles * tm, LANE)), w_gate, w_up, w_down, tm)
    tc = 128
    pair_tiles = pair_row.reshape(m // tc, tc, 2).transpose(0, 2, 1).reshape(m // tc, 1, 2 * tc)
    return moe_combine(ys, pair_tiles, h, rows_a, tm=tc)


def _cover_matrix(nsb):
    i = np.arange(N_CMP_BLK)[:, None]
    j = np.arange(LANE)[None, :]
    m = np.zeros((N_CMP_BLK, LANE), np.float32)
    for a in range(L_SEL // CMP_STRIDE):
        for c in range(R_CMP):
            m += (i == (L_SEL // CMP_STRIDE) * j + a - c)
    m[N_CMP_BLK - 1:, :] = 0.0
    m[:, nsb:] = 0.0
    assert nsb <= SEL_ROWS
    return jnp.asarray(m.T, BF16)


def _expand_matrix(n_keys):
    b = np.arange(LANE)[:, None]
    k = np.arange(n_keys)[None, :]
    return jnp.asarray((k // L_SEL == b).astype(np.float32), BF16)


def _to_heads(q, n, t):
    return q.reshape(n, t, N_Q_HEADS, HEAD_DIM).transpose(0, 2, 1, 3)


def kernel(x_prompt, x_sample, cache_cmp_kv, cache_sel_kv, state_win_kv, state_conv, page_table, norm_mix_g, w_in,
           conv_w, q_norm_g, k_norm_g, phi_pe, phi_w1, phi_w2, out_norm_g, w_out, norm_ffn_g, w_group_router,
           b_group_router, w_expert_router, b_expert_router, w_gate, w_up, w_down):
    n_p, t_p, d = x_prompt.shape
    n_s, t_s, _ = x_sample.shape
    assert w_in.shape[0] == 1 and t_s < CMP_STRIDE and t_s <= T_PAD and t_p % WINDOW == 0
    kv_cols = 2 * KV_DIM
    w_in_t = jnp.transpose(w_in[0])
    w_in_gate_t = jnp.pad(w_in_t[Z_MAIN:], ((0, LANE - 3 * N_Q_HEADS), (0, 0)))
    w_o = w_out
    wkv = phi_w1[0].reshape(2, R_CMP, CMP_STRIDE, HEAD_DIM, HEAD_DIM).transpose(2, 0, 3, 1, 4)
    wkv = wkv.reshape(CMP_STRIDE * 2 * HEAD_DIM, R_CMP * HEAD_DIM)
    pe5 = phi_pe[0].reshape(2, R_CMP, CMP_STRIDE, 1, HEAD_DIM)
    pe_kv = (pe5 * jnp.eye(2, dtype=F32).reshape(2, 1, 1, 2, 1)).reshape(2, R_CMP, CMP_STRIDE * 2 * HEAD_DIM)
    w2cat = phi_w2[0].reshape(2 * HEAD_DIM, HEAD_DIM)
    w_router = jnp.pad(jnp.concatenate([w_expert_router[0], w_group_router[0]], axis=1),
                       ((0, 0), (0, LANE - N_EXPERTS - N_GROUPS)))
    b_router = jnp.pad(jnp.concatenate([b_expert_router[0], b_group_router[0]]),
                       (0, LANE - N_EXPERTS - N_GROUPS)).reshape(1, LANE)
    wg_bf, wu_bf, wd_bf = w_gate[0].astype(BF16), w_up[0].astype(BF16), w_down[0].astype(BF16)

    def project(x2d):
        xn = rmsnorm_cast(x2d, norm_mix_g[0])
        z = matmul([([xn], None)], [(w_in_t, 0)], tm=1024, tn=512, w_transposed=True)
        gate_logits = matmul([([xn], None)], [(w_in_gate_t, 0)], tn=LANE, w_transposed=True)
        return z, gate_logits

    m_p = n_p * t_p

    xp = x_prompt.reshape(n_p * t_p, d)
    z, glog = project(xp)
    qh, ks_h, kw_h, kvc, kvs, kvw = postproj_prompt(z, n_p, t_p, q_norm_g[0], k_norm_g[0])
    conv_out, conv_last = conv_prompt(z, n_p, t_p, conv_w[0], out_norm_g[0][:CONV_DIM])
    ident = jnp.arange(n_p * (t_p // PAGE_SIZE), dtype=jnp.int32).reshape(n_p, t_p // PAGE_SIZE)
    kv_cmp = compress(kvc, ident, wkv, pe_kv, w2cat, k_norm_g[0])
    o_cmp, sel = cmp_select(qh, kv_cmp, _cover_matrix(t_p // L_SEL), bn=1, tq=256, pos_base=0)
    o_sel = attn_prompt(qh, ks_h, sel, _expand_matrix(t_p))
    o_win = attn_prompt(qh, kw_h, None, None)
    attn_out = combine(o_cmp, o_sel, o_win, glog, out_norm_g[0][CONV_DIM:], bn=1, tt=256)
    mixed_p = ([conv_out, attn_out], xp)
    kv_shape = (1, n_p, t_p, 2, N_KV_HEADS, HEAD_DIM)
    w_keep = min(WINDOW, t_p)
    prompt_win = kvw.reshape(kv_shape)[:, :, t_p - w_keep:]
    prompt_conv = conv_last[:, 8 - (CONV_WIDTH - 1):, :][None]

    xs = x_sample.reshape(n_s * t_s, d)
    z, glog = project(xs)
    q, kvc_s, kvs_s, kvw_s = postproj(z, q_norm_g[0], k_norm_g[0])
    conv_out, conv_state = conv_sample(z, n_s, t_s, state_conv[0], conv_w[0], out_norm_g[0][:CONV_DIM])
    pad_t = lambda a: jnp.pad(a, ((0, 0), (0, T_PAD - t_s), (0, 0)))
    qh = jnp.pad(_to_heads(q, n_s, t_s), ((0, 0), (0, 0), (0, T_PAD - t_s), (0, 0)))
    pool_cmp = cache_cmp_kv[0].reshape(-1, HEAD_DIM)
    pool_sel = cache_sel_kv[0].reshape(-1, HEAD_DIM)
    kv_cmp = compress(pool_cmp, page_table, wkv, pe_kv, w2cat, k_norm_g[0])
    o_cmp, sel = cmp_select(qh, kv_cmp, _cover_matrix(PAST_LEN // L_SEL + 1), bn=16, tq=T_PAD, pos_base=PAST_LEN)
    o_sel = attn_sample_sel(pool_sel, page_table, qh, sel, pad_t(kvs_s.reshape(n_s, t_s, kv_cols)),
                            _expand_matrix(PAST_LEN), t_s)
    win = state_win_kv[0].reshape(-1, HEAD_DIM)
    o_win, win_new = attn_sample_win(win, qh, pad_t(kvw_s.reshape(n_s, t_s, kv_cols)),
                                     kvw_s.reshape(-1, HEAD_DIM), t_s)
    glog_pad = pad_t(glog.reshape(n_s, t_s, LANE)).reshape(n_s * T_PAD, LANE)
    attn_out = combine(o_cmp, o_sel, o_win, glog_pad, out_norm_g[0][CONV_DIM:], bn=32, tt=T_PAD)
    attn_out = attn_out.reshape(n_s, T_PAD, ATTN_DIM)[:, :t_s].reshape(n_s * t_s, ATTN_DIM)
    s_shape = (1, n_s, t_s, 2, N_KV_HEADS, HEAD_DIM)

    h_all = matmul([mixed_p, ([conv_out, attn_out], xs)], [(w_o, 0), (w_o, 1)])
    x_rows, ids, wts = router(h_all, norm_ffn_g[0], w_router, b_router)
    y_prompt, y_sample = moe_routed(x_rows, ids, wts, h_all, wg_bf, wu_bf, wd_bf, 256, m_p)

    return (y_prompt.reshape(n_p, t_p, d), y_sample.reshape(n_s, t_s, d), kvc.reshape(kv_shape), kvs.reshape(kv_shape), prompt_win, prompt_conv,
            kvc_s.reshape(s_shape), kvs_s.reshape(s_shape),
            win_new.reshape(1, n_s, -1, 2, N_KV_HEADS, HEAD_DIM), conv_state[None])
```

```python
import functools

import numpy as np
import jax
import jax.numpy as jnp
from jax import lax
from jax.experimental import pallas as pl
from jax.experimental.pallas import tpu as pltpu

F32 = jnp.float32
BF16 = jnp.bfloat16

D_MODEL = 4096
PAST_LEN = 2048
PAGE_SIZE = 128
HEAD_DIM = 128
CONV_DIM = 2048
N_Q_HEADS = 16
N_KV_HEADS = 4
GQA = 4
ATTN_DIM = 2048
KV_DIM = 512
CONV_WIDTH = 3
L_CMP = 32
CMP_STRIDE = 16
R_CMP = 2
L_SEL = 64
N_SEL = 8
WINDOW = 512
FORCE_BONUS = 1e3
SCALE = HEAD_DIM ** -0.5
N_GROUPS = 4
EXPERTS_PER_GROUP = 4
N_EXPERTS = 16
D_FF_EXPERT = 512
RMS_EPS = 1e-6
NEG_INF = -1e30
TINY = 1e-30
PICKED = -3e38
LOG2_E = 1.4426950408889634

N_PAGES = PAST_LEN // PAGE_SIZE
N_CMP_BLK = 128
Z_MAIN = 3 * CONV_DIM + ATTN_DIM + 6 * KV_DIM
LANE = 128
VMEM_LIMIT = 56 * 1024 * 1024


def _cparams(*sem):
    return pltpu.CompilerParams(dimension_semantics=sem, vmem_limit_bytes=VMEM_LIMIT)


def _masked_softmax(s, mask):
    s = jnp.where(mask, s, NEG_INF)
    m = jnp.max(s, axis=-1, keepdims=True)
    e = jnp.where(mask, jnp.exp(s - m), 0.0)
    return e / jnp.maximum(jnp.sum(e, axis=-1, keepdims=True), TINY)


def _dot_nt(a, b):
    return lax.dot_general(a, b, (((1,), (1,)), ((), ())), preferred_element_type=F32)


def _dot(a, b):
    return jnp.dot(a, b, preferred_element_type=F32)


def _lane_pick(x, c):
    lane = lax.broadcasted_iota(jnp.int32, x.shape, 1)
    return jnp.sum(jnp.where(lane == c, x, 0.0), axis=-1, keepdims=True)


def _rmsnorm_body(x_ref, g_ref, o_ref):
    x = x_ref[...]
    inv = lax.rsqrt(jnp.mean(x * x, axis=-1, keepdims=True) + RMS_EPS)
    o_ref[...] = ((x * inv) * g_ref[...]).astype(o_ref.dtype)


def rmsnorm_cast(x, g, dtype=BF16, tm=256):
    m, d = x.shape
    return pl.pallas_call(
        _rmsnorm_body, grid=(m // tm,),
        in_specs=[pl.BlockSpec((tm, d), lambda i: (i, 0)), pl.BlockSpec((1, d), lambda i: (0, 0))],
        out_specs=pl.BlockSpec((tm, d), lambda i: (i, 0)),
        out_shape=jax.ShapeDtypeStruct((m, d), dtype),
        compiler_params=_cparams("arbitrary"), name="rmsnorm_cast",
    )(x, g.reshape(1, d))


def _matmul_body(*refs, n_w, has_res, tiles, w_transposed):
    per = n_w + (1 if has_res else 0)
    n_groups = len(tiles) - 1
    o_ref = refs[n_w + n_groups * per]
    wbf = refs[n_w + n_groups * per + 1:]
    i = pl.program_id(1)
    dot = _dot_nt if w_transposed else _dot

    @pl.when(i == 0)
    def _():
        for p in range(n_w):
            wbf[p][...] = refs[p][...].astype(BF16)

    for g in range(n_groups):
        grp = refs[n_w + g * per:n_w + (g + 1) * per]

        def compute(grp=grp):
            acc = dot(grp[0][...], wbf[0][...])
            for p in range(1, n_w):
                acc = acc + dot(grp[p][...], wbf[p][...])
            if has_res:
                acc = acc + grp[n_w][...]
            o_ref[...] = acc

        if n_groups == 1:
            compute()
        else:
            pl.when((i >= tiles[g]) & (i < tiles[g + 1]))(compute)


def matmul(groups, weights, tm=512, tn=512, w_transposed=False):
    ms = [g[0][0].shape[0] for g in groups]
    tm = min([tm] + ms)
    assert all(m % tm == 0 for m in ms)
    n = (weights[0][0].shape[0 if w_transposed else -1] // tn) * tn
    has_res = groups[0][1] is not None
    tiles = [0]
    for m in ms:
        tiles.append(tiles[-1] + m // tm)
    in_specs, args, scratch = [], [], []
    ks = [a.shape[1] for a in groups[0][0]]
    for (w, kb), k in zip(weights, ks):
        if w_transposed:
            in_specs.append(pl.BlockSpec((tn, k), lambda j, i, kb=kb: (j, kb)))
        elif w.ndim == 3:
            in_specs.append(pl.BlockSpec((None, k, tn), lambda j, i, kb=kb: (0, kb, j)))
        else:
            in_specs.append(pl.BlockSpec((k, tn), lambda j, i, kb=kb: (kb, j)))
        args.append(w)
        scratch.append(pltpu.VMEM((tn, k) if w_transposed else (k, tn), BF16))
    for g, (a_list, res) in enumerate(groups):
        lo, hi = tiles[g], tiles[g + 1]
        row = lambda i, lo=lo, hi=hi: jnp.clip(i, lo, hi - 1) - lo
        for a, k in zip(a_list, ks):
            in_specs.append(pl.BlockSpec((tm, k), lambda j, i, row=row: (row(i), 0)))
            args.append(a)
        if has_res:
            in_specs.append(pl.BlockSpec((tm, tn), lambda j, i, row=row: (row(i), j)))
            args.append(res)
    return pl.pallas_call(
        functools.partial(_matmul_body, n_w=len(weights), has_res=has_res, tiles=tuple(tiles),
                          w_transposed=w_transposed),
        grid=(n // tn, tiles[-1]), in_specs=in_specs,
        out_specs=pl.BlockSpec((tm, tn), lambda j, i: (i, j)),
        out_shape=jax.ShapeDtypeStruct((sum(ms), n), F32), scratch_shapes=scratch,
        compiler_params=_cparams("arbitrary", "arbitrary"), name="matmul",
    )(*args)


def _head_norm(x, g):
    inv = lax.rsqrt(jnp.mean(x * x, axis=-1, keepdims=True) + RMS_EPS)
    return (x * inv) * g


def _postproj_body(zq_ref, zc_ref, zs_ref, zw_ref, qg_ref, kg_ref, q_ref, kvc_ref, kvs_ref, kvw_ref):
    for h in range(N_Q_HEADS):
        sl = slice(h * HEAD_DIM, (h + 1) * HEAD_DIM)
        q_ref[:, sl] = _head_norm(zq_ref[:, sl], qg_ref[...]).astype(q_ref.dtype)
    kvc_ref[...] = zc_ref[...]
    for h in range(N_KV_HEADS):
        sl = slice(h * HEAD_DIM, (h + 1) * HEAD_DIM)
        kvs_ref[:, sl] = _head_norm(zs_ref[:, sl], kg_ref[1:2, :])
        kvw_ref[:, sl] = _head_norm(zw_ref[:, sl], kg_ref[2:3, :])
    kvs_ref[:, KV_DIM:] = zs_ref[:, KV_DIM:]
    kvw_ref[:, KV_DIM:] = zw_ref[:, KV_DIM:]


def postproj(z, q_norm_g, k_norm_g, tm=256):
    m = z.shape[0]
    kv = 2 * KV_DIM
    q0 = 3 * CONV_DIM // ATTN_DIM
    c0 = (3 * CONV_DIM + ATTN_DIM) // kv
    return pl.pallas_call(
        _postproj_body, grid=(m // tm,),
        in_specs=[pl.BlockSpec((tm, ATTN_DIM), lambda i: (i, q0)),
                  pl.BlockSpec((tm, kv), lambda i: (i, c0)),
                  pl.BlockSpec((tm, kv), lambda i: (i, c0 + 1)),
                  pl.BlockSpec((tm, kv), lambda i: (i, c0 + 2)),
                  pl.BlockSpec((1, HEAD_DIM), lambda i: (0, 0)),
                  pl.BlockSpec((3, HEAD_DIM), lambda i: (0, 0))],
        out_specs=[pl.BlockSpec((tm, ATTN_DIM), lambda i: (i, 0)),
                   pl.BlockSpec((tm, kv), lambda i: (i, 0)),
                   pl.BlockSpec((tm, kv), lambda i: (i, 0)),
                   pl.BlockSpec((tm, kv), lambda i: (i, 0))],
        out_shape=[jax.ShapeDtypeStruct((m, ATTN_DIM), F32)] + [jax.ShapeDtypeStruct((m, kv), F32)] * 3,
        compiler_params=_cparams("arbitrary"), name="postproj",
    )(z, z, z, z, q_norm_g.reshape(1, HEAD_DIM), k_norm_g)


N_SLOT = 2 * N_KV_HEADS


def _postproj_prompt_body(zq_ref, zc_ref, zs_ref, zw_ref, qg_ref, kg_ref,
                          q_ref, ks_ref, kw_ref, kvc_ref, kvs_ref, kvw_ref):
    tm = zq_ref.shape[0]
    for h in range(N_Q_HEADS):
        sl = slice(h * HEAD_DIM, (h + 1) * HEAD_DIM)
        q_ref[0, h] = _head_norm(zq_ref[:, sl], qg_ref[...]).astype(BF16)
    for c in range(N_SLOT):
        sl = slice(c * HEAD_DIM, (c + 1) * HEAD_DIM)
        rows = pl.ds(c, tm, stride=N_SLOT)
        kvc_ref[rows, :] = zc_ref[:, sl]
        xs, xw = zs_ref[:, sl], zw_ref[:, sl]
        if c < N_KV_HEADS:
            xs, xw = _head_norm(xs, kg_ref[1:2, :]), _head_norm(xw, kg_ref[2:3, :])
        kvs_ref[rows, :] = xs
        kvw_ref[rows, :] = xw
        ks_ref[0, c] = xs.astype(BF16)
        kw_ref[0, c] = xw.astype(BF16)


def postproj_prompt(z, n, t, q_norm_g, k_norm_g, tm=256):
    kv = 2 * KV_DIM
    nt = t // tm
    q0 = 3 * CONV_DIM // ATTN_DIM
    c0 = (3 * CONV_DIM + ATTN_DIM) // kv
    rows_spec = pl.BlockSpec((tm * N_SLOT, HEAD_DIM), lambda b, i: (b * nt + i, 0))
    slot_spec = pl.BlockSpec((1, N_SLOT, tm, HEAD_DIM), lambda b, i: (b, 0, i, 0))
    rows_sds = jax.ShapeDtypeStruct((n * t * N_SLOT, HEAD_DIM), F32)
    slot_sds = jax.ShapeDtypeStruct((n, N_SLOT, t, HEAD_DIM), BF16)
    return pl.pallas_call(
        _postproj_prompt_body, grid=(n, nt),
        in_specs=[pl.BlockSpec((tm, ATTN_DIM), lambda b, i: (b * nt + i, q0)),
                  pl.BlockSpec((tm, kv), lambda b, i: (b * nt + i, c0)),
                  pl.BlockSpec((tm, kv), lambda b, i: (b * nt + i, c0 + 1)),
                  pl.BlockSpec((tm, kv), lambda b, i: (b * nt + i, c0 + 2)),
                  pl.BlockSpec((1, HEAD_DIM), lambda b, i: (0, 0)),
                  pl.BlockSpec((3, HEAD_DIM), lambda b, i: (0, 0))],
        out_specs=[pl.BlockSpec((1, N_Q_HEADS, tm, HEAD_DIM), lambda b, i: (b, 0, i, 0)),
                   slot_spec, slot_spec, rows_spec, rows_spec, rows_spec],
        out_shape=[jax.ShapeDtypeStruct((n, N_Q_HEADS, t, HEAD_DIM), BF16), slot_sds, slot_sds,
                   rows_sds, rows_sds, rows_sds],
        compiler_params=_cparams("arbitrary", "arbitrary"), name="postproj_prompt",
    )(z, z, z, z, q_norm_g.reshape(1, HEAD_DIM), k_norm_g)


def _conv_finish(b, y, g):
    c = b * y
    inv = lax.rsqrt(jnp.mean(c * c, axis=-1, keepdims=True) + RMS_EPS)
    return ((c * inv) * g).astype(BF16)


def _conv_prompt_body(gb_ref, gc_ref, hc_ref, w_ref, g_ref, o_ref, st_ref, carry_ref):
    tt = gb_ref.shape[0]

    @pl.when(pl.program_id(1) == 0)
    def _():
        carry_ref[...] = jnp.zeros_like(carry_ref)

    u = gc_ref[...] * hc_ref[...]
    prev = carry_ref[...]
    p1, p2 = prev[7:8, :], prev[6:7, :]
    row = lax.broadcasted_iota(jnp.int32, u.shape, 0)
    u1 = jnp.where(row == 0, p1, pltpu.roll(u, 1, axis=0))
    u2 = jnp.where(row == 0, p2, jnp.where(row == 1, p1, pltpu.roll(u, 2, axis=0)))
    y = u2 * w_ref[0:1, :] + u1 * w_ref[1:2, :] + u * w_ref[2:3, :]
    o_ref[...] = _conv_finish(gb_ref[...], y, g_ref[...])
    last = u[tt - 8:tt, :]
    carry_ref[...] = last
    st_ref[0] = last


def conv_prompt(z, n, t, conv_w, gain, tt=256):
    nt = t // tt
    row = lambda b, i: (b * nt + i, 0)
    return pl.pallas_call(
        _conv_prompt_body, grid=(n, nt),
        in_specs=[pl.BlockSpec((tt, CONV_DIM), lambda b, i: (b * nt + i, 0)),
                  pl.BlockSpec((tt, CONV_DIM), lambda b, i: (b * nt + i, 1)),
                  pl.BlockSpec((tt, CONV_DIM), lambda b, i: (b * nt + i, 2)),
                  pl.BlockSpec((CONV_WIDTH, CONV_DIM), lambda b, i: (0, 0)),
                  pl.BlockSpec((1, CONV_DIM), lambda b, i: (0, 0))],
        out_specs=[pl.BlockSpec((tt, CONV_DIM), row),
                   pl.BlockSpec((1, 8, CONV_DIM), lambda b, i: (b, 0, 0))],
        out_shape=[jax.ShapeDtypeStruct((n * t, CONV_DIM), BF16), jax.ShapeDtypeStruct((n, 8, CONV_DIM), F32)],
        scratch_shapes=[pltpu.VMEM((8, CONV_DIM), F32)],
        compiler_params=_cparams("arbitrary", "arbitrary"), name="conv_prompt",
    )(z, z, z, conv_w, gain.reshape(1, CONV_DIM))


def _conv_sample_body(z_ref, pre_ref, w_ref, g_ref, o_ref, st_ref, *, t):
    up = [pre_ref[k] for k in range(CONV_WIDTH - 1)] + [z_ref[1, k] * z_ref[2, k] for k in range(t)]
    for k in range(t):
        y = up[k] * w_ref[0:1, :] + up[k + 1] * w_ref[1:2, :] + up[k + 2] * w_ref[2:3, :]
        o_ref[k] = _conv_finish(z_ref[0, k], y, g_ref[...])
    for k in range(CONV_WIDTH - 1):
        st_ref[k] = up[t + k]


def conv_sample(z, n, t, state, conv_w, gain):
    zt = z[:, :3 * CONV_DIM].reshape(n, t, 3, CONV_DIM).transpose(2, 1, 0, 3)
    whole = lambda shape: pl.BlockSpec(shape, lambda i: (0,) * len(shape))
    out, st = pl.pallas_call(
        functools.partial(_conv_sample_body, t=t), grid=(1,),
        in_specs=[whole((3, t, n, CONV_DIM)), whole((CONV_WIDTH - 1, n, CONV_DIM)),
                  whole((CONV_WIDTH, CONV_DIM)), whole((1, CONV_DIM))],
        out_specs=[whole((t, n, CONV_DIM)), whole((CONV_WIDTH - 1, n, CONV_DIM))],
        out_shape=[jax.ShapeDtypeStruct((t, n, CONV_DIM), BF16),
                   jax.ShapeDtypeStruct((CONV_WIDTH - 1, n, CONV_DIM), F32)],
        compiler_params=_cparams("arbitrary"), name="conv_sample",
    )(zt, state.transpose(1, 0, 2), conv_w, gain.reshape(1, CONV_DIM))
    return out.transpose(1, 0, 2).reshape(n * t, CONV_DIM), st.transpose(1, 0, 2)


def _compress_body(pt_ref, *refs):
    del pt_ref
    pages = refs[:N_PAGES]
    wkv_ref, pe_ref, w2_ref, kg_ref, out_ref = refs[N_PAGES:]
    cpp = PAGE_SIZE // CMP_STRIDE
    x4 = [pages[p][...].reshape(cpp, CMP_STRIDE, N_SLOT, HEAD_DIM) for p in range(N_PAGES)]
    n_rows = N_PAGES * cpp * N_SLOT
    is_key = (lax.broadcasted_iota(jnp.int32, (n_rows, HEAD_DIM), 0) & (N_SLOT - 1)) < N_KV_HEADS

    def split(x):
        return [jnp.where(is_key, x, 0.0).astype(BF16), jnp.where(is_key, 0.0, x).astype(BF16)]

    pieces = []
    for s in range(CMP_STRIDE):
        pieces += split(jnp.concatenate([x4[p][:, s].reshape(cpp * N_SLOT, HEAD_DIM) for p in range(N_PAGES)], axis=0))
    wkv = wkv_ref[...].astype(BF16)
    r = _dot(jnp.concatenate(pieces, axis=1), wkv)
    hpre = r[:, :HEAD_DIM] + pltpu.roll(r[:, HEAD_DIM:], n_rows - N_SLOT, axis=0)
    bias = []
    for v in range(2):
        b = jnp.zeros((8, HEAD_DIM), F32)
        for rr in range(R_CMP):
            pe = jnp.broadcast_to(pe_ref[v, rr:rr + 1, :], (8, wkv.shape[0])).astype(BF16)
            b = b + _dot(pe, wkv[:, rr * HEAD_DIM:(rr + 1) * HEAD_DIM])
        bias.append(b[0:1, :])
    hid = jax.nn.gelu(hpre + jnp.where(is_key, bias[0], bias[1]))
    out = _dot(jnp.concatenate(split(hid), axis=1), w2_ref[...].astype(BF16))
    out_ref[0] = jnp.where(is_key, _head_norm(out, kg_ref[0:1, :]), out)


def compress(pool, page_table, wkv, pe_kv, w2cat, k_norm_g):
    n = page_table.shape[0]
    page_rows = PAGE_SIZE * N_SLOT
    page_spec = lambda p: pl.BlockSpec((page_rows, HEAD_DIM), lambda i, pt, p=p: (pt[i, p], 0))
    const = lambda shape: pl.BlockSpec(shape, lambda i, pt: (0,) * len(shape))
    return pl.pallas_call(
        _compress_body,
        grid_spec=pltpu.PrefetchScalarGridSpec(
            num_scalar_prefetch=1, grid=(n,),
            in_specs=[page_spec(p) for p in range(N_PAGES)]
            + [const(wkv.shape), const(pe_kv.shape), const(w2cat.shape), const(k_norm_g.shape)],
            out_specs=pl.BlockSpec((1, N_CMP_BLK * N_SLOT, HEAD_DIM), lambda i, pt: (i, 0, 0))),
        out_shape=jax.ShapeDtypeStruct((n, N_CMP_BLK * N_SLOT, HEAD_DIM), F32),
        compiler_params=_cparams("arbitrary"), name="compress",
    )(page_table, *([pool] * N_PAGES), wkv, pe_kv, w2cat, k_norm_g)


SEL_ROWS = 40


def _cmp_select_body(q_ref, kvc_ref, cov_ref, o_ref, sel_ref, *, bn, tq, pos_base):
    qt = pl.program_id(1)
    grp = GQA * tq
    pairs = [(b, j) for b in range(bn) for j in range(N_KV_HEADS)]

    def slot(b, c):
        return kvc_ref[b, pl.ds(c, N_CMP_BLK, stride=N_SLOT), :].astype(BF16)

    s = jnp.concatenate(
        [_dot_nt(q_ref[b, j * GQA:(j + 1) * GQA].reshape(grp, HEAD_DIM).astype(BF16), slot(b, j)) for b, j in pairs],
        axis=0) * SCALE
    rows = len(pairs) * grp
    pos = pos_base + qt * tq + (lax.broadcasted_iota(jnp.int32, (rows, N_CMP_BLK), 0) & (tq - 1))
    blk = lax.broadcasted_iota(jnp.int32, (rows, N_CMP_BLK), 1)
    valid = (blk < N_CMP_BLK - 1) & (blk * CMP_STRIDE + (L_CMP - 1) <= pos)
    p = _masked_softmax(s, valid)
    p_bf = p.astype(BF16)
    for i, (b, j) in enumerate(pairs):
        o_ref[b, j * GQA:(j + 1) * GQA] = _dot(p_bf[i * grp:(i + 1) * grp],
                                               slot(b, N_KV_HEADS + j)).reshape(GQA, tq, HEAD_DIM)
    psum = jnp.sum(p.reshape(len(pairs), GQA, tq, N_CMP_BLK), axis=1).reshape(len(pairs) * tq, N_CMP_BLK)
    p_hi = psum.astype(BF16)
    p_lo = (psum - p_hi.astype(F32)).astype(BF16)
    imp_t = _dot_nt(cov_ref[...], p_hi) + _dot_nt(cov_ref[...], p_lo)
    srows = len(pairs) * tq
    imp_t = imp_t[0:SEL_ROWS, :]
    blk = lax.broadcasted_iota(jnp.int32, (SEL_ROWS, srows), 0)
    cur = (pos_base + qt * tq + (lax.broadcasted_iota(jnp.int32, (SEL_ROWS, srows), 1) & (tq - 1))) >> 6
    forced = (blk == 0) | (blk == cur) | (blk == cur - 1)
    score = jnp.where(blk <= cur, imp_t + jnp.where(forced, FORCE_BONUS, 0.0), NEG_INF)
    sel_t = jnp.zeros((SEL_ROWS, srows), F32)
    for _ in range(N_SEL):
        m = jnp.max(score, axis=0, keepdims=True)
        first = jnp.min(jnp.where(score == m, blk, LANE), axis=0, keepdims=True)
        hit = blk == first
        sel_t = jnp.where(hit, 1.0, sel_t)
        score = jnp.where(hit, PICKED, score)
    sel = jnp.transpose(jnp.concatenate([sel_t, jnp.zeros((LANE - SEL_ROWS, srows), F32)], axis=0))
    for i, (b, j) in enumerate(pairs):
        sel_ref[b, j] = sel[i * tq:(i + 1) * tq]


def cmp_select(q, kvc, cover, bn, tq, pos_base):
    n, _, t, _ = q.shape
    assert L_SEL == 64 and tq & (tq - 1) == 0
    return pl.pallas_call(
        functools.partial(_cmp_select_body, bn=bn, tq=tq, pos_base=pos_base), grid=(n // bn, t // tq),
        in_specs=[pl.BlockSpec((bn, N_Q_HEADS, tq, HEAD_DIM), lambda b, i: (b, 0, i, 0)),
                  pl.BlockSpec((bn, N_CMP_BLK * N_SLOT, HEAD_DIM), lambda b, i: (b, 0, 0)),
                  pl.BlockSpec((N_CMP_BLK, LANE), lambda b, i: (0, 0))],
        out_specs=[pl.BlockSpec((bn, N_Q_HEADS, tq, HEAD_DIM), lambda b, i: (b, 0, i, 0)),
                   pl.BlockSpec((bn, N_KV_HEADS, tq, LANE), lambda b, i: (b, 0, i, 0))],
        out_shape=[jax.ShapeDtypeStruct((n, N_Q_HEADS, t, HEAD_DIM), F32),
                   jax.ShapeDtypeStruct((n, N_KV_HEADS, t, LANE), F32)],
        compiler_params=_cparams("arbitrary", "arbitrary"), name="cmp_select",
    )(q, kvc, cover)


SOFTMAX_ROWS = 16


def _attn_prompt_body(*refs, tq, t, selected):
    if selected:
        q_ref, k_ref, v_ref, sel_ref, e_ref, o_ref, s_ref, bias_ref, p_ref, inv_ref = refs
    else:
        q_ref, k_ref, v_ref, o_ref, s_ref, bias_ref, p_ref, inv_ref = refs
    qt = pl.program_id(2)
    rows = GQA * tq
    q = q_ref[0].reshape(rows, HEAD_DIM)

    def attend(start, nk):
        k = k_ref[0, 0, pl.ds(start, nk), :]
        v = v_ref[0, 0, pl.ds(start, nk), :]
        s_ref[:, 0:nk] = _dot_nt(q, k)
        qpos = qt * tq + lax.broadcasted_iota(jnp.int32, (tq, nk), 0)
        kpos = start + lax.broadcasted_iota(jnp.int32, (tq, nk), 1)
        ok = kpos <= qpos
        if selected:
            ok = ok & (_dot(sel_ref[0, 0].astype(BF16), e_ref[:, 0:nk]) > 0.5)
        else:
            ok = ok & (kpos > qpos - WINDOW)
        bias_ref[:, 0:nk] = jnp.where(ok, 0.0, NEG_INF)

        for r0 in range(0, rows, SOFTMAX_ROWS):
            b0 = r0 % tq
            sb = s_ref[r0:r0 + SOFTMAX_ROWS, 0:nk] * (SCALE * LOG2_E) + bias_ref[b0:b0 + SOFTMAX_ROWS, 0:nk]
            m = jnp.max(sb, axis=-1, keepdims=True)
            e = jnp.exp2(sb - m)
            p_ref[r0:r0 + SOFTMAX_ROWS, 0:nk] = e.astype(BF16)
            total = jnp.maximum(jnp.sum(e, axis=-1, keepdims=True), TINY)
            inv_ref[r0:r0 + SOFTMAX_ROWS, :] = jnp.broadcast_to(jnp.where(m > 0.5 * NEG_INF, 1.0 / total, 0.0),
                                                                (SOFTMAX_ROWS, HEAD_DIM))
        o_ref[0] = (_dot(p_ref[:, 0:nk], v) * inv_ref[...]).reshape(GQA, tq, HEAD_DIM)

    if selected:
        n_bucket = t // WINDOW
        per = WINDOW // tq
        for b in range(n_bucket):
            @pl.when(qt // per == b)
            def _(b=b):
                attend(0, (b + 1) * WINDOW)
    else:
        attend(pl.multiple_of(jnp.maximum(qt * tq - WINDOW, 0), tq), WINDOW + tq)


def attn_prompt(q, kv, sel, expand, tq=128):
    n, _, t, _ = q.shape
    selected = sel is not None
    in_specs = [pl.BlockSpec((1, GQA, tq, HEAD_DIM), lambda b, j, i: (b, j, i, 0)),
                pl.BlockSpec((1, 1, t, HEAD_DIM), lambda b, j, i: (b, j, 0, 0)),
                pl.BlockSpec((1, 1, t, HEAD_DIM), lambda b, j, i: (b, N_KV_HEADS + j, 0, 0))]
    args = [q, kv, kv]
    if selected:
        in_specs += [pl.BlockSpec((1, 1, tq, LANE), lambda b, j, i: (b, j, i, 0)),
                     pl.BlockSpec(expand.shape, lambda b, j, i: (0, 0))]
        args += [sel, expand]
    nk_max = t if selected else WINDOW + tq
    return pl.pallas_call(
        functools.partial(_attn_prompt_body, tq=tq, t=t, selected=selected), grid=(n, N_KV_HEADS, t // tq),
        in_specs=in_specs,
        out_specs=pl.BlockSpec((1, GQA, tq, HEAD_DIM), lambda b, j, i: (b, j, i, 0)),
        out_shape=jax.ShapeDtypeStruct((n, N_Q_HEADS, t, HEAD_DIM), F32),
        scratch_shapes=[pltpu.VMEM((GQA * tq, nk_max), F32), pltpu.VMEM((tq, nk_max), F32),
                        pltpu.VMEM((GQA * tq, nk_max), BF16), pltpu.VMEM((GQA * tq, HEAD_DIM), F32)],
        compiler_params=_cparams("arbitrary", "arbitrary", "arbitrary"),
        name="attn_prompt_sel" if selected else "attn_prompt_win",
    )(*args)


T_PAD = 8


def _pad_keys(x):
    return jnp.concatenate([x, jnp.zeros((PAGE_SIZE - x.shape[0], x.shape[1]), x.dtype)], axis=0)


SEQS_PER_STEP = 2


def _attn_sample_sel_body(pt_ref, *refs, t_real):
    del pt_ref
    n_page_refs = SEQS_PER_STEP * N_PAGES
    q_ref, sel_ref, new_ref, e_ref, o_ref = refs[n_page_refs:]
    rows = GQA * T_PAD
    tok = lax.broadcasted_iota(jnp.int32, (rows, PAGE_SIZE), 0) & (T_PAD - 1)
    col = lax.broadcasted_iota(jnp.int32, (rows, PAGE_SIZE), 1)
    for b, j in [(b, j) for b in range(SEQS_PER_STEP) for j in range(N_KV_HEADS)]:
        pages = refs[b * N_PAGES:(b + 1) * N_PAGES]
        kc0, vc0 = j * HEAD_DIM, KV_DIM + j * HEAD_DIM
        q = q_ref[b, j * GQA:(j + 1) * GQA].reshape(rows, HEAD_DIM).astype(BF16)
        k_rows = pl.ds(j, PAGE_SIZE, stride=N_SLOT)
        v_rows = pl.ds(N_KV_HEADS + j, PAGE_SIZE, stride=N_SLOT)
        parts = [_dot_nt(q, pages[p][k_rows, :].astype(BF16)) for p in range(N_PAGES)]
        parts.append(_dot_nt(q, _pad_keys(new_ref[b, :, kc0:kc0 + HEAD_DIM]).astype(BF16)))
        s = jnp.concatenate(parts, axis=1) * SCALE
        sel = sel_ref[b, j]
        sel4 = jnp.concatenate([sel] * GQA, axis=0)
        picked = _dot(sel4.astype(BF16), e_ref[...])
        new_blk = PAST_LEN // L_SEL
        new_ok = (col < t_real) & (col <= tok)
        new_picked = jnp.where(new_ok, _lane_pick(sel4, new_blk), 0.0)
        mask = jnp.concatenate([picked, new_picked], axis=1) > 0.5
        p = _masked_softmax(s, mask).astype(BF16)
        o = _dot(p[:, PAST_LEN:], _pad_keys(new_ref[b, :, vc0:vc0 + HEAD_DIM]).astype(BF16))
        for pg in range(N_PAGES):
            o = o + _dot(p[:, pg * PAGE_SIZE:(pg + 1) * PAGE_SIZE], pages[pg][v_rows, :].astype(BF16))
        o_ref[b, j * GQA:(j + 1) * GQA] = o.reshape(GQA, T_PAD, HEAD_DIM)


def attn_sample_sel(pool, page_table, q, sel, new_rows, expand, t_real):
    n = page_table.shape[0]
    cols = new_rows.shape[-1]
    bs = SEQS_PER_STEP
    assert n % bs == 0
    page_spec = lambda b, p: pl.BlockSpec((PAGE_SIZE * N_SLOT, HEAD_DIM), lambda i, pt, b=b, p=p: (pt[bs * i + b, p], 0))
    return pl.pallas_call(
        functools.partial(_attn_sample_sel_body, t_real=t_real),
        grid_spec=pltpu.PrefetchScalarGridSpec(
            num_scalar_prefetch=1, grid=(n // bs,),
            in_specs=[page_spec(b, p) for b in range(bs) for p in range(N_PAGES)]
            + [pl.BlockSpec((bs, N_Q_HEADS, T_PAD, HEAD_DIM), lambda i, pt: (i, 0, 0, 0)),
               pl.BlockSpec((bs, N_KV_HEADS, T_PAD, LANE), lambda i, pt: (i, 0, 0, 0)),
               pl.BlockSpec((bs, T_PAD, cols), lambda i, pt: (i, 0, 0)),
               pl.BlockSpec(expand.shape, lambda i, pt: (0, 0))],
            out_specs=pl.BlockSpec((bs, N_Q_HEADS, T_PAD, HEAD_DIM), lambda i, pt: (i, 0, 0, 0))),
        out_shape=jax.ShapeDtypeStruct((n, N_Q_HEADS, T_PAD, HEAD_DIM), F32),
        compiler_params=_cparams("arbitrary"), name="attn_sample_sel",
    )(page_table, *([pool] * (bs * N_PAGES)), q, sel, new_rows, expand)


def _attn_sample_win_body(win_ref, q_ref, new_ref, newrows_ref, o_ref, wout_ref, *, t_real):
    rows = GQA * T_PAD
    buf_rows = win_ref.shape[0] // SEQS_PER_STEP
    w_buf = buf_rows // N_SLOT
    tok_o = lax.broadcasted_iota(jnp.int32, (rows, w_buf), 0) & (T_PAD - 1)
    col_o = lax.broadcasted_iota(jnp.int32, (rows, w_buf), 1)
    tok_n = lax.broadcasted_iota(jnp.int32, (rows, PAGE_SIZE), 0) & (T_PAD - 1)
    col_n = lax.broadcasted_iota(jnp.int32, (rows, PAGE_SIZE), 1)
    old_ok = jnp.where(col_o + (WINDOW - w_buf) > tok_o, 1.0, 0.0)
    new_ok = jnp.where((col_n < t_real) & (col_n <= tok_n), 1.0, 0.0)
    mask = jnp.concatenate([old_ok, new_ok], axis=1) > 0.5
    for b, j in [(b, j) for b in range(SEQS_PER_STEP) for j in range(N_KV_HEADS)]:
        kc0, vc0 = j * HEAD_DIM, KV_DIM + j * HEAD_DIM
        base = b * buf_rows
        q = q_ref[b, j * GQA:(j + 1) * GQA].reshape(rows, HEAD_DIM).astype(BF16)
        k_new = _pad_keys(new_ref[b, :, kc0:kc0 + HEAD_DIM]).astype(BF16)
        v_new = _pad_keys(new_ref[b, :, vc0:vc0 + HEAD_DIM]).astype(BF16)
        k_old = win_ref[pl.ds(base + j, w_buf, stride=N_SLOT), :].astype(BF16)
        v_old = win_ref[pl.ds(base + N_KV_HEADS + j, w_buf, stride=N_SLOT), :].astype(BF16)
        s = jnp.concatenate([_dot_nt(q, k_old), _dot_nt(q, k_new)], axis=1) * SCALE
        p = _masked_softmax(s, mask).astype(BF16)
        o = _dot(p[:, :w_buf], v_old) + _dot(p[:, w_buf:], v_new)
        o_ref[b, j * GQA:(j + 1) * GQA] = o.reshape(GQA, T_PAD, HEAD_DIM)
    keep = (w_buf - t_real) * N_SLOT
    new_n = t_real * N_SLOT
    for b in range(SEQS_PER_STEP):
        base = b * buf_rows
        wout_ref[base:base + keep, :] = win_ref[base + new_n:base + buf_rows, :]
        wout_ref[base + keep:base + buf_rows, :] = newrows_ref[b * new_n:(b + 1) * new_n, :]


def attn_sample_win(win, q, new_rows, new_cache_rows, t_real):
    n = q.shape[0]
    cols = new_rows.shape[-1]
    bs = SEQS_PER_STEP
    assert n % bs == 0
    buf_rows = bs * (win.shape[0] // n)
    return pl.pallas_call(
        functools.partial(_attn_sample_win_body, t_real=t_real), grid=(n // bs,),
        in_specs=[pl.BlockSpec((buf_rows, HEAD_DIM), lambda i: (i, 0)),
                  pl.BlockSpec((bs, N_Q_HEADS, T_PAD, HEAD_DIM), lambda i: (i, 0, 0, 0)),
                  pl.BlockSpec((bs, T_PAD, cols), lambda i: (i, 0, 0)),
                  pl.BlockSpec((bs * t_real * N_SLOT, HEAD_DIM), lambda i: (i, 0))],
        out_specs=[pl.BlockSpec((bs, N_Q_HEADS, T_PAD, HEAD_DIM), lambda i: (i, 0, 0, 0)),
                   pl.BlockSpec((buf_rows, HEAD_DIM), lambda i: (i, 0))],
        out_shape=[jax.ShapeDtypeStruct((n, N_Q_HEADS, T_PAD, HEAD_DIM), F32),
                   jax.ShapeDtypeStruct(win.shape, F32)],
        compiler_params=_cparams("arbitrary"), name="attn_sample_win",
    )(win, q, new_rows, new_cache_rows)


def _combine_body(oc_ref, os_ref, ow_ref, gl_ref, g_ref, out_ref, *, rows):
    gates = jax.nn.sigmoid(gl_ref[...])
    outs = []
    sq = jnp.zeros((rows, 1), F32)
    for h in range(N_Q_HEADS):
        a = (_lane_pick(gates, h) * oc_ref[:, h].reshape(rows, HEAD_DIM)
             + _lane_pick(gates, N_Q_HEADS + h) * os_ref[:, h].reshape(rows, HEAD_DIM)
             + _lane_pick(gates, 2 * N_Q_HEADS + h) * ow_ref[:, h].reshape(rows, HEAD_DIM))
        outs.append(a)
        sq = sq + jnp.sum(a * a, axis=-1, keepdims=True)
    inv = lax.rsqrt(sq * (1.0 / ATTN_DIM) + RMS_EPS)
    for h in range(N_Q_HEADS):
        sl = slice(h * HEAD_DIM, (h + 1) * HEAD_DIM)
        out_ref[:, sl] = ((outs[h] * inv) * g_ref[:, sl]).astype(out_ref.dtype)


def combine(o_cmp, o_sel, o_win, gate_logits, gain, bn, tt):
    n, _, t, _ = o_cmp.shape
    nt = t // tt
    rows = bn * tt
    o_spec = pl.BlockSpec((bn, N_Q_HEADS, tt, HEAD_DIM), lambda b, i: (b, 0, i, 0))
    return pl.pallas_call(
        functools.partial(_combine_body, rows=rows), grid=(n // bn, nt),
        in_specs=[o_spec, o_spec, o_spec,
                  pl.BlockSpec((rows, LANE), lambda b, i: (b * nt + i, 0)),
                  pl.BlockSpec((1, ATTN_DIM), lambda b, i: (0, 0))],
        out_specs=pl.BlockSpec((rows, ATTN_DIM), lambda b, i: (b * nt + i, 0)),
        out_shape=jax.ShapeDtypeStruct((n * t, ATTN_DIM), BF16),
        compiler_params=_cparams("arbitrary", "arbitrary"), name="combine",
    )(o_cmp, o_sel, o_win, gate_logits, gain.reshape(1, ATTN_DIM))


TOKEN_CHUNKS = D_MODEL // LANE
TOKEN_PITCH = 40


def _router_body(h_ref, g_ref, w_ref, b_ref, xrows_ref, ids_ref, wts_ref):
    tm = h_ref.shape[0]
    h = h_ref[...]
    x = (h * lax.rsqrt(jnp.mean(h * h, axis=-1, keepdims=True) + RMS_EPS)) * g_ref[...]
    for c in range(TOKEN_PITCH):
        piece = x[:, c * LANE:(c + 1) * LANE] if c < TOKEN_CHUNKS else jnp.zeros((tm, LANE), F32)
        xrows_ref[pl.ds(c, tm, stride=TOKEN_PITCH), :] = piece
    logits = _dot(x.astype(BF16), w_ref[...].astype(BF16)) + b_ref[...]
    lane = lax.broadcasted_iota(jnp.int32, logits.shape, 1)
    is_grp = (lane >= N_EXPERTS) & (lane < N_EXPERTS + N_GROUPS)
    gl = jnp.where(is_grp, logits, NEG_INF)
    ge = jnp.where(is_grp, jnp.exp(gl - jnp.max(gl, axis=-1, keepdims=True)), 0.0)
    p_grp = ge / jnp.sum(ge, axis=-1, keepdims=True)
    g_val = jnp.max(p_grp, axis=-1, keepdims=True)
    g_idx = jnp.min(jnp.where(is_grp & (p_grp == g_val), lane, 2 * LANE), axis=-1, keepdims=True) - N_EXPERTS
    lo = g_idx * EXPERTS_PER_GROUP
    in_grp = (lane >= lo) & (lane < lo + EXPERTS_PER_GROUP)
    el = jnp.where(in_grp, logits, NEG_INF)
    ee = jnp.where(in_grp, jnp.exp(el - jnp.max(el, axis=-1, keepdims=True)), 0.0)
    p_e = ee / jnp.sum(ee, axis=-1, keepdims=True)
    cand = jnp.where(in_grp, p_e, -1.0)
    e1 = jnp.max(cand, axis=-1, keepdims=True)
    i1 = jnp.min(jnp.where(cand == e1, lane, 2 * LANE), axis=-1, keepdims=True)
    cand = jnp.where(lane == i1, -1.0, cand)
    e2 = jnp.max(cand, axis=-1, keepdims=True)
    i2 = jnp.min(jnp.where(cand == e2, lane, 2 * LANE), axis=-1, keepdims=True)
    tot = e1 + e2
    ids_ref[...] = jnp.where(lane == 0, i1, jnp.where(lane == 1, i2, 0))
    wts_ref[...] = jnp.where(lane == 0, g_val * e1 / tot, jnp.where(lane == 1, g_val * e2 / tot, 0.0))


def router(h, gain, w_router, b_router, tm=256):
    m, d = h.shape
    out_spec = pl.BlockSpec((tm, LANE), lambda i: (i, 0))
    return pl.pallas_call(
        _router_body, grid=(m // tm,),
        in_specs=[pl.BlockSpec((tm, d), lambda i: (i, 0)),
                  pl.BlockSpec((1, d), lambda i: (0, 0)),
                  pl.BlockSpec((d, LANE), lambda i: (0, 0)),
                  pl.BlockSpec((1, LANE), lambda i: (0, 0))],
        out_specs=[pl.BlockSpec((tm * TOKEN_PITCH, LANE), lambda i: (i, 0)), out_spec, out_spec],
        out_shape=[jax.ShapeDtypeStruct((m * TOKEN_PITCH, LANE), F32),
                   jax.ShapeDtypeStruct((m, LANE), jnp.int32), jax.ShapeDtypeStruct((m, LANE), F32)],
        compiler_params=_cparams("arbitrary"), name="router",
    )(h, gain.reshape(1, d), w_router, b_router)


def _gather_rows(idx_ref, n_rows, src_hbm, dst, sem, wait):
    def copy(r, src):
        return pltpu.make_async_copy(src_hbm.at[pl.ds(src, 1), :], dst.at[pl.ds(r, 1), :], sem)

    if wait:
        def body(r, c):
            copy(r, 0).wait()
            return c
        lax.fori_loop(0, n_rows, body, 0, unroll=8)
    else:
        for r in range(n_rows):
            copy(r, idx_ref[0, 0, r]).start()


def _gather_tokens(idx_ref, tokens, src_hbm, dst, base, sem, wait):
    def copy(r, tok):
        return pltpu.make_async_copy(src_hbm.at[pl.ds(pl.multiple_of(tok * TOKEN_PITCH, 8), TOKEN_CHUNKS), :],
                                     dst.at[pl.ds(pl.multiple_of(base + r * TOKEN_PITCH, 8), TOKEN_CHUNKS), :], sem)

    if wait:
        def body(r, c):
            copy(r, 0).wait()
            return c
        lax.fori_loop(tokens.start, tokens.stop, body, 0, unroll=8)
    else:
        for r in tokens:
            copy(r, idx_ref[0, 0, r]).start()


def _moe_ffn_body(te_ref, used_ref, cur_ref, nxt_ref, x_hbm, rw_ref, wg_ref, wu_ref, wd_ref, ys_ref, xbuf, sem, *, tm):
    del te_ref
    i = pl.program_id(0)
    n_used = used_ref[0]
    slot = lax.rem(i, 2)
    half = tm * TOKEN_PITCH
    here, other = slot * half, (1 - slot) * half

    @pl.when(i == 0)
    def _():
        _gather_tokens(cur_ref, range(tm), x_hbm, xbuf, 0, sem.at[0], wait=False)

    @pl.when(i < n_used)
    def _():
        _gather_tokens(cur_ref, range(tm), x_hbm, xbuf, here, sem.at[slot], wait=True)

    @pl.when(i + 1 < n_used)
    def _():
        _gather_tokens(nxt_ref, range(tm), x_hbm, xbuf, other, sem.at[1 - slot], wait=False)

    @pl.when(i < n_used)
    def _():
        x = jnp.concatenate([xbuf[pl.ds(here + c, tm, stride=TOKEN_PITCH), :].astype(BF16)
                             for c in range(TOKEN_CHUNKS)], axis=1)
        hid = jax.nn.silu(_dot(x, wg_ref[0])) * _dot(x, wu_ref[0])
        gate = jnp.concatenate([rw_ref[...]] * (hid.shape[1] // LANE), axis=1)
        ys_ref[...] = _dot((hid * gate).astype(BF16), wd_ref[0])

    @pl.when(i >= n_used)
    def _():
        ys_ref[...] = jnp.zeros_like(ys_ref)


def moe_ffn(x_rows, tile_expert, n_used, row_token, row_weight, w_gate, w_up, w_down, tm):
    n_tiles = tile_expert.shape[0]
    d = TOKEN_CHUNKS * LANE
    f = w_gate.shape[-1]
    smem_rows = lambda fn: pl.BlockSpec((1, 1, tm), fn, memory_space=pltpu.SMEM)
    return pl.pallas_call(
        functools.partial(_moe_ffn_body, tm=tm),
        grid_spec=pltpu.PrefetchScalarGridSpec(
            num_scalar_prefetch=2, grid=(n_tiles,),
            in_specs=[smem_rows(lambda i, te, nu: (i, 0, 0)),
                      smem_rows(lambda i, te, nu: (jnp.minimum(i + 1, n_tiles - 1), 0, 0)),
                      pl.BlockSpec(memory_space=pl.ANY),
                      pl.BlockSpec((tm, LANE), lambda i, te, nu: (i, 0)),
                      pl.BlockSpec((1, d, f), lambda i, te, nu: (te[i], 0, 0)),
                      pl.BlockSpec((1, d, f), lambda i, te, nu: (te[i], 0, 0)),
                      pl.BlockSpec((1, f, d), lambda i, te, nu: (te[i], 0, 0))],
            out_specs=pl.BlockSpec((tm, d), lambda i, te, nu: (i, 0)),
            scratch_shapes=[pltpu.VMEM((2 * tm * TOKEN_PITCH, LANE), F32), pltpu.SemaphoreType.DMA((2,))]),
        out_shape=jax.ShapeDtypeStruct((n_tiles * tm, d), F32),
        compiler_params=_cparams("arbitrary"), name="moe_ffn",
    )(tile_expert, n_used, row_token, row_token, x_rows, row_weight, w_gate, w_up, w_down)


def _moe_combine_body(cur_ref, nxt_ref, ys_hbm, h_ref, ya_ref, yb_ref, buf, sem, *, tiles_a):
    i = pl.program_id(0)
    n = pl.num_programs(0)
    rows = buf.shape[1]
    tm = h_ref.shape[0]
    slot = lax.rem(i, 2)

    @pl.when(i == 0)
    def _():
        _gather_rows(cur_ref, rows, ys_hbm, buf.at[0], sem.at[0], wait=False)

    _gather_rows(cur_ref, rows, ys_hbm, buf.at[slot], sem.at[slot], wait=True)
    _gather_rows(nxt_ref, rows, ys_hbm, buf.at[1 - slot], sem.at[1 - slot], wait=False)
    y = h_ref[...] + buf[slot, 0:tm, :] + buf[slot, tm:rows, :]

    @pl.when(i == n - 1)
    def _():
        _gather_rows(cur_ref, rows, ys_hbm, buf.at[1 - slot], sem.at[1 - slot], wait=True)

    @pl.when(i < tiles_a)
    def _():
        ya_ref[...] = y

    @pl.when(i >= tiles_a)
    def _():
        yb_ref[...] = y


def moe_combine(ys, pair_row, h, rows_a, tm=128):
    m, d = h.shape
    n_tiles = m // tm
    tiles_a = rows_a // tm
    assert rows_a % tm == 0 and 0 < tiles_a < n_tiles
    smem_rows = lambda fn: pl.BlockSpec((1, 1, 2 * tm), fn, memory_space=pltpu.SMEM)
    return pl.pallas_call(
        functools.partial(_moe_combine_body, tiles_a=tiles_a), grid=(n_tiles,),
        in_specs=[smem_rows(lambda i: (i, 0, 0)),
                  smem_rows(lambda i: (jnp.minimum(i + 1, n_tiles - 1), 0, 0)),
                  pl.BlockSpec(memory_space=pl.ANY),
                  pl.BlockSpec((tm, d), lambda i: (i, 0))],
        out_specs=[pl.BlockSpec((tm, d), lambda i: (jnp.minimum(i, tiles_a - 1), 0)),
                   pl.BlockSpec((tm, d), lambda i: (jnp.maximum(i - tiles_a, 0), 0))],
        out_shape=[jax.ShapeDtypeStruct((rows_a, d), F32), jax.ShapeDtypeStruct((m - rows_a, d), F32)],
        scratch_shapes=[pltpu.VMEM((2, 2 * tm, d), F32), pltpu.SemaphoreType.DMA((2,))],
        compiler_params=_cparams("arbitrary"), name="moe_combine",
    )(pair_row, pair_row, ys, h)


def moe_routed(xt, ids, wts, h, w_gate, w_up, w_down, tm, rows_a):
    m = h.shape[0]
    n_pair = 2 * m
    n_tiles = (n_pair + N_EXPERTS * (tm - 1)) // tm + 1
    flat_e = ids[:, :2].reshape(n_pair)
    flat_w = wts[:, :2].reshape(n_pair)
    order = jnp.argsort(flat_e, stable=True).astype(jnp.int32)
    rank = jnp.argsort(order).astype(jnp.int32)
    counts = jnp.sum(flat_e[:, None] == jnp.arange(N_EXPERTS, dtype=jnp.int32)[None, :], axis=0, dtype=jnp.int32)
    padded = ((counts + tm - 1) // tm) * tm
    pad_end = jnp.cumsum(padded)
    pad_start = pad_end - padded
    start = jnp.cumsum(counts) - counts
    tile_start = jnp.arange(n_tiles, dtype=jnp.int32) * tm
    tile_expert = jnp.minimum(jnp.sum(tile_start[:, None] >= pad_end[None, :], axis=1), N_EXPERTS - 1).astype(jnp.int32)
    row_e = jnp.repeat(tile_expert, tm)
    offs = jnp.arange(n_tiles * tm, dtype=jnp.int32) - pad_start[row_e]
    used = offs < counts[row_e]
    src = order[jnp.clip(start[row_e] + offs, 0, n_pair - 1)]
    row_token = jnp.where(used, src // 2, 0)
    row_weight = jnp.where(used, flat_w[src], 0.0)
    pair_row = (pad_start[flat_e] + rank - start[flat_e]).reshape(m, 2)
    n_used = (pad_end[N_EXPERTS - 1:] // tm).astype(jnp.int32)
    ys = moe_ffn(xt, tile_expert, n_used, row_token.reshape(n_tiles, 1, tm),
                 jnp.broadcast_to(row_weight[:, None], (n_tiles * tm, LANE)), w_gate, w_up, w_down, tm)
    tc = 128
    pair_tiles = pair_row.reshape(m // tc, tc, 2).transpose(0, 2, 1).reshape(m // tc, 1, 2 * tc)
    return moe_combine(ys, pair_tiles, h, rows_a, tm=tc)


def _cover_matrix(nsb):
    i = np.arange(N_CMP_BLK)[:, None]
    j = np.arange(LANE)[None, :]
    m = np.zeros((N_CMP_BLK, LANE), np.float32)
    for a in range(L_SEL // CMP_STRIDE):
        for c in range(R_CMP):
            m += (i == (L_SEL // CMP_STRIDE) * j + a - c)
    m[N_CMP_BLK - 1:, :] = 0.0
    m[:, nsb:] = 0.0
    assert nsb <= SEL_ROWS
    return jnp.asarray(m.T, BF16)


def _expand_matrix(n_keys):
    b = np.arange(LANE)[:, None]
    k = np.arange(n_keys)[None, :]
    return jnp.asarray((k // L_SEL == b).astype(np.float32), BF16)


def _to_heads(q, n, t):
    return q.reshape(n, t, N_Q_HEADS, HEAD_DIM).transpose(0, 2, 1, 3)


def kernel(x_prompt, x_sample, cache_cmp_kv, cache_sel_kv, state_win_kv, state_conv, page_table, norm_mix_g, w_in,
           conv_w, q_norm_g, k_norm_g, phi_pe, phi_w1, phi_w2, out_norm_g, w_out, norm_ffn_g, w_group_router,
           b_group_router, w_expert_router, b_expert_router, w_gate, w_up, w_down):
    n_p, t_p, d = x_prompt.shape
    n_s, t_s, _ = x_sample.shape
    assert w_in.shape[0] == 1 and t_s < CMP_STRIDE and t_s <= T_PAD and t_p % WINDOW == 0
    kv_cols = 2 * KV_DIM
    w_in_t = jnp.transpose(w_in[0])
    w_in_gate_t = jnp.pad(w_in_t[Z_MAIN:], ((0, LANE - 3 * N_Q_HEADS), (0, 0)))
    w_o = w_out
    wkv = phi_w1[0].reshape(2, R_CMP, CMP_STRIDE, HEAD_DIM, HEAD_DIM).transpose(2, 0, 3, 1, 4)
    wkv = wkv.reshape(CMP_STRIDE * 2 * HEAD_DIM, R_CMP * HEAD_DIM)
    pe5 = phi_pe[0].reshape(2, R_CMP, CMP_STRIDE, 1, HEAD_DIM)
    pe_kv = (pe5 * jnp.eye(2, dtype=F32).reshape(2, 1, 1, 2, 1)).reshape(2, R_CMP, CMP_STRIDE * 2 * HEAD_DIM)
    w2cat = phi_w2[0].reshape(2 * HEAD_DIM, HEAD_DIM)
    w_router = jnp.pad(jnp.concatenate([w_expert_router[0], w_group_router[0]], axis=1),
                       ((0, 0), (0, LANE - N_EXPERTS - N_GROUPS)))
    b_router = jnp.pad(jnp.concatenate([b_expert_router[0], b_group_router[0]]),
                       (0, LANE - N_EXPERTS - N_GROUPS)).reshape(1, LANE)
    wg_bf, wu_bf, wd_bf = w_gate[0].astype(BF16), w_up[0].astype(BF16), w_down[0].astype(BF16)

    def project(x2d):
        xn = rmsnorm_cast(x2d, norm_mix_g[0])
        z = matmul([([xn], None)], [(w_in_t, 0)], tm=1024, tn=512, w_transposed=True)
        gate_logits = matmul([([xn], None)], [(w_in_gate_t, 0)], tn=LANE, w_transposed=True)
        return z, gate_logits

    m_p = n_p * t_p

    xp = x_prompt.reshape(n_p * t_p, d)
    z, glog = project(xp)
    qh, ks_h, kw_h, kvc, kvs, kvw = postproj_prompt(z, n_p, t_p, q_norm_g[0], k_norm_g[0])
    conv_out, conv_last = conv_prompt(z, n_p, t_p, conv_w[0], out_norm_g[0][:CONV_DIM])
    ident = jnp.arange(n_p * (t_p // PAGE_SIZE), dtype=jnp.int32).reshape(n_p, t_p // PAGE_SIZE)
    kv_cmp = compress(kvc, ident, wkv, pe_kv, w2cat, k_norm_g[0])
    o_cmp, sel = cmp_select(qh, kv_cmp, _cover_matrix(t_p // L_SEL), bn=1, tq=256, pos_base=0)
    o_sel = attn_prompt(qh, ks_h, sel, _expand_matrix(t_p))
    o_win = attn_prompt(qh, kw_h, None, None)
    attn_out = combine(o_cmp, o_sel, o_win, glog, out_norm_g[0][CONV_DIM:], bn=1, tt=256)
    mixed_p = ([conv_out, attn_out], xp)
    kv_shape = (1, n_p, t_p, 2, N_KV_HEADS, HEAD_DIM)
    w_keep = min(WINDOW, t_p)
    prompt_win = kvw.reshape(kv_shape)[:, :, t_p - w_keep:]
    prompt_conv = conv_last[:, 8 - (CONV_WIDTH - 1):, :][None]

    xs = x_sample.reshape(n_s * t_s, d)
    z, glog = project(xs)
    q, kvc_s, kvs_s, kvw_s = postproj(z, q_norm_g[0], k_norm_g[0])
    conv_out, conv_state = conv_sample(z, n_s, t_s, state_conv[0], conv_w[0], out_norm_g[0][:CONV_DIM])
    pad_t = lambda a: jnp.pad(a, ((0, 0), (0, T_PAD - t_s), (0, 0)))
    qh = jnp.pad(_to_heads(q, n_s, t_s), ((0, 0), (0, 0), (0, T_PAD - t_s), (0, 0)))
    pool_cmp = cache_cmp_kv[0].reshape(-1, HEAD_DIM)
    pool_sel = cache_sel_kv[0].reshape(-1, HEAD_DIM)
    kv_cmp = compress(pool_cmp, page_table, wkv, pe_kv, w2cat, k_norm_g[0])
    o_cmp, sel = cmp_select(qh, kv_cmp, _cover_matrix(PAST_LEN // L_SEL + 1), bn=16, tq=T_PAD, pos_base=PAST_LEN)
    o_sel = attn_sample_sel(pool_sel, page_table, qh, sel, pad_t(kvs_s.reshape(n_s, t_s, kv_cols)),
                            _expand_matrix(PAST_LEN), t_s)
    win = state_win_kv[0].reshape(-1, HEAD_DIM)
    o_win, win_new = attn_sample_win(win, qh, pad_t(kvw_s.reshape(n_s, t_s, kv_cols)),
                                     kvw_s.reshape(-1, HEAD_DIM), t_s)
    glog_pad = pad_t(glog.reshape(n_s, t_s, LANE)).reshape(n_s * T_PAD, LANE)
    attn_out = combine(o_cmp, o_sel, o_win, glog_pad, out_norm_g[0][CONV_DIM:], bn=32, tt=T_PAD)
    attn_out = attn_out.reshape(n_s, T_PAD, ATTN_DIM)[:, :t_s].reshape(n_s * t_s, ATTN_DIM)
    s_shape = (1, n_s, t_s, 2, N_KV_HEADS, HEAD_DIM)

    h_all = matmul([mixed_p, ([conv_out, attn_out], xs)], [(w_o, 0), (w_o, 1)])
    x_rows, ids, wts = router(h_all, norm_ffn_g[0], w_router, b_router)
    y_prompt, y_sample = moe_routed(x_rows, ids, wts, h_all, wg_bf, wu_bf, wd_bf, 256, m_p)

    return (y_prompt.reshape(n_p, t_p, d), y_sample.reshape(n_s, t_s, d), kvc.reshape(kv_shape), kvs.reshape(kv_shape), prompt_win, prompt_conv,
            kvc_s.reshape(s_shape), kvs_s.reshape(s_shape),
            win_new.reshape(1, n_s, -1, 2, N_KV_HEADS, HEAD_DIM), conv_state[None])
```

```python
import functools

import numpy as np
import jax
import jax.numpy as jnp
from jax import lax
from jax.experimental import pallas as pl
from jax.experimental.pallas import tpu as pltpu

F32 = jnp.float32
BF16 = jnp.bfloat16

D_MODEL = 4096
PAST_LEN = 2048
PAGE_SIZE = 128
HEAD_DIM = 128
CONV_DIM = 2048
N_Q_HEADS = 16
N_KV_HEADS = 4
GQA = 4
ATTN_DIM = 2048
KV_DIM = 512
CONV_WIDTH = 3
L_CMP = 32
CMP_STRIDE = 16
R_CMP = 2
L_SEL = 64
N_SEL = 8
WINDOW = 512
FORCE_BONUS = 1e3
SCALE = HEAD_DIM ** -0.5
N_GROUPS = 4
EXPERTS_PER_GROUP = 4
N_EXPERTS = 16
D_FF_EXPERT = 512
RMS_EPS = 1e-6
NEG_INF = -1e30
TINY = 1e-30
PICKED = -3e38
LOG2_E = 1.4426950408889634

N_PAGES = PAST_LEN // PAGE_SIZE
N_CMP_BLK = 128
Z_MAIN = 3 * CONV_DIM + ATTN_DIM + 6 * KV_DIM
LANE = 128
VMEM_LIMIT = 56 * 1024 * 1024


def _cparams(*sem):
    return pltpu.CompilerParams(dimension_semantics=sem, vmem_limit_bytes=VMEM_LIMIT)


def _masked_softmax(s, mask):
    s = jnp.where(mask, s, NEG_INF)
    m = jnp.max(s, axis=-1, keepdims=True)
    e = jnp.where(mask, jnp.exp(s - m), 0.0)
    return e / jnp.maximum(jnp.sum(e, axis=-1, keepdims=True), TINY)


def _dot_nt(a, b):
    return lax.dot_general(a, b, (((1,), (1,)), ((), ())), preferred_element_type=F32)


def _dot(a, b):
    return jnp.dot(a, b, preferred_element_type=F32)


def _lane_pick(x, c):
    lane = lax.broadcasted_iota(jnp.int32, x.shape, 1)
    return jnp.sum(jnp.where(lane == c, x, 0.0), axis=-1, keepdims=True)


def _rmsnorm_body(x_ref, g_ref, o_ref):
    x = x_ref[...]
    inv = lax.rsqrt(jnp.mean(x * x, axis=-1, keepdims=True) + RMS_EPS)
    o_ref[...] = ((x * inv) * g_ref[...]).astype(o_ref.dtype)


def rmsnorm_cast(x, g, dtype=BF16, tm=256):
    m, d = x.shape
    return pl.pallas_call(
        _rmsnorm_body, grid=(m // tm,),
        in_specs=[pl.BlockSpec((tm, d), lambda i: (i, 0)), pl.BlockSpec((1, d), lambda i: (0, 0))],
        out_specs=pl.BlockSpec((tm, d), lambda i: (i, 0)),
        out_shape=jax.ShapeDtypeStruct((m, d), dtype),
        compiler_params=_cparams("arbitrary"), name="rmsnorm_cast",
    )(x, g.reshape(1, d))


def _matmul_body(*refs, n_w, has_res, tiles, w_transposed):
    per = n_w + (1 if has_res else 0)
    n_groups = len(tiles) - 1
    o_ref = refs[n_w + n_groups * per]
    wbf = refs[n_w + n_groups * per + 1:]
    i = pl.program_id(1)
    dot = _dot_nt if w_transposed else _dot

    @pl.when(i == 0)
    def _():
        for p in range(n_w):
            wbf[p][...] = refs[p][...].astype(BF16)

    for g in range(n_groups):
        grp = refs[n_w + g * per:n_w + (g + 1) * per]

        def compute(grp=grp):
            acc = dot(grp[0][...], wbf[0][...])
            for p in range(1, n_w):
                acc = acc + dot(grp[p][...], wbf[p][...])
            if has_res:
                acc = acc + grp[n_w][...]
            o_ref[...] = acc

        if n_groups == 1:
            compute()
        else:
            pl.when((i >= tiles[g]) & (i < tiles[g + 1]))(compute)


def matmul(groups, weights, tm=512, tn=512, w_transposed=False):
    ms = [g[0][0].shape[0] for g in groups]
    tm = min([tm] + ms)
    assert all(m % tm == 0 for m in ms)
    n = (weights[0][0].shape[0 if w_transposed else -1] // tn) * tn
    has_res = groups[0][1] is not None
    tiles = [0]
    for m in ms:
        tiles.append(tiles[-1] + m // tm)
    in_specs, args, scratch = [], [], []
    ks = [a.shape[1] for a in groups[0][0]]
    for (w, kb), k in zip(weights, ks):
        if w_transposed:
            in_specs.append(pl.BlockSpec((tn, k), lambda j, i, kb=kb: (j, kb)))
        elif w.ndim == 3:
            in_specs.append(pl.BlockSpec((None, k, tn), lambda j, i, kb=kb: (0, kb, j)))
        else:
            in_specs.append(pl.BlockSpec((k, tn), lambda j, i, kb=kb: (kb, j)))
        args.append(w)
        scratch.append(pltpu.VMEM((tn, k) if w_transposed else (k, tn), BF16))
    for g, (a_list, res) in enumerate(groups):
        lo, hi = tiles[g], tiles[g + 1]
        row = lambda i, lo=lo, hi=hi: jnp.clip(i, lo, hi - 1) - lo
        for a, k in zip(a_list, ks):
            in_specs.append(pl.BlockSpec((tm, k), lambda j, i, row=row: (row(i), 0)))
            args.append(a)
        if has_res:
            in_specs.append(pl.BlockSpec((tm, tn), lambda j, i, row=row: (row(i), j)))
            args.append(res)
    return pl.pallas_call(
        functools.partial(_matmul_body, n_w=len(weights), has_res=has_res, tiles=tuple(tiles),
                          w_transposed=w_transposed),
        grid=(n // tn, tiles[-1]), in_specs=in_specs,
        out_specs=pl.BlockSpec((tm, tn), lambda j, i: (i, j)),
        out_shape=jax.ShapeDtypeStruct((sum(ms), n), F32), scratch_shapes=scratch,
        compiler_params=_cparams("arbitrary", "arbitrary"), name="matmul",
    )(*args)


def _head_norm(x, g):
    inv = lax.rsqrt(jnp.mean(x * x, axis=-1, keepdims=True) + RMS_EPS)
    return (x * inv) * g


def _postproj_body(zq_ref, zc_ref, zs_ref, zw_ref, qg_ref, kg_ref, q_ref, kvc_ref, kvs_ref, kvw_ref):
    for h in range(N_Q_HEADS):
        sl = slice(h * HEAD_DIM, (h + 1) * HEAD_DIM)
        q_ref[:, sl] = _head_norm(zq_ref[:, sl], qg_ref[...]).astype(q_ref.dtype)
    kvc_ref[...] = zc_ref[...]
    for h in range(N_KV_HEADS):
        sl = slice(h * HEAD_DIM, (h + 1) * HEAD_DIM)
        kvs_ref[:, sl] = _head_norm(zs_ref[:, sl], kg_ref[1:2, :])
        kvw_ref[:, sl] = _head_norm(zw_ref[:, sl], kg_ref[2:3, :])
    kvs_ref[:, KV_DIM:] = zs_ref[:, KV_DIM:]
    kvw_ref[:, KV_DIM:] = zw_ref[:, KV_DIM:]


def postproj(z, q_norm_g, k_norm_g, tm=256):
    m = z.shape[0]
    kv = 2 * KV_DIM
    q0 = 3 * CONV_DIM // ATTN_DIM
    c0 = (3 * CONV_DIM + ATTN_DIM) // kv
    return pl.pallas_call(
        _postproj_body, grid=(m // tm,),
        in_specs=[pl.BlockSpec((tm, ATTN_DIM), lambda i: (i, q0)),
                  pl.BlockSpec((tm, kv), lambda i: (i, c0)),
                  pl.BlockSpec((tm, kv), lambda i: (i, c0 + 1)),
                  pl.BlockSpec((tm, kv), lambda i: (i, c0 + 2)),
                  pl.BlockSpec((1, HEAD_DIM), lambda i: (0, 0)),
                  pl.BlockSpec((3, HEAD_DIM), lambda i: (0, 0))],
        out_specs=[pl.BlockSpec((tm, ATTN_DIM), lambda i: (i, 0)),
                   pl.BlockSpec((tm, kv), lambda i: (i, 0)),
                   pl.BlockSpec((tm, kv), lambda i: (i, 0)),
                   pl.BlockSpec((tm, kv), lambda i: (i, 0))],
        out_shape=[jax.ShapeDtypeStruct((m, ATTN_DIM), F32)] + [jax.ShapeDtypeStruct((m, kv), F32)] * 3,
        compiler_params=_cparams("arbitrary"), name="postproj",
    )(z, z, z, z, q_norm_g.reshape(1, HEAD_DIM), k_norm_g)


N_SLOT = 2 * N_KV_HEADS


def _postproj_prompt_body(zq_ref, zc_ref, zs_ref, zw_ref, qg_ref, kg_ref,
                          q_ref, ks_ref, kw_ref, kvc_ref, kvs_ref, kvw_ref):
    tm = zq_ref.shape[0]
    for h in range(N_Q_HEADS):
        sl = slice(h * HEAD_DIM, (h + 1) * HEAD_DIM)
        q_ref[0, h] = _head_norm(zq_ref[:, sl], qg_ref[...]).astype(BF16)
    for c in range(N_SLOT):
        sl = slice(c * HEAD_DIM, (c + 1) * HEAD_DIM)
        rows = pl.ds(c, tm, stride=N_SLOT)
        kvc_ref[rows, :] = zc_ref[:, sl]
        xs, xw = zs_ref[:, sl], zw_ref[:, sl]
        if c < N_KV_HEADS:
            xs, xw = _head_norm(xs, kg_ref[1:2, :]), _head_norm(xw, kg_ref[2:3, :])
        kvs_ref[rows, :] = xs
        kvw_ref[rows, :] = xw
        ks_ref[0, c] = xs.astype(BF16)
        kw_ref[0, c] = xw.astype(BF16)


def postproj_prompt(z, n, t, q_norm_g, k_norm_g, tm=256):
    kv = 2 * KV_DIM
    nt = t // tm
    q0 = 3 * CONV_DIM // ATTN_DIM
    c0 = (3 * CONV_DIM + ATTN_DIM) // kv
    rows_spec = pl.BlockSpec((tm * N_SLOT, HEAD_DIM), lambda b, i: (b * nt + i, 0))
    slot_spec = pl.BlockSpec((1, N_SLOT, tm, HEAD_DIM), lambda b, i: (b, 0, i, 0))
    rows_sds = jax.ShapeDtypeStruct((n * t * N_SLOT, HEAD_DIM), F32)
    slot_sds = jax.ShapeDtypeStruct((n, N_SLOT, t, HEAD_DIM), BF16)
    return pl.pallas_call(
        _postproj_prompt_body, grid=(n, nt),
        in_specs=[pl.BlockSpec((tm, ATTN_DIM), lambda b, i: (b * nt + i, q0)),
                  pl.BlockSpec((tm, kv), lambda b, i: (b * nt + i, c0)),
                  pl.BlockSpec((tm, kv), lambda b, i: (b * nt + i, c0 + 1)),
                  pl.BlockSpec((tm, kv), lambda b, i: (b * nt + i, c0 + 2)),
                  pl.BlockSpec((1, HEAD_DIM), lambda b, i: (0, 0)),
                  pl.BlockSpec((3, HEAD_DIM), lambda b, i: (0, 0))],
        out_specs=[pl.BlockSpec((1, N_Q_HEADS, tm, HEAD_DIM), lambda b, i: (b, 0, i, 0)),
                   slot_spec, slot_spec, rows_spec, rows_spec, rows_spec],
        out_shape=[jax.ShapeDtypeStruct((n, N_Q_HEADS, t, HEAD_DIM), BF16), slot_sds, slot_sds,
                   rows_sds, rows_sds, rows_sds],
        compiler_params=_cparams("arbitrary", "arbitrary"), name="postproj_prompt",
    )(z, z, z, z, q_norm_g.reshape(1, HEAD_DIM), k_norm_g)


def _conv_finish(b, y, g):
    c = b * y
    inv = lax.rsqrt(jnp.mean(c * c, axis=-1, keepdims=True) + RMS_EPS)
    return ((c * inv) * g).astype(BF16)


def _conv_prompt_body(gb_ref, gc_ref, hc_ref, w_ref, g_ref, o_ref, st_ref, carry_ref):
    tt = gb_ref.shape[0]

    @pl.when(pl.program_id(1) == 0)
    def _():
        carry_ref[...] = jnp.zeros_like(carry_ref)

    u = gc_ref[...] * hc_ref[...]
    prev = carry_ref[...]
    p1, p2 = prev[7:8, :], prev[6:7, :]
    row = lax.broadcasted_iota(jnp.int32, u.shape, 0)
    u1 = jnp.where(row == 0, p1, pltpu.roll(u, 1, axis=0))
    u2 = jnp.where(row == 0, p2, jnp.where(row == 1, p1, pltpu.roll(u, 2, axis=0)))
    y = u2 * w_ref[0:1, :] + u1 * w_ref[1:2, :] + u * w_ref[2:3, :]
    o_ref[...] = _conv_finish(gb_ref[...], y, g_ref[...])
    last = u[tt - 8:tt, :]
    carry_ref[...] = last
    st_ref[0] = last


def conv_prompt(z, n, t, conv_w, gain, tt=256):
    nt = t // tt
    row = lambda b, i: (b * nt + i, 0)
    return pl.pallas_call(
        _conv_prompt_body, grid=(n, nt),
        in_specs=[pl.BlockSpec((tt, CONV_DIM), lambda b, i: (b * nt + i, 0)),
                  pl.BlockSpec((tt, CONV_DIM), lambda b, i: (b * nt + i, 1)),
                  pl.BlockSpec((tt, CONV_DIM), lambda b, i: (b * nt + i, 2)),
                  pl.BlockSpec((CONV_WIDTH, CONV_DIM), lambda b, i: (0, 0)),
                  pl.BlockSpec((1, CONV_DIM), lambda b, i: (0, 0))],
        out_specs=[pl.BlockSpec((tt, CONV_DIM), row),
                   pl.BlockSpec((1, 8, CONV_DIM), lambda b, i: (b, 0, 0))],
        out_shape=[jax.ShapeDtypeStruct((n * t, CONV_DIM), BF16), jax.ShapeDtypeStruct((n, 8, CONV_DIM), F32)],
        scratch_shapes=[pltpu.VMEM((8, CONV_DIM), F32)],
        compiler_params=_cparams("arbitrary", "arbitrary"), name="conv_prompt",
    )(z, z, z, conv_w, gain.reshape(1, CONV_DIM))


def _conv_sample_body(z_ref, pre_ref, w_ref, g_ref, o_ref, st_ref, *, t):
    up = [pre_ref[k] for k in range(CONV_WIDTH - 1)] + [z_ref[1, k] * z_ref[2, k] for k in range(t)]
    for k in range(t):
        y = up[k] * w_ref[0:1, :] + up[k + 1] * w_ref[1:2, :] + up[k + 2] * w_ref[2:3, :]
        o_ref[k] = _conv_finish(z_ref[0, k], y, g_ref[...])
    for k in range(CONV_WIDTH - 1):
        st_ref[k] = up[t + k]


def conv_sample(z, n, t, state, conv_w, gain):
    zt = z[:, :3 * CONV_DIM].reshape(n, t, 3, CONV_DIM).transpose(2, 1, 0, 3)
    whole = lambda shape: pl.BlockSpec(shape, lambda i: (0,) * len(shape))
    out, st = pl.pallas_call(
        functools.partial(_conv_sample_body, t=t), grid=(1,),
        in_specs=[whole((3, t, n, CONV_DIM)), whole((CONV_WIDTH - 1, n, CONV_DIM)),
                  whole((CONV_WIDTH, CONV_DIM)), whole((1, CONV_DIM))],
        out_specs=[whole((t, n, CONV_DIM)), whole((CONV_WIDTH - 1, n, CONV_DIM))],
        out_shape=[jax.ShapeDtypeStruct((t, n, CONV_DIM), BF16),
                   jax.ShapeDtypeStruct((CONV_WIDTH - 1, n, CONV_DIM), F32)],
        compiler_params=_cparams("arbitrary"), name="conv_sample",
    )(zt, state.transpose(1, 0, 2), conv_w, gain.reshape(1, CONV_DIM))
    return out.transpose(1, 0, 2).reshape(n * t, CONV_DIM), st.transpose(1, 0, 2)


def _compress_body(pt_ref, *refs):
    del pt_ref
    pages = refs[:N_PAGES]
    wkv_ref, pe_ref, w2_ref, kg_ref, out_ref = refs[N_PAGES:]
    cpp = PAGE_SIZE // CMP_STRIDE
    x4 = [pages[p][...].reshape(cpp, CMP_STRIDE, N_SLOT, HEAD_DIM) for p in range(N_PAGES)]
    n_rows = N_PAGES * cpp * N_SLOT
    is_key = (lax.broadcasted_iota(jnp.int32, (n_rows, HEAD_DIM), 0) & (N_SLOT - 1)) < N_KV_HEADS

    def split(x):
        return [jnp.where(is_key, x, 0.0).astype(BF16), jnp.where(is_key, 0.0, x).astype(BF16)]

    pieces = []
    for s in range(CMP_STRIDE):
        pieces += split(jnp.concatenate([x4[p][:, s].reshape(cpp * N_SLOT, HEAD_DIM) for p in range(N_PAGES)], axis=0))
    wkv = wkv_ref[...].astype(BF16)
    r = _dot(jnp.concatenate(pieces, axis=1), wkv)
    hpre = r[:, :HEAD_DIM] + pltpu.roll(r[:, HEAD_DIM:], n_rows - N_SLOT, axis=0)
    bias = []
    for v in range(2):
        b = jnp.zeros((8, HEAD_DIM), F32)
        for rr in range(R_CMP):
            pe = jnp.broadcast_to(pe_ref[v, rr:rr + 1, :], (8, wkv.shape[0])).astype(BF16)
            b = b + _dot(pe, wkv[:, rr * HEAD_DIM:(rr + 1) * HEAD_DIM])
        bias.append(b[0:1, :])
    hid = jax.nn.gelu(hpre + jnp.where(is_key, bias[0], bias[1]))
    out = _dot(jnp.concatenate(split(hid), axis=1), w2_ref[...].astype(BF16))
    out_ref[0] = jnp.where(is_key, _head_norm(out, kg_ref[0:1, :]), out)


def compress(pool, page_table, wkv, pe_kv, w2cat, k_norm_g):
    n = page_table.shape[0]
    page_rows = PAGE_SIZE * N_SLOT
    page_spec = lambda p: pl.BlockSpec((page_rows, HEAD_DIM), lambda i, pt, p=p: (pt[i, p], 0))
    const = lambda shape: pl.BlockSpec(shape, lambda i, pt: (0,) * len(shape))
    return pl.pallas_call(
        _compress_body,
        grid_spec=pltpu.PrefetchScalarGridSpec(
            num_scalar_prefetch=1, grid=(n,),
            in_specs=[page_spec(p) for p in range(N_PAGES)]
            + [const(wkv.shape), const(pe_kv.shape), const(w2cat.shape), const(k_norm_g.shape)],
            out_specs=pl.BlockSpec((1, N_CMP_BLK * N_SLOT, HEAD_DIM), lambda i, pt: (i, 0, 0))),
        out_shape=jax.ShapeDtypeStruct((n, N_CMP_BLK * N_SLOT, HEAD_DIM), F32),
        compiler_params=_cparams("arbitrary"), name="compress",
    )(page_table, *([pool] * N_PAGES), wkv, pe_kv, w2cat, k_norm_g)


SEL_ROWS = 40


def _cmp_select_body(q_ref, kvc_ref, cov_ref, o_ref, sel_ref, *, bn, tq, pos_base):
    qt = pl.program_id(1)
    grp = GQA * tq
    pairs = [(b, j) for b in range(bn) for j in range(N_KV_HEADS)]

    def slot(b, c):
        return kvc_ref[b, pl.ds(c, N_CMP_BLK, stride=N_SLOT), :].astype(BF16)

    s = jnp.concatenate(
        [_dot_nt(q_ref[b, j * GQA:(j + 1) * GQA].reshape(grp, HEAD_DIM).astype(BF16), slot(b, j)) for b, j in pairs],
        axis=0) * SCALE
    rows = len(pairs) * grp
    pos = pos_base + qt * tq + (lax.broadcasted_iota(jnp.int32, (rows, N_CMP_BLK), 0) & (tq - 1))
    blk = lax.broadcasted_iota(jnp.int32, (rows, N_CMP_BLK), 1)
    valid = (blk < N_CMP_BLK - 1) & (blk * CMP_STRIDE + (L_CMP - 1) <= pos)
    p = _masked_softmax(s, valid)
    p_bf = p.astype(BF16)
    for i, (b, j) in enumerate(pairs):
        o_ref[b, j * GQA:(j + 1) * GQA] = _dot(p_bf[i * grp:(i + 1) * grp],
                                               slot(b, N_KV_HEADS + j)).reshape(GQA, tq, HEAD_DIM)
    psum = jnp.sum(p.reshape(len(pairs), GQA, tq, N_CMP_BLK), axis=1).reshape(len(pairs) * tq, N_CMP_BLK)
    p_hi = psum.astype(BF16)
    p_lo = (psum - p_hi.astype(F32)).astype(BF16)
    imp_t = _dot_nt(cov_ref[...], p_hi) + _dot_nt(cov_ref[...], p_lo)
    srows = len(pairs) * tq
    imp_t = imp_t[0:SEL_ROWS, :]
    blk = lax.broadcasted_iota(jnp.int32, (SEL_ROWS, srows), 0)
    cur = (pos_base + qt * tq + (lax.broadcasted_iota(jnp.int32, (SEL_ROWS, srows), 1) & (tq - 1))) >> 6
    forced = (blk == 0) | (blk == cur) | (blk == cur - 1)
    score = jnp.where(blk <= cur, imp_t + jnp.where(forced, FORCE_BONUS, 0.0), NEG_INF)
    sel_t = jnp.zeros((SEL_ROWS, srows), F32)
    for _ in range(N_SEL):
        m = jnp.max(score, axis=0, keepdims=True)
        first = jnp.min(jnp.where(score == m, blk, LANE), axis=0, keepdims=True)
        hit = blk == first
        sel_t = jnp.where(hit, 1.0, sel_t)
        score = jnp.where(hit, PICKED, score)
    sel = jnp.transpose(jnp.concatenate([sel_t, jnp.zeros((LANE - SEL_ROWS, srows), F32)], axis=0))
    for i, (b, j) in enumerate(pairs):
        sel_ref[b, j] = sel[i * tq:(i + 1) * tq]


def cmp_select(q, kvc, cover, bn, tq, pos_base):
    n, _, t, _ = q.shape
    assert L_SEL == 64 and tq & (tq - 1) == 0
    return pl.pallas_call(
        functools.partial(_cmp_select_body, bn=bn, tq=tq, pos_base=pos_base), grid=(n // bn, t // tq),
        in_specs=[pl.BlockSpec((bn, N_Q_HEADS, tq, HEAD_DIM), lambda b, i: (b, 0, i, 0)),
                  pl.BlockSpec((bn, N_CMP_BLK * N_SLOT, HEAD_DIM), lambda b, i: (b, 0, 0)),
                  pl.BlockSpec((N_CMP_BLK, LANE), lambda b, i: (0, 0))],
        out_specs=[pl.BlockSpec((bn, N_Q_HEADS, tq, HEAD_DIM), lambda b, i: (b, 0, i, 0)),
                   pl.BlockSpec((bn, N_KV_HEADS, tq, LANE), lambda b, i: (b, 0, i, 0))],
        out_shape=[jax.ShapeDtypeStruct((n, N_Q_HEADS, t, HEAD_DIM), F32),
                   jax.ShapeDtypeStruct((n, N_KV_HEADS, t, LANE), F32)],
        compiler_params=_cparams("arbitrary", "arbitrary"), name="cmp_select",
    )(q, kvc, cover)


SOFTMAX_ROWS = 16


def _attn_prompt_body(*refs, tq, t, selected):
    if selected:
        q_ref, k_ref, v_ref, sel_ref, e_ref, o_ref, s_ref, bias_ref, p_ref, max_ref, sum_ref, acc_ref = refs
    else:
        q_ref, k_ref, v_ref, o_ref, s_ref, bias_ref, p_ref, inv_ref = refs
    qt = pl.program_id(2)
    rows = GQA * tq
    q = q_ref[0].reshape(rows, HEAD_DIM)

    def attend(start, nk):
        k = k_ref[0, 0, pl.ds(start, nk), :]
        v = v_ref[0, 0, pl.ds(start, nk), :]
        s_ref[:, 0:nk] = _dot_nt(q, k)
        qpos = qt * tq + lax.broadcasted_iota(jnp.int32, (tq, nk), 0)
        kpos = start + lax.broadcasted_iota(jnp.int32, (tq, nk), 1)
        ok = kpos <= qpos
        if selected:
            ok = ok & (_dot(sel_ref[0, 0].astype(BF16), e_ref[:, 0:nk]) > 0.5)
        else:
            ok = ok & (kpos > qpos - WINDOW)
        bias_ref[:, 0:nk] = jnp.where(ok, 0.0, NEG_INF)

        for r0 in range(0, rows, SOFTMAX_ROWS):
            b0 = r0 % tq
            sb = s_ref[r0:r0 + SOFTMAX_ROWS, 0:nk] * (SCALE * LOG2_E) + bias_ref[b0:b0 + SOFTMAX_ROWS, 0:nk]
            m = jnp.max(sb, axis=-1, keepdims=True)
            e = jnp.exp2(sb - m)
            p_ref[r0:r0 + SOFTMAX_ROWS, 0:nk] = e.astype(BF16)
            total = jnp.maximum(jnp.sum(e, axis=-1, keepdims=True), TINY)
            inv_ref[r0:r0 + SOFTMAX_ROWS, :] = jnp.broadcast_to(jnp.where(m > 0.5 * NEG_INF, 1.0 / total, 0.0),
                                                                (SOFTMAX_ROWS, HEAD_DIM))
        o_ref[0] = (_dot(p_ref[:, 0:nk], v) * inv_ref[...]).reshape(GQA, tq, HEAD_DIM)

    if selected:
        ch = WINDOW
        per = ch // tq
        lanes = ch // HEAD_DIM
        max_ref[...] = jnp.full(max_ref.shape, NEG_INF, F32)
        sum_ref[...] = jnp.zeros(sum_ref.shape, F32)
        acc_ref[...] = jnp.zeros(acc_ref.shape, F32)
        for c in range(t // ch):
            @pl.when(c <= qt // per)
            def _(c=c):
                k = k_ref[0, 0, c * ch:(c + 1) * ch, :]
                v = v_ref[0, 0, c * ch:(c + 1) * ch, :]
                s_ref[...] = _dot_nt(q, k)
                qpos = qt * tq + lax.broadcasted_iota(jnp.int32, (tq, ch), 0)
                kpos = c * ch + lax.broadcasted_iota(jnp.int32, (tq, ch), 1)
                ok = (kpos <= qpos) & (_dot(sel_ref[0, 0].astype(BF16), e_ref[:, c * ch:(c + 1) * ch]) > 0.5)
                bias_ref[...] = jnp.where(ok, 0.0, NEG_INF)
                for r0 in range(0, rows, SOFTMAX_ROWS):
                    b0 = r0 % tq
                    rs = slice(r0, r0 + SOFTMAX_ROWS)
                    sb = s_ref[rs, :] * (SCALE * LOG2_E) + bias_ref[b0:b0 + SOFTMAX_ROWS, :]
                    m_old = max_ref[rs, :]
                    m_new = jnp.maximum(m_old, jnp.max(sb, axis=-1, keepdims=True))
                    alpha = jnp.exp2(m_old - m_new)
                    e = jnp.exp2(sb - jnp.concatenate([m_new] * lanes, axis=1))
                    p_ref[rs, :] = e.astype(BF16)
                    sum_ref[rs, :] = alpha * sum_ref[rs, :] + jnp.sum(e, axis=-1, keepdims=True)
                    acc_ref[rs, :] = alpha * acc_ref[rs, :]
                    max_ref[rs, :] = m_new
                acc_ref[...] += _dot(p_ref[...], v)
        inv = jnp.where(max_ref[...] > 0.5 * NEG_INF, 1.0 / jnp.maximum(sum_ref[...], TINY), 0.0)
        o_ref[0] = (acc_ref[...] * inv).reshape(GQA, tq, HEAD_DIM)
    else:
        attend(pl.multiple_of(jnp.maximum(qt * tq - WINDOW, 0), tq), WINDOW + tq)


def attn_prompt(q, kv, sel, expand, tq=128):
    n, _, t, _ = q.shape
    selected = sel is not None
    in_specs = [pl.BlockSpec((1, GQA, tq, HEAD_DIM), lambda b, j, i: (b, j, i, 0)),
                pl.BlockSpec((1, 1, t, HEAD_DIM), lambda b, j, i: (b, j, 0, 0)),
                pl.BlockSpec((1, 1, t, HEAD_DIM), lambda b, j, i: (b, N_KV_HEADS + j, 0, 0))]
    args = [q, kv, kv]
    if selected:
        in_specs += [pl.BlockSpec((1, 1, tq, LANE), lambda b, j, i: (b, j, i, 0)),
                     pl.BlockSpec(expand.shape, lambda b, j, i: (0, 0))]
        args += [sel, expand]
    nk_max = WINDOW if selected else WINDOW + tq
    stats = [pltpu.VMEM((GQA * tq, HEAD_DIM), F32)] * (3 if selected else 1)
    return pl.pallas_call(
        functools.partial(_attn_prompt_body, tq=tq, t=t, selected=selected), grid=(n, N_KV_HEADS, t // tq),
        in_specs=in_specs,
        out_specs=pl.BlockSpec((1, GQA, tq, HEAD_DIM), lambda b, j, i: (b, j, i, 0)),
        out_shape=jax.ShapeDtypeStruct((n, N_Q_HEADS, t, HEAD_DIM), F32),
        scratch_shapes=[pltpu.VMEM((GQA * tq, nk_max), F32), pltpu.VMEM((tq, nk_max), F32),
                        pltpu.VMEM((GQA * tq, nk_max), BF16)] + stats,
        compiler_params=_cparams("arbitrary", "arbitrary", "arbitrary"),
        name="attn_prompt_sel" if selected else "attn_prompt_win",
    )(*args)


T_PAD = 8


def _pad_keys(x):
    return jnp.concatenate([x, jnp.zeros((PAGE_SIZE - x.shape[0], x.shape[1]), x.dtype)], axis=0)


SEQS_PER_STEP = 2


def _attn_sample_sel_body(pt_ref, *refs, t_real):
    del pt_ref
    n_page_refs = SEQS_PER_STEP * N_PAGES
    q_ref, sel_ref, new_ref, e_ref, o_ref = refs[n_page_refs:]
    rows = GQA * T_PAD
    tok = lax.broadcasted_iota(jnp.int32, (rows, PAGE_SIZE), 0) & (T_PAD - 1)
    col = lax.broadcasted_iota(jnp.int32, (rows, PAGE_SIZE), 1)
    for b, j in [(b, j) for b in range(SEQS_PER_STEP) for j in range(N_KV_HEADS)]:
        pages = refs[b * N_PAGES:(b + 1) * N_PAGES]
        kc0, vc0 = j * HEAD_DIM, KV_DIM + j * HEAD_DIM
        q = q_ref[b, j * GQA:(j + 1) * GQA].reshape(rows, HEAD_DIM).astype(BF16)
        k_rows = pl.ds(j, PAGE_SIZE, stride=N_SLOT)
        v_rows = pl.ds(N_KV_HEADS + j, PAGE_SIZE, stride=N_SLOT)
        parts = [_dot_nt(q, pages[p][k_rows, :].astype(BF16)) for p in range(N_PAGES)]
        parts.append(_dot_nt(q, _pad_keys(new_ref[b, :, kc0:kc0 + HEAD_DIM]).astype(BF16)))
        s = jnp.concatenate(parts, axis=1) * SCALE
        sel = sel_ref[b, j]
        sel4 = jnp.concatenate([sel] * GQA, axis=0)
        picked = _dot(sel4.astype(BF16), e_ref[...])
        new_blk = PAST_LEN // L_SEL
        new_ok = (col < t_real) & (col <= tok)
        new_picked = jnp.where(new_ok, _lane_pick(sel4, new_blk), 0.0)
        mask = jnp.concatenate([picked, new_picked], axis=1) > 0.5
        p = _masked_softmax(s, mask).astype(BF16)
        o = _dot(p[:, PAST_LEN:], _pad_keys(new_ref[b, :, vc0:vc0 + HEAD_DIM]).astype(BF16))
        for pg in range(N_PAGES):
            o = o + _dot(p[:, pg * PAGE_SIZE:(pg + 1) * PAGE_SIZE], pages[pg][v_rows, :].astype(BF16))
        o_ref[b, j * GQA:(j + 1) * GQA] = o.reshape(GQA, T_PAD, HEAD_DIM)


def attn_sample_sel(pool, page_table, q, sel, new_rows, expand, t_real):
    n = page_table.shape[0]
    cols = new_rows.shape[-1]
    bs = SEQS_PER_STEP
    assert n % bs == 0
    page_spec = lambda b, p: pl.BlockSpec((PAGE_SIZE * N_SLOT, HEAD_DIM), lambda i, pt, b=b, p=p: (pt[bs * i + b, p], 0))
    return pl.pallas_call(
        functools.partial(_attn_sample_sel_body, t_real=t_real),
        grid_spec=pltpu.PrefetchScalarGridSpec(
            num_scalar_prefetch=1, grid=(n // bs,),
            in_specs=[page_spec(b, p) for b in range(bs) for p in range(N_PAGES)]
            + [pl.BlockSpec((bs, N_Q_HEADS, T_PAD, HEAD_DIM), lambda i, pt: (i, 0, 0, 0)),
               pl.BlockSpec((bs, N_KV_HEADS, T_PAD, LANE), lambda i, pt: (i, 0, 0, 0)),
               pl.BlockSpec((bs, T_PAD, cols), lambda i, pt: (i, 0, 0)),
               pl.BlockSpec(expand.shape, lambda i, pt: (0, 0))],
            out_specs=pl.BlockSpec((bs, N_Q_HEADS, T_PAD, HEAD_DIM), lambda i, pt: (i, 0, 0, 0))),
        out_shape=jax.ShapeDtypeStruct((n, N_Q_HEADS, T_PAD, HEAD_DIM), F32),
        compiler_params=_cparams("arbitrary"), name="attn_sample_sel",
    )(page_table, *([pool] * (bs * N_PAGES)), q, sel, new_rows, expand)


def _attn_sample_win_body(win_ref, q_ref, new_ref, newrows_ref, o_ref, wout_ref, *, t_real):
    rows = GQA * T_PAD
    buf_rows = win_ref.shape[0] // SEQS_PER_STEP
    w_buf = buf_rows // N_SLOT
    tok_o = lax.broadcasted_iota(jnp.int32, (rows, w_buf), 0) & (T_PAD - 1)
    col_o = lax.broadcasted_iota(jnp.int32, (rows, w_buf), 1)
    tok_n = lax.broadcasted_iota(jnp.int32, (rows, PAGE_SIZE), 0) & (T_PAD - 1)
    col_n = lax.broadcasted_iota(jnp.int32, (rows, PAGE_SIZE), 1)
    old_ok = jnp.where(col_o + (WINDOW - w_buf) > tok_o, 1.0, 0.0)
    new_ok = jnp.where((col_n < t_real) & (col_n <= tok_n), 1.0, 0.0)
    mask = jnp.concatenate([old_ok, new_ok], axis=1) > 0.5
    for b, j in [(b, j) for b in range(SEQS_PER_STEP) for j in range(N_KV_HEADS)]:
        kc0, vc0 = j * HEAD_DIM, KV_DIM + j * HEAD_DIM
        base = b * buf_rows
        q = q_ref[b, j * GQA:(j + 1) * GQA].reshape(rows, HEAD_DIM).astype(BF16)
        k_new = _pad_keys(new_ref[b, :, kc0:kc0 + HEAD_DIM]).astype(BF16)
        v_new = _pad_keys(new_ref[b, :, vc0:vc0 + HEAD_DIM]).astype(BF16)
        k_old = win_ref[pl.ds(base + j, w_buf, stride=N_SLOT), :].astype(BF16)
        v_old = win_ref[pl.ds(base + N_KV_HEADS + j, w_buf, stride=N_SLOT), :].astype(BF16)
        s = jnp.concatenate([_dot_nt(q, k_old), _dot_nt(q, k_new)], axis=1) * SCALE
        p = _masked_softmax(s, mask).astype(BF16)
        o = _dot(p[:, :w_buf], v_old) + _dot(p[:, w_buf:], v_new)
        o_ref[b, j * GQA:(j + 1) * GQA] = o.reshape(GQA, T_PAD, HEAD_DIM)
    keep = (w_buf - t_real) * N_SLOT
    new_n = t_real * N_SLOT
    for b in range(SEQS_PER_STEP):
        base = b * buf_rows
        wout_ref[base:base + keep, :] = win_ref[base + new_n:base + buf_rows, :]
        wout_ref[base + keep:base + buf_rows, :] = newrows_ref[b * new_n:(b + 1) * new_n, :]


def attn_sample_win(win, q, new_rows, new_cache_rows, t_real):
    n = q.shape[0]
    cols = new_rows.shape[-1]
    bs = SEQS_PER_STEP
    assert n % bs == 0
    buf_rows = bs * (win.shape[0] // n)
    return pl.pallas_call(
        functools.partial(_attn_sample_win_body, t_real=t_real), grid=(n // bs,),
        in_specs=[pl.BlockSpec((buf_rows, HEAD_DIM), lambda i: (i, 0)),
                  pl.BlockSpec((bs, N_Q_HEADS, T_PAD, HEAD_DIM), lambda i: (i, 0, 0, 0)),
                  pl.BlockSpec((bs, T_PAD, cols), lambda i: (i, 0, 0)),
                  pl.BlockSpec((bs * t_real * N_SLOT, HEAD_DIM), lambda i: (i, 0))],
        out_specs=[pl.BlockSpec((bs, N_Q_HEADS, T_PAD, HEAD_DIM), lambda i: (i, 0, 0, 0)),
                   pl.BlockSpec((buf_rows, HEAD_DIM), lambda i: (i, 0))],
        out_shape=[jax.ShapeDtypeStruct((n, N_Q_HEADS, T_PAD, HEAD_DIM), F32),
                   jax.ShapeDtypeStruct(win.shape, F32)],
        compiler_params=_cparams("arbitrary"), name="attn_sample_win",
    )(win, q, new_rows, new_cache_rows)


def _combine_body(oc_ref, os_ref, ow_ref, gl_ref, g_ref, out_ref, *, rows):
    gates = jax.nn.sigmoid(gl_ref[...])
    outs = []
    sq = jnp.zeros((rows, 1), F32)
    for h in range(N_Q_HEADS):
        a = (_lane_pick(gates, h) * oc_ref[:, h].reshape(rows, HEAD_DIM)
             + _lane_pick(gates, N_Q_HEADS + h) * os_ref[:, h].reshape(rows, HEAD_DIM)
             + _lane_pick(gates, 2 * N_Q_HEADS + h) * ow_ref[:, h].reshape(rows, HEAD_DIM))
        outs.append(a)
        sq = sq + jnp.sum(a * a, axis=-1, keepdims=True)
    inv = lax.rsqrt(sq * (1.0 / ATTN_DIM) + RMS_EPS)
    for h in range(N_Q_HEADS):
        sl = slice(h * HEAD_DIM, (h + 1) * HEAD_DIM)
        out_ref[:, sl] = ((outs[h] * inv) * g_ref[:, sl]).astype(out_ref.dtype)


def combine(o_cmp, o_sel, o_win, gate_logits, gain, bn, tt):
    n, _, t, _ = o_cmp.shape
    nt = t // tt
    rows = bn * tt
    o_spec = pl.BlockSpec((bn, N_Q_HEADS, tt, HEAD_DIM), lambda b, i: (b, 0, i, 0))
    return pl.pallas_call(
        functools.partial(_combine_body, rows=rows), grid=(n // bn, nt),
        in_specs=[o_spec, o_spec, o_spec,
                  pl.BlockSpec((rows, LANE), lambda b, i: (b * nt + i, 0)),
                  pl.BlockSpec((1, ATTN_DIM), lambda b, i: (0, 0))],
        out_specs=pl.BlockSpec((rows, ATTN_DIM), lambda b, i: (b * nt + i, 0)),
        out_shape=jax.ShapeDtypeStruct((n * t, ATTN_DIM), BF16),
        compiler_params=_cparams("arbitrary", "arbitrary"), name="combine",
    )(o_cmp, o_sel, o_win, gate_logits, gain.reshape(1, ATTN_DIM))


TOKEN_CHUNKS = D_MODEL // LANE
TOKEN_PITCH = 40


def _router_body(h_ref, g_ref, w_ref, b_ref, xrows_ref, ids_ref, wts_ref):
    tm = h_ref.shape[0]
    h = h_ref[...]
    x = (h * lax.rsqrt(jnp.mean(h * h, axis=-1, keepdims=True) + RMS_EPS)) * g_ref[...]
    for c in range(TOKEN_PITCH):
        piece = x[:, c * LANE:(c + 1) * LANE] if c < TOKEN_CHUNKS else jnp.zeros((tm, LANE), F32)
        xrows_ref[pl.ds(c, tm, stride=TOKEN_PITCH), :] = piece
    logits = _dot(x.astype(BF16), w_ref[...].astype(BF16)) + b_ref[...]
    lane = lax.broadcasted_iota(jnp.int32, logits.shape, 1)
    is_grp = (lane >= N_EXPERTS) & (lane < N_EXPERTS + N_GROUPS)
    gl = jnp.where(is_grp, logits, NEG_INF)
    ge = jnp.where(is_grp, jnp.exp(gl - jnp.max(gl, axis=-1, keepdims=True)), 0.0)
    p_grp = ge / jnp.sum(ge, axis=-1, keepdims=True)
    g_val = jnp.max(p_grp, axis=-1, keepdims=True)
    g_idx = jnp.min(jnp.where(is_grp & (p_grp == g_val), lane, 2 * LANE), axis=-1, keepdims=True) - N_EXPERTS
    lo = g_idx * EXPERTS_PER_GROUP
    in_grp = (lane >= lo) & (lane < lo + EXPERTS_PER_GROUP)
    el = jnp.where(in_grp, logits, NEG_INF)
    ee = jnp.where(in_grp, jnp.exp(el - jnp.max(el, axis=-1, keepdims=True)), 0.0)
    p_e = ee / jnp.sum(ee, axis=-1, keepdims=True)
    cand = jnp.where(in_grp, p_e, -1.0)
    e1 = jnp.max(cand, axis=-1, keepdims=True)
    i1 = jnp.min(jnp.where(cand == e1, lane, 2 * LANE), axis=-1, keepdims=True)
    cand = jnp.where(lane == i1, -1.0, cand)
    e2 = jnp.max(cand, axis=-1, keepdims=True)
    i2 = jnp.min(jnp.where(cand == e2, lane, 2 * LANE), axis=-1, keepdims=True)
    tot = e1 + e2
    ids_ref[...] = jnp.where(lane == 0, i1, jnp.where(lane == 1, i2, 0))
    wts_ref[...] = jnp.where(lane == 0, g_val * e1 / tot, jnp.where(lane == 1, g_val * e2 / tot, 0.0))


def router(h, gain, w_router, b_router, tm=256):
    m, d = h.shape
    out_spec = pl.BlockSpec((tm, LANE), lambda i: (i, 0))
    return pl.pallas_call(
        _router_body, grid=(m // tm,),
        in_specs=[pl.BlockSpec((tm, d), lambda i: (i, 0)),
                  pl.BlockSpec((1, d), lambda i: (0, 0)),
                  pl.BlockSpec((d, LANE), lambda i: (0, 0)),
                  pl.BlockSpec((1, LANE), lambda i: (0, 0))],
        out_specs=[pl.BlockSpec((tm * TOKEN_PITCH, LANE), lambda i: (i, 0)), out_spec, out_spec],
        out_shape=[jax.ShapeDtypeStruct((m * TOKEN_PITCH, LANE), F32),
                   jax.ShapeDtypeStruct((m, LANE), jnp.int32), jax.ShapeDtypeStruct((m, LANE), F32)],
        compiler_params=_cparams("arbitrary"), name="router",
    )(h, gain.reshape(1, d), w_router, b_router)


def _gather_rows(idx_ref, n_rows, src_hbm, dst, sem, wait):
    def copy(r, src):
        return pltpu.make_async_copy(src_hbm.at[pl.ds(src, 1), :], dst.at[pl.ds(r, 1), :], sem)

    if wait:
        def body(r, c):
            copy(r, 0).wait()
            return c
        lax.fori_loop(0, n_rows, body, 0, unroll=8)
    else:
        for r in range(n_rows):
            copy(r, idx_ref[0, 0, r]).start()


def _gather_tokens(idx_ref, tokens, src_hbm, dst, base, sem, wait):
    def copy(r, tok):
        return pltpu.make_async_copy(src_hbm.at[pl.ds(pl.multiple_of(tok * TOKEN_PITCH, 8), TOKEN_CHUNKS), :],
                                     dst.at[pl.ds(pl.multiple_of(base + r * TOKEN_PITCH, 8), TOKEN_CHUNKS), :], sem)

    if wait:
        def body(r, c):
            copy(r, 0).wait()
            return c
        lax.fori_loop(tokens.start, tokens.stop, body, 0, unroll=8)
    else:
        for r in tokens:
            copy(r, idx_ref[0, 0, r]).start()


def _moe_ffn_body(te_ref, used_ref, cur_ref, nxt_ref, x_hbm, rw_ref, wg_ref, wu_ref, wd_ref, ys_ref, xbuf, sem, *, tm):
    del te_ref
    i = pl.program_id(0)
    n_used = used_ref[0]
    slot = lax.rem(i, 2)
    half = tm * TOKEN_PITCH
    here, other = slot * half, (1 - slot) * half

    @pl.when(i == 0)
    def _():
        _gather_tokens(cur_ref, range(tm), x_hbm, xbuf, 0, sem.at[0], wait=False)

    @pl.when(i < n_used)
    def _():
        _gather_tokens(cur_ref, range(tm), x_hbm, xbuf, here, sem.at[slot], wait=True)

    @pl.when(i + 1 < n_used)
    def _():
        _gather_tokens(nxt_ref, range(tm), x_hbm, xbuf, other, sem.at[1 - slot], wait=False)

    @pl.when(i < n_used)
    def _():
        x = jnp.concatenate([xbuf[pl.ds(here + c, tm, stride=TOKEN_PITCH), :].astype(BF16)
                             for c in range(TOKEN_CHUNKS)], axis=1)
        hid = jax.nn.silu(_dot(x, wg_ref[0])) * _dot(x, wu_ref[0])
        gate = jnp.concatenate([rw_ref[...]] * (hid.shape[1] // LANE), axis=1)
        ys_ref[...] = _dot((hid * gate).astype(BF16), wd_ref[0])

    @pl.when(i >= n_used)
    def _():
        ys_ref[...] = jnp.zeros_like(ys_ref)


def moe_ffn(x_rows, tile_expert, n_used, row_token, row_weight, w_gate, w_up, w_down, tm):
    n_tiles = tile_expert.shape[0]
    d = TOKEN_CHUNKS * LANE
    f = w_gate.shape[-1]
    smem_rows = lambda fn: pl.BlockSpec((1, 1, tm), fn, memory_space=pltpu.SMEM)
    return pl.pallas_call(
        functools.partial(_moe_ffn_body, tm=tm),
        grid_spec=pltpu.PrefetchScalarGridSpec(
            num_scalar_prefetch=2, grid=(n_tiles,),
            in_specs=[smem_rows(lambda i, te, nu: (i, 0, 0)),
                      smem_rows(lambda i, te, nu: (jnp.minimum(i + 1, n_tiles - 1), 0, 0)),
                      pl.BlockSpec(memory_space=pl.ANY),
                      pl.BlockSpec((tm, LANE), lambda i, te, nu: (i, 0)),
                      pl.BlockSpec((1, d, f), lambda i, te, nu: (te[i], 0, 0)),
                      pl.BlockSpec((1, d, f), lambda i, te, nu: (te[i], 0, 0)),
                      pl.BlockSpec((1, f, d), lambda i, te, nu: (te[i], 0, 0))],
            out_specs=pl.BlockSpec((tm, d), lambda i, te, nu: (i, 0)),
            scratch_shapes=[pltpu.VMEM((2 * tm * TOKEN_PITCH, LANE), F32), pltpu.SemaphoreType.DMA((2,))]),
        out_shape=jax.ShapeDtypeStruct((n_tiles * tm, d), F32),
        compiler_params=_cparams("arbitrary"), name="moe_ffn",
    )(tile_expert, n_used, row_token, row_token, x_rows, row_weight, w_gate, w_up, w_down)


def _moe_combine_body(cur_ref, nxt_ref, ys_hbm, h_ref, ya_ref, yb_ref, buf, sem, *, tiles_a):
    i = pl.program_id(0)
    n = pl.num_programs(0)
    rows = buf.shape[1]
    tm = h_ref.shape[0]
    slot = lax.rem(i, 2)

    @pl.when(i == 0)
    def _():
        _gather_rows(cur_ref, rows, ys_hbm, buf.at[0], sem.at[0], wait=False)

    _gather_rows(cur_ref, rows, ys_hbm, buf.at[slot], sem.at[slot], wait=True)
    _gather_rows(nxt_ref, rows, ys_hbm, buf.at[1 - slot], sem.at[1 - slot], wait=False)
    y = h_ref[...] + buf[slot, 0:tm, :] + buf[slot, tm:rows, :]

    @pl.when(i == n - 1)
    def _():
        _gather_rows(cur_ref, rows, ys_hbm, buf.at[1 - slot], sem.at[1 - slot], wait=True)

    @pl.when(i < tiles_a)
    def _():
        ya_ref[...] = y

    @pl.when(i >= tiles_a)
    def _():
        yb_ref[...] = y


def moe_combine(ys, pair_row, h, rows_a, tm=128):
    m, d = h.shape
    n_tiles = m // tm
    tiles_a = rows_a // tm
    assert rows_a % tm == 0 and 0 < tiles_a < n_tiles
    smem_rows = lambda fn: pl.BlockSpec((1, 1, 2 * tm), fn, memory_space=pltpu.SMEM)
    return pl.pallas_call(
        functools.partial(_moe_combine_body, tiles_a=tiles_a), grid=(n_tiles,),
        in_specs=[smem_rows(lambda i: (i, 0, 0)),
                  smem_rows(lambda i: (jnp.minimum(i + 1, n_tiles - 1), 0, 0)),
                  pl.BlockSpec(memory_space=pl.ANY),
                  pl.BlockSpec((tm, d), lambda i: (i, 0))],
        out_specs=[pl.BlockSpec((tm, d), lambda i: (jnp.minimum(i, tiles_a - 1), 0)),
                   pl.BlockSpec((tm, d), lambda i: (jnp.maximum(i - tiles_a, 0), 0))],
        out_shape=[jax.ShapeDtypeStruct((rows_a, d), F32), jax.ShapeDtypeStruct((m - rows_a, d), F32)],
        scratch_shapes=[pltpu.VMEM((2, 2 * tm, d), F32), pltpu.SemaphoreType.DMA((2,))],
        compiler_params=_cparams("arbitrary"), name="moe_combine",
    )(pair_row, pair_row, ys, h)


def moe_routed(xt, ids, wts, h, w_gate, w_up, w_down, tm, rows_a):
    m = h.shape[0]
    n_pair = 2 * m
    n_tiles = (n_pair + N_EXPERTS * (tm - 1)) // tm + 1
    flat_e = ids[:, :2].reshape(n_pair)
    flat_w = wts[:, :2].reshape(n_pair)
    order = jnp.argsort(flat_e, stable=True).astype(jnp.int32)
    rank = jnp.argsort(order).astype(jnp.int32)
    counts = jnp.sum(flat_e[:, None] == jnp.arange(N_EXPERTS, dtype=jnp.int32)[None, :], axis=0, dtype=jnp.int32)
    padded = ((counts + tm - 1) // tm) * tm
    pad_end = jnp.cumsum(padded)
    pad_start = pad_end - padded
    start = jnp.cumsum(counts) - counts
    tile_start = jnp.arange(n_tiles, dtype=jnp.int32) * tm
    tile_expert = jnp.minimum(jnp.sum(tile_start[:, None] >= pad_end[None, :], axis=1), N_EXPERTS - 1).astype(jnp.int32)
    row_e = jnp.repeat(tile_expert, tm)
    offs = jnp.arange(n_tiles * tm, dtype=jnp.int32) - pad_start[row_e]
    used = offs < counts[row_e]
    src = order[jnp.clip(start[row_e] + offs, 0, n_pair - 1)]
    row_token = jnp.where(used, src // 2, 0)
    row_weight = jnp.where(used, flat_w[src], 0.0)
    pair_row = (pad_start[flat_e] + rank - start[flat_e]).reshape(m, 2)
    n_used = (pad_end[N_EXPERTS - 1:] // tm).astype(jnp.int32)
    ys = moe_ffn(xt, tile_expert, n_used, row_token.reshape(n_tiles, 1, tm),
                 jnp.broadcast_to(row_weight[:, None], (n_tiles * tm, LANE)), w_gate, w_up, w_down, tm)
    tc = 128
    pair_tiles = pair_row.reshape(m // tc, tc, 2).transpose(0, 2, 1).reshape(m // tc, 1, 2 * tc)
    return moe_combine(ys, pair_tiles, h, rows_a, tm=tc)


def _cover_matrix(nsb):
    i = np.arange(N_CMP_BLK)[:, None]
    j = np.arange(LANE)[None, :]
    m = np.zeros((N_CMP_BLK, LANE), np.float32)
    for a in range(L_SEL // CMP_STRIDE):
        for c in range(R_CMP):
            m += (i == (L_SEL // CMP_STRIDE) * j + a - c)
    m[N_CMP_BLK - 1:, :] = 0.0
    m[:, nsb:] = 0.0
    assert nsb <= SEL_ROWS
    return jnp.asarray(m.T, BF16)


def _expand_matrix(n_keys):
    b = np.arange(LANE)[:, None]
    k = np.arange(n_keys)[None, :]
    return jnp.asarray((k // L_SEL == b).astype(np.float32), BF16)


def _to_heads(q, n, t):
    return q.reshape(n, t, N_Q_HEADS, HEAD_DIM).transpose(0, 2, 1, 3)


def kernel(x_prompt, x_sample, cache_cmp_kv, cache_sel_kv, state_win_kv, state_conv, page_table, norm_mix_g, w_in,
           conv_w, q_norm_g, k_norm_g, phi_pe, phi_w1, phi_w2, out_norm_g, w_out, norm_ffn_g, w_group_router,
           b_group_router, w_expert_router, b_expert_router, w_gate, w_up, w_down):
    n_p, t_p, d = x_prompt.shape
    n_s, t_s, _ = x_sample.shape
    assert w_in.shape[0] == 1 and t_s < CMP_STRIDE and t_s <= T_PAD and t_p % WINDOW == 0
    kv_cols = 2 * KV_DIM
    w_in_t = jnp.transpose(w_in[0])
    w_in_gate_t = jnp.pad(w_in_t[Z_MAIN:], ((0, LANE - 3 * N_Q_HEADS), (0, 0)))
    w_o = w_out
    wkv = phi_w1[0].reshape(2, R_CMP, CMP_STRIDE, HEAD_DIM, HEAD_DIM).transpose(2, 0, 3, 1, 4)
    wkv = wkv.reshape(CMP_STRIDE * 2 * HEAD_DIM, R_CMP * HEAD_DIM)
    pe5 = phi_pe[0].reshape(2, R_CMP, CMP_STRIDE, 1, HEAD_DIM)
    pe_kv = (pe5 * jnp.eye(2, dtype=F32).reshape(2, 1, 1, 2, 1)).reshape(2, R_CMP, CMP_STRIDE * 2 * HEAD_DIM)
    w2cat = phi_w2[0].reshape(2 * HEAD_DIM, HEAD_DIM)
    w_router = jnp.pad(jnp.concatenate([w_expert_router[0], w_group_router[0]], axis=1),
                       ((0, 0), (0, LANE - N_EXPERTS - N_GROUPS)))
    b_router = jnp.pad(jnp.concatenate([b_expert_router[0], b_group_router[0]]),
                       (0, LANE - N_EXPERTS - N_GROUPS)).reshape(1, LANE)
    wg_bf, wu_bf, wd_bf = w_gate[0].astype(BF16), w_up[0].astype(BF16), w_down[0].astype(BF16)

    def project(x2d):
        xn = rmsnorm_cast(x2d, norm_mix_g[0])
        z = matmul([([xn], None)], [(w_in_t, 0)], tm=1024, tn=512, w_transposed=True)
        gate_logits = matmul([([xn], None)], [(w_in_gate_t, 0)], tn=LANE, w_transposed=True)
        return z, gate_logits

    m_p = n_p * t_p

    xp = x_prompt.reshape(n_p * t_p, d)
    z, glog = project(xp)
    qh, ks_h, kw_h, kvc, kvs, kvw = postproj_prompt(z, n_p, t_p, q_norm_g[0], k_norm_g[0])
    conv_out, conv_last = conv_prompt(z, n_p, t_p, conv_w[0], out_norm_g[0][:CONV_DIM])
    ident = jnp.arange(n_p * (t_p // PAGE_SIZE), dtype=jnp.int32).reshape(n_p, t_p // PAGE_SIZE)
    kv_cmp = compress(kvc, ident, wkv, pe_kv, w2cat, k_norm_g[0])
    o_cmp, sel = cmp_select(qh, kv_cmp, _cover_matrix(t_p // L_SEL), bn=1, tq=256, pos_base=0)
    o_sel = attn_prompt(qh, ks_h, sel, _expand_matrix(t_p))
    o_win = attn_prompt(qh, kw_h, None, None)
    attn_out = combine(o_cmp, o_sel, o_win, glog, out_norm_g[0][CONV_DIM:], bn=1, tt=256)
    mixed_p = ([conv_out, attn_out], xp)
    kv_shape = (1, n_p, t_p, 2, N_KV_HEADS, HEAD_DIM)
    w_keep = min(WINDOW, t_p)
    prompt_win = kvw.reshape(kv_shape)[:, :, t_p - w_keep:]
    prompt_conv = conv_last[:, 8 - (CONV_WIDTH - 1):, :][None]

    xs = x_sample.reshape(n_s * t_s, d)
    z, glog = project(xs)
    q, kvc_s, kvs_s, kvw_s = postproj(z, q_norm_g[0], k_norm_g[0])
    conv_out, conv_state = conv_sample(z, n_s, t_s, state_conv[0], conv_w[0], out_norm_g[0][:CONV_DIM])
    pad_t = lambda a: jnp.pad(a, ((0, 0), (0, T_PAD - t_s), (0, 0)))
    qh = jnp.pad(_to_heads(q, n_s, t_s), ((0, 0), (0, 0), (0, T_PAD - t_s), (0, 0)))
    pool_cmp = cache_cmp_kv[0].reshape(-1, HEAD_DIM)
    pool_sel = cache_sel_kv[0].reshape(-1, HEAD_DIM)
    kv_cmp = compress(pool_cmp, page_table, wkv, pe_kv, w2cat, k_norm_g[0])
    o_cmp, sel = cmp_select(qh, kv_cmp, _cover_matrix(PAST_LEN // L_SEL + 1), bn=16, tq=T_PAD, pos_base=PAST_LEN)
    o_sel = attn_sample_sel(pool_sel, page_table, qh, sel, pad_t(kvs_s.reshape(n_s, t_s, kv_cols)),
                            _expand_matrix(PAST_LEN), t_s)
    win = state_win_kv[0].reshape(-1, HEAD_DIM)
    o_win, win_new = attn_sample_win(win, qh, pad_t(kvw_s.reshape(n_s, t_s, kv_cols)),
                                     kvw_s.reshape(-1, HEAD_DIM), t_s)
    glog_pad = pad_t(glog.reshape(n_s, t_s, LANE)).reshape(n_s * T_PAD, LANE)
    attn_out = combine(o_cmp, o_sel, o_win, glog_pad, out_norm_g[0][CONV_DIM:], bn=32, tt=T_PAD)
    attn_out = attn_out.reshape(n_s, T_PAD, ATTN_DIM)[:, :t_s].reshape(n_s * t_s, ATTN_DIM)
    s_shape = (1, n_s, t_s, 2, N_KV_HEADS, HEAD_DIM)

    h_all = matmul([mixed_p, ([conv_out, attn_out], xs)], [(w_o, 0), (w_o, 1)])
    x_rows, ids, wts = router(h_all, norm_ffn_g[0], w_router, b_router)
    y_prompt, y_sample = moe_routed(x_rows, ids, wts, h_all, wg_bf, wu_bf, wd_bf, 256, m_p)

    return (y_prompt.reshape(n_p, t_p, d), y_sample.reshape(n_s, t_s, d), kvc.reshape(kv_shape), kvs.reshape(kv_shape), prompt_win, prompt_conv,
            kvc_s.reshape(s_shape), kvs_s.reshape(s_shape),
            win_new.reshape(1, n_s, -1, 2, N_KV_HEADS, HEAD_DIM), conv_state[None])
```
